```python
import math
import jax, jax.numpy as jnp
from jax import lax
import numpy as np

D_MODEL = 1024
BATCH = 2
SEQ = 8192
DEPTH = 2
DEC_BATCH = 32
DEC_SEQ = 4
PAST_LEN = 8192
PAGE_SIZE = 128

N_MIXERS = 2
N_ATTN_LAYERS = (DEPTH + 1) // 2
N_GLA_LAYERS = DEPTH // 2
DEEPNORM_ALPHA = (2.0 * DEPTH) ** 0.25
DEEPNORM_BETA = (8.0 * DEPTH) ** -0.25
LN_EPS = 1e-5

ATT_HEADS = 16
ATT_KV_HEADS = 4
ATT_HEAD_DIM = 64
ATT_GROUP = ATT_HEADS // ATT_KV_HEADS
IDX_HEADS = 8
IDX_DIM = 64
TOPK_MAX = 256
Q_BLOCK = 128
ATT_Q_DIM = ATT_HEADS * ATT_HEAD_DIM
ATT_KV_DIM = ATT_KV_HEADS * ATT_HEAD_DIM
ATT_SPLITS = [ATT_Q_DIM, ATT_Q_DIM + ATT_KV_DIM, ATT_Q_DIM + 2 * ATT_KV_DIM,
              ATT_Q_DIM + 2 * ATT_KV_DIM + IDX_HEADS * IDX_DIM,
              ATT_Q_DIM + 2 * ATT_KV_DIM + IDX_HEADS * IDX_DIM + IDX_DIM]
ATT_REST_DIM = IDX_HEADS * IDX_DIM + IDX_DIM + IDX_HEADS

GLA_HEADS = 4
GLA_DK = D_MODEL // 2 // GLA_HEADS
GLA_DV = D_MODEL // GLA_HEADS
GLA_GATE_RANK = 16
GLA_TAU = 16.0
GLA_CHUNK = 64
GLA_SPLITS = [GLA_HEADS * GLA_DK, 2 * GLA_HEADS * GLA_DK,
              2 * GLA_HEADS * GLA_DK + GLA_HEADS * GLA_DV,
              2 * GLA_HEADS * GLA_DK + GLA_HEADS * GLA_DV + GLA_GATE_RANK]

PEER_HEADS = 8
PEER_N_KEYS = 128
PEER_N_EXPERTS = PEER_N_KEYS * PEER_N_KEYS
PEER_D_KEY = 256
PEER_HALF = PEER_D_KEY // 2
PEER_TOPK = 16
PEER_BLOCK = 128

kernel_name = 'dsa_gla_peer_hybrid_step'


def layer_norm(x, g, b):
    xf = x.astype(jnp.float32)
    mu = jnp.mean(xf, axis=-1, keepdims=True)
    var = jnp.mean(jnp.square(xf - mu), axis=-1, keepdims=True)
    return ((xf - mu) * lax.rsqrt(var + LN_EPS) * g + b).astype(x.dtype)


def dsa_project(x, w_in):
    B, T, _ = x.shape
    q, k, v, qi, ki, wi = jnp.split(x @ w_in, ATT_SPLITS, axis=-1)
    q = q.reshape(B, T, ATT_KV_HEADS, ATT_GROUP, ATT_HEAD_DIM)
    k = k.reshape(B, T, ATT_KV_HEADS, ATT_HEAD_DIM)
    v = v.reshape(B, T, ATT_KV_HEADS, ATT_HEAD_DIM)
    qi = qi.reshape(B, T, IDX_HEADS, IDX_DIM)
    wi = wi * (IDX_HEADS ** -0.5)
    return q, k, v, qi, ki, wi


def indexer_scores(qi, wi, ki):
    dots = jnp.einsum('bqhd,bld->bqhl', qi, ki, preferred_element_type=jnp.float32) * (IDX_DIM ** -0.5)
    return jnp.einsum('bqhl,bqh->bql', jax.nn.relu(dots), wi.astype(jnp.float32))


def select_keys(scores, q_pos, n_sel):
    L = scores.shape[-1]
    adm = jnp.arange(L)[None, :] <= q_pos[:, None]
    masked = jnp.where(adm[None], scores, -jnp.inf)
    _, idx = lax.top_k(masked, n_sel)
    valid = idx <= q_pos[None, :, None]
    return idx, valid


def sparse_attend(q, k_sel, v_sel, valid):
    B, Q = q.shape[:2]
    s = jnp.einsum('bqngd,bqknd->bqngk', q, k_sel, preferred_element_type=jnp.float32) * (ATT_HEAD_DIM ** -0.5)
    s = jnp.where(valid[:, :, None, None, :], s, -jnp.inf)
    p = jax.nn.softmax(s, axis=-1)
    o = jnp.einsum('bqngk,bqknd->bqngd', p.astype(v_sel.dtype), v_sel)
    return o.reshape(B, Q, ATT_Q_DIM)


def dsa_prompt(x, w_in, w_out):
    B, T, _ = x.shape
    q, k, v, qi, ki, wi = dsa_project(x, w_in)
    n_sel = min(TOPK_MAX, T // 4)
    qb = min(Q_BLOCK, T)
    nb = T // qb

    def to_blocks(a):
        return a.reshape((B, nb, qb) + a.shape[2:]).swapaxes(0, 1)

    pos = jnp.arange(T).reshape(nb, qb)
    take = jax.vmap(lambda a, i: a[i])

    def block(args):
        q_b, qi_b, wi_b, pos_b = args
        idx, valid = select_keys(indexer_scores(qi_b, wi_b, ki), pos_b, n_sel)
        return sparse_attend(q_b, take(k, idx), take(v, idx), valid)

    o = lax.map(block, (to_blocks(q), to_blocks(qi), to_blocks(wi), pos))
    o = o.swapaxes(0, 1).reshape(B, T, ATT_Q_DIM)
    return o @ w_out, k, v, ki


def dsa_sample(x, cache_k, cache_v, cache_ki, page_table, w_in, w_out):
    B, T, _ = x.shape
    q, k, v, qi, ki, wi = dsa_project(x, w_in)
    past = page_table.shape[1] * PAGE_SIZE
    n_sel = min(TOPK_MAX, (past + T) // 4)
    ki_past = cache_ki[page_table].reshape(B, past, IDX_DIM)
    ki_all = jnp.concatenate([ki_past, ki.astype(ki_past.dtype)], axis=1)
    q_pos = past + jnp.arange(T)
    idx, valid = select_keys(indexer_scores(qi, wi, ki_all), q_pos, n_sel)
    in_past = idx < past
    pidx = jnp.minimum(idx, past - 1)
    phys = jax.vmap(lambda pt, i: pt[i])(page_table, pidx // PAGE_SIZE)
    off = pidx % PAGE_SIZE
    nidx = jnp.clip(idx - past, 0, T - 1)
    take = jax.vmap(lambda a, i: a[i])
    sel = in_past[..., None, None]
    k_sel = jnp.where(sel, cache_k[phys, off], take(k, nidx))
    v_sel = jnp.where(sel, cache_v[phys, off], take(v, nidx))
    o = sparse_attend(q, k_sel.astype(q.dtype), v_sel.astype(q.dtype), valid)
    return o @ w_out, k, v, ki


def gla_scan(q, k, v, log_a, S0):
    B, T, H, _ = q.shape
    C = math.gcd(T, GLA_CHUNK)
    n = T // C

    def chunks(a):
        return a.astype(jnp.float32).reshape((B, n, C, H) + a.shape[3:]).transpose(1, 0, 3, 2, 4)

    tri = jnp.tril(jnp.ones((C, C), dtype=bool))

    def step(S, inp):
        qc, kc, vc, gc = inp
        b = jnp.cumsum(gc, axis=2)
        diff = b[:, :, :, None, :] - b[:, :, None, :, :]
        decay = jnp.exp(jnp.where(tri[:, :, None], diff, -jnp.inf))
        A = jnp.einsum('bhtd,bhsd,bhtsd->bhts', qc, kc, decay)
        o = jnp.einsum('bhts,bhsv->bhtv', A, vc) + jnp.einsum('bhtd,bhdv->bhtv', qc * jnp.exp(b), S)
        bC = b[:, :, -1:, :]
        S_new = jnp.exp(bC[:, :, 0, :])[..., None] * S + jnp.einsum('bhsd,bhsv->bhdv', kc * jnp.exp(bC - b), vc)
        return S_new, o

    S_fin, o = lax.scan(step, S0.astype(jnp.float32), (chunks(q), chunks(k), chunks(v), chunks(log_a)))
    o = o.transpose(1, 0, 3, 2, 4).reshape(B, T, H, GLA_DV)
    return o, S_fin


def gla_mixer(x, S0, w_in, w_gate_up, b_gate, norm_g, w_out):
    B, T, _ = x.shape
    q, k, v, gd, r = jnp.split(x @ w_in, GLA_SPLITS, axis=-1)
    q = q.reshape(B, T, GLA_HEADS, GLA_DK) * (GLA_DK ** -0.5)
    k = k.reshape(B, T, GLA_HEADS, GLA_DK)
    v = v.reshape(B, T, GLA_HEADS, GLA_DV)
    glogit = (gd @ w_gate_up + b_gate).astype(jnp.float32)
    log_a = (jax.nn.log_sigmoid(glogit) / GLA_TAU).reshape(B, T, GLA_HEADS, GLA_DK)
    o, S_fin = gla_scan(q, k, v, log_a, S0)
    o = o * lax.rsqrt(jnp.mean(o * o, axis=-1, keepdims=True) + LN_EPS) * norm_g.astype(jnp.float32)
    o = o.reshape(B, T, GLA_HEADS * GLA_DV).astype(x.dtype) * jax.nn.silu(r)
    return o @ w_out, S_fin


def peer(x, w_q, keys1, keys2, u, v):
    shp = x.shape
    xf = x.reshape(-1, D_MODEL)
    n = xf.shape[0]
    pb = math.gcd(n, PEER_BLOCK)

    def block(xb):
        q = (xb @ w_q).reshape(pb, PEER_HEADS, PEER_D_KEY)
        s1 = jnp.einsum('nhd,kd->nhk', q[..., :PEER_HALF], keys1, preferred_element_type=jnp.float32)
        s2 = jnp.einsum('nhd,kd->nhk', q[..., PEER_HALF:], keys2, preferred_element_type=jnp.float32)
        t1, i1 = lax.top_k(s1, PEER_TOPK)
        t2, i2 = lax.top_k(s2, PEER_TOPK)
        cand = (t1[..., :, None] + t2[..., None, :]).reshape(pb, PEER_HEADS, PEER_TOPK * PEER_TOPK)
        cid = (i1[..., :, None] * PEER_N_KEYS + i2[..., None, :]).reshape(pb, PEER_HEADS, PEER_TOPK * PEER_TOPK)
        sc, pick = lax.top_k(cand, PEER_TOPK)
        eid = jnp.take_along_axis(cid, pick, axis=-1)
        g = jax.nn.softmax(sc, axis=-1)
        h = jax.nn.gelu(jnp.einsum('nd,nhkd->nhk', xb, u[eid], preferred_element_type=jnp.float32), approximate=False)
        return jnp.einsum('nhk,nhkd->nd', (g * h).astype(v.dtype), v[eid])

    y = lax.map(block, xf.reshape(n // pb, pb, D_MODEL))
    return y.reshape(shp).astype(x.dtype)


def setup_inputs(seed: int = 0) -> dict:
    key = jax.random.key(seed)
    ks = iter(jax.random.split(key, 32))

    def nrm(shape, scale):
        return jax.random.normal(next(ks), shape, jnp.float32) * scale

    n_pages = PAST_LEN // PAGE_SIZE
    used = DEC_BATCH * n_pages
    n_pool = used + max(1, used // 4)
    page_table = jax.random.permutation(next(ks), n_pool)[:used].reshape(DEC_BATCH, n_pages).astype(jnp.int32)
    s = D_MODEL ** -0.5
    beta = DEEPNORM_BETA
    x_prompt = nrm((BATCH, SEQ, D_MODEL), 1.0)
    x_sample = nrm((DEC_BATCH, DEC_SEQ, D_MODEL), 1.0)
    cache_k = nrm((N_ATTN_LAYERS, n_pool, PAGE_SIZE, ATT_KV_HEADS, ATT_HEAD_DIM), 1.0)
    cache_v = nrm((N_ATTN_LAYERS, n_pool, PAGE_SIZE, ATT_KV_HEADS, ATT_HEAD_DIM), beta)
    cache_kidx = nrm((N_ATTN_LAYERS, n_pool, PAGE_SIZE, IDX_DIM), 1.0)
    state_gla = nrm((N_GLA_LAYERS, DEC_BATCH, GLA_HEADS, GLA_DK, GLA_DV), 1.0)
    attn_w_in = jnp.concatenate([nrm((N_ATTN_LAYERS, D_MODEL, ATT_Q_DIM + ATT_KV_DIM), s),
                                 nrm((N_ATTN_LAYERS, D_MODEL, ATT_KV_DIM), s * beta),
                                 nrm((N_ATTN_LAYERS, D_MODEL, ATT_REST_DIM), s)], axis=-1)
    attn_w_out = nrm((N_ATTN_LAYERS, ATT_Q_DIM, D_MODEL), (ATT_Q_DIM ** -0.5) * beta)
    gla_w_in = jnp.concatenate([nrm((N_GLA_LAYERS, D_MODEL, 2 * GLA_HEADS * GLA_DK), s),
                                nrm((N_GLA_LAYERS, D_MODEL, GLA_HEADS * GLA_DV), s * beta),
                                nrm((N_GLA_LAYERS, D_MODEL, GLA_GATE_RANK + GLA_HEADS * GLA_DV), s)], axis=-1)
    gla_w_gate_up = nrm((N_GLA_LAYERS, GLA_GATE_RANK, GLA_HEADS * GLA_DK), GLA_GATE_RANK ** -0.5)
    gla_b_gate = nrm((N_GLA_LAYERS, GLA_HEADS * GLA_DK), 0.1)
    gla_norm_g = 1.0 + nrm((N_GLA_LAYERS, GLA_DV), 0.02)
    gla_w_out = nrm((N_GLA_LAYERS, GLA_HEADS * GLA_DV, D_MODEL), ((GLA_HEADS * GLA_DV) ** -0.5) * beta)
    peer_w_q = nrm((DEPTH, D_MODEL, PEER_HEADS * PEER_D_KEY), s)
    peer_keys1 = nrm((DEPTH, PEER_N_KEYS, PEER_HALF), PEER_HALF ** -0.5)
    peer_keys2 = nrm((DEPTH, PEER_N_KEYS, PEER_HALF), PEER_HALF ** -0.5)
    peer_u = nrm((DEPTH, PEER_N_EXPERTS, D_MODEL), s)
    peer_v = nrm((DEPTH, PEER_N_EXPERTS, D_MODEL), ((PEER_HEADS * PEER_TOPK) ** -0.5) * beta)
    ln1_g = 1.0 + nrm((DEPTH, D_MODEL), 0.02)
    ln1_b = nrm((DEPTH, D_MODEL), 0.02)
    ln2_g = 1.0 + nrm((DEPTH, D_MODEL), 0.02)
    ln2_b = nrm((DEPTH, D_MODEL), 0.02)
    return {'x_prompt': x_prompt, 'x_sample': x_sample, 'cache_k': cache_k, 'cache_v': cache_v,
            'cache_kidx': cache_kidx, 'state_gla': state_gla, 'page_table': page_table,
            'attn_w_in': attn_w_in, 'attn_w_out': attn_w_out, 'gla_w_in': gla_w_in,
            'gla_w_gate_up': gla_w_gate_up, 'gla_b_gate': gla_b_gate, 'gla_norm_g': gla_norm_g,
            'gla_w_out': gla_w_out, 'peer_w_q': peer_w_q, 'peer_keys1': peer_keys1,
            'peer_keys2': peer_keys2, 'peer_u': peer_u, 'peer_v': peer_v,
            'ln1_g': ln1_g, 'ln1_b': ln1_b, 'ln2_g': ln2_g, 'ln2_b': ln2_b}


def reference(x_prompt, x_sample, cache_k, cache_v, cache_kidx, state_gla, page_table,
              attn_w_in, attn_w_out, gla_w_in, gla_w_gate_up, gla_b_gate, gla_norm_g, gla_w_out,
              peer_w_q, peer_keys1, peer_keys2, peer_u, peer_v, ln1_g, ln1_b, ln2_g, ln2_b):
    hp, hs = x_prompt, x_sample
    kp_l, vp_l, kip_l, sp_l = [], [], [], []
    ks_l, vs_l, kis_l, ss_l = [], [], [], []
    for i in range(DEPTH):
        j = i // N_MIXERS
        if i % N_MIXERS == 0:
            yp, kp, vp, kip = dsa_prompt(hp, attn_w_in[j], attn_w_out[j])
            ys, ks_, vs_, kis = dsa_sample(hs, cache_k[j], cache_v[j], cache_kidx[j], page_table,
                                           attn_w_in[j], attn_w_out[j])
            kp_l.append(kp); vp_l.append(vp); kip_l.append(kip)
            ks_l.append(ks_); vs_l.append(vs_); kis_l.append(kis)
        else:
            S0 = jnp.zeros((hp.shape[0], GLA_HEADS, GLA_DK, GLA_DV), jnp.float32)
            yp, sp = gla_mixer(hp, S0, gla_w_in[j], gla_w_gate_up[j], gla_b_gate[j], gla_norm_g[j], gla_w_out[j])
            ys, ss = gla_mixer(hs, state_gla[j], gla_w_in[j], gla_w_gate_up[j], gla_b_gate[j], gla_norm_g[j], gla_w_out[j])
            sp_l.append(sp); ss_l.append(ss)
        hp = layer_norm(DEEPNORM_ALPHA * hp + yp, ln1_g[i], ln1_b[i])
        hs = layer_norm(DEEPNORM_ALPHA * hs + ys, ln1_g[i], ln1_b[i])
        hp = layer_norm(DEEPNORM_ALPHA * hp + peer(hp, peer_w_q[i], peer_keys1[i], peer_keys2[i], peer_u[i], peer_v[i]), ln2_g[i], ln2_b[i])
        hs = layer_norm(DEEPNORM_ALPHA * hs + peer(hs, peer_w_q[i], peer_keys1[i], peer_keys2[i], peer_u[i], peer_v[i]), ln2_g[i], ln2_b[i])
    return (hp, hs, jnp.stack(kp_l), jnp.stack(vp_l), jnp.stack(kip_l), jnp.stack(sp_l),
            jnp.stack(ks_l), jnp.stack(vs_l), jnp.stack(kis_l), jnp.stack(ss_l))
```

```python
import functools
import math

import jax
import jax.numpy as jnp
from jax import lax
from jax.experimental import pallas as pl
from jax.experimental.pallas import tpu as pltpu

F32 = jnp.float32
BF16 = jnp.bfloat16
I32 = jnp.int32

D_MODEL = 1024
DEPTH = 2
PAGE_SIZE = 128
DEEPNORM_ALPHA = (2.0 * DEPTH) ** 0.25
LN_EPS = 1e-5

ATT_HEADS = 16
ATT_KV_HEADS = 4
ATT_HEAD_DIM = 64
ATT_GROUP = ATT_HEADS // ATT_KV_HEADS
IDX_HEADS = 8
IDX_DIM = 64
TOPK_MAX = 256
Q_BLOCK = 128
ATT_Q_DIM = ATT_HEADS * ATT_HEAD_DIM
ATT_KV_DIM = ATT_KV_HEADS * ATT_HEAD_DIM

GLA_HEADS = 4
GLA_DK = D_MODEL // 2 // GLA_HEADS
GLA_DV = D_MODEL // GLA_HEADS
GLA_GATE_RANK = 16
GLA_TAU = 16.0
GLA_CHUNK = 64
GLA_SUB = 16

PEER_HEADS = 8
PEER_N_KEYS = 128
PEER_D_KEY = 256
PEER_HALF = PEER_D_KEY // 2
PEER_TOPK = 16

LANES = 128
INT_MIN = -(2 ** 31)
NEG_INF = float("-inf")

NT_DIMS = (((1,), (1,)), ((), ()))
TN_DIMS = (((0,), (0,)), ((), ()))


def _cparams(sem, vmem_mib=None):
    kw = dict(dimension_semantics=sem)
    if vmem_mib is not None:
        kw["vmem_limit_bytes"] = vmem_mib * 1024 * 1024
    return pltpu.CompilerParams(**kw)


def _sort_key(x):
    b = pltpu.bitcast(x + 0.0, I32)
    return b ^ ((b >> 31) & 0x7FFFFFFF)


def _layer_norm(z, g, b):
    mu = jnp.mean(z, axis=-1, keepdims=True)
    zc = z - mu
    var = jnp.mean(zc * zc, axis=-1, keepdims=True)
    return zc * lax.rsqrt(var + LN_EPS) * g + b


def _row_count(keys_ref, n_chunks, ck, rows, pred):
    def body(c, acc):
        base = pl.multiple_of(c * ck, ck)
        blk = keys_ref[:, pl.ds(base, ck)]
        for j in range(ck // LANES):
            hit = pred(blk[:, j * LANES:(j + 1) * LANES], base + j * LANES)
            acc = acc + jnp.where(hit, 1.0, 0.0)
        return acc
    acc = lax.fori_loop(0, n_chunks, body, jnp.zeros((rows, LANES), F32))
    return jnp.sum(acc, axis=1, keepdims=True)


def _select_threshold(keys_ref, cut_ref, n_chunks, ck, rows, kth, idx_bits):
    kth_f = float(kth)

    def count_ge(t):
        tb = jnp.broadcast_to(t, (rows, LANES))
        return _row_count(keys_ref, n_chunks, ck, rows, lambda blk, base: blk >= tb)

    theta = jnp.full((rows, 1), INT_MIN, I32)
    zero = jnp.zeros((rows, 1), I32)
    theta = jnp.where(count_ge(zero) >= kth_f, zero, theta)

    def bit_body(i, t):
        cand = t | lax.shift_left(jnp.int32(1), jnp.int32(30) - i)
        return jnp.where(count_ge(cand) >= kth_f, cand, t)
    theta = lax.fori_loop(0, 31, bit_body, theta)

    n_gt = count_ge(theta + 1)
    n_ge = count_ge(theta)
    need = kth_f - n_gt
    ambiguous = (n_ge - n_gt > need) & (theta > INT_MIN)
    cut_ref[...] = jnp.full(cut_ref.shape, 2 ** idx_bits, I32)

    @pl.when(jnp.max(jnp.where(ambiguous, 1.0, 0.0)) > 0.5)
    def _():
        thb = jnp.broadcast_to(theta, (rows, LANES))
        lane = lax.broadcasted_iota(I32, (rows, LANES), 1)

        def ties_before(jc):
            jb = jnp.broadcast_to(jc, (rows, LANES))
            return _row_count(keys_ref, n_chunks, ck, rows,
                              lambda blk, base: (blk == thb) & (lane + base < jb))

        def jbit(i, jcur):
            cand = jcur | lax.shift_left(jnp.int32(1), jnp.int32(idx_bits - 1) - i)
            return jnp.where(ties_before(cand) <= need, cand, jcur)
        jfin = lax.fori_loop(0, idx_bits, jbit, jnp.zeros((rows, 1), I32))
        jfin = jnp.where(ambiguous, jfin, 2 ** idx_bits)
        cut_ref[...] = jnp.broadcast_to(jfin, cut_ref.shape)
    return theta


def _attn_proj_kernel(x_ref, wq_ref, wkv_ref, wqi_ref, wkw_ref,
                      q_ref, k_ref, v_ref, kb_ref, vb_ref, qi_ref, ki_ref, kib_ref, wi_ref):
    xb = x_ref[...].astype(BF16)
    q = jnp.dot(xb, wq_ref[...], preferred_element_type=F32)
    q_ref[...] = (q * (ATT_HEAD_DIM ** -0.5)).astype(BF16)
    kv = jnp.dot(xb, wkv_ref[...], preferred_element_type=F32)
    k = kv[:, :ATT_KV_DIM]
    v = kv[:, ATT_KV_DIM:]
    k_ref[...] = k
    v_ref[...] = v
    kb_ref[...] = k.astype(BF16)
    vb_ref[...] = v.astype(BF16)
    qi = jnp.dot(xb, wqi_ref[...], preferred_element_type=F32)
    qi_ref[...] = (qi * (IDX_DIM ** -0.5)).astype(BF16)
    kw = jnp.dot(xb, wkw_ref[...], preferred_element_type=F32)
    ki = kw[:, :IDX_DIM]
    ki_ref[...] = ki
    kib_ref[...] = ki.astype(BF16)
    wi_ref[...] = kw[:, IDX_DIM:IDX_DIM + IDX_HEADS] * (IDX_HEADS ** -0.5)


def _attn_proj(x2d, w_in):
    m = x2d.shape[0]
    tm = min(m, 512)
    s0, s1, s2, s3, s4 = (ATT_Q_DIM, ATT_Q_DIM + ATT_KV_DIM, ATT_Q_DIM + 2 * ATT_KV_DIM,
                          ATT_Q_DIM + 2 * ATT_KV_DIM + IDX_HEADS * IDX_DIM,
                          ATT_Q_DIM + 2 * ATT_KV_DIM + IDX_HEADS * IDX_DIM + IDX_DIM)
    wq = w_in[:, :s0].astype(BF16)
    wkv = w_in[:, s0:s2].astype(BF16)
    wqi = w_in[:, s2:s3].astype(BF16)
    wkw = jnp.pad(w_in[:, s3:], ((0, 0), (0, LANES - (w_in.shape[1] - s3)))).astype(BF16)
    row = lambda n: pl.BlockSpec((tm, n), lambda i: (i, 0))
    full = lambda a: pl.BlockSpec(a.shape, lambda i: (0, 0))
    outs = [(ATT_Q_DIM, BF16), (ATT_KV_DIM, F32), (ATT_KV_DIM, F32), (ATT_KV_DIM, BF16),
            (ATT_KV_DIM, BF16), (IDX_HEADS * IDX_DIM, BF16), (IDX_DIM, F32), (IDX_DIM, BF16),
            (IDX_HEADS, F32)]
    return pl.pallas_call(
        _attn_proj_kernel,
        grid=(m // tm,),
        in_specs=[row(D_MODEL), full(wq), full(wkv), full(wqi), full(wkw)],
        out_specs=[row(n) for n, _ in outs],
        out_shape=[jax.ShapeDtypeStruct((m, n), dt) for n, dt in outs],
        compiler_params=_cparams(("parallel",)),
        name="attn_proj",
    )(x2d, wq, wkv, wqi, wkw)


def _flash_update(s, bias, vc, m_ref, l_ref, acc_ref, n):
    rows, ck = s.shape
    r = bias.shape[0]
    s = (s.reshape(rows // r, r, ck) + bias[None]).reshape(rows, ck)
    m_prev = m_ref[n]
    m_new = jnp.maximum(m_prev, jnp.max(s, axis=1, keepdims=True))
    m_safe = jnp.where(m_new == NEG_INF, 0.0, m_new)
    alpha = jnp.exp(m_prev - m_safe)
    p = jnp.exp(s - m_safe)
    l_ref[n] = alpha * l_ref[n] + jnp.sum(p, axis=1, keepdims=True)
    acc_ref[n] = alpha * acc_ref[n] + jnp.dot(p.astype(BF16), vc, preferred_element_type=F32)
    m_ref[n] = m_new


def _dsa_prompt_kernel(q_ref, qi_ref, wi_ref, kb_ref, vb_ref, kib_ref, o_ref,
                       keys_ref, cut_ref, m_ref, l_ref, acc_ref, *, ck, n_sel, idx_bits):
    i = pl.program_id(1)
    qb = Q_BLOCK
    n_chunks = ((i + 1) * qb + ck - 1) // ck
    q_pos = i * qb + lax.broadcasted_iota(I32, (qb, ck), 0)
    lane = lax.broadcasted_iota(I32, (qb, ck), 1)

    qi = qi_ref[0]
    wi = wi_ref[0]

    def score_body(c, carry):
        base = pl.multiple_of(c * ck, ck)
        kic = kib_ref[0, pl.ds(base, ck), :]
        acc = jnp.zeros((qb, ck), F32)
        for h in range(IDX_HEADS):
            d = lax.dot_general(qi[:, h * IDX_DIM:(h + 1) * IDX_DIM], kic, NT_DIMS,
                                preferred_element_type=F32)
            acc = acc + jnp.maximum(d, 0.0) * wi[:, h:h + 1]
        keys_ref[:, pl.ds(base, ck)] = jnp.where(lane + base <= q_pos, _sort_key(acc), INT_MIN)
        return carry
    lax.fori_loop(0, n_chunks, score_body, 0)

    theta = _select_threshold(keys_ref, cut_ref, n_chunks, ck, qb, n_sel, idx_bits)
    thb = jnp.broadcast_to(theta, (qb, ck))
    cutb = jnp.broadcast_to(cut_ref[:, :1], (qb, ck))

    q = q_ref[0]
    qn = [jnp.concatenate([q[:, (n * ATT_GROUP + g) * ATT_HEAD_DIM:(n * ATT_GROUP + g + 1) * ATT_HEAD_DIM]
                           for g in range(ATT_GROUP)], axis=0) for n in range(ATT_KV_HEADS)]
    m_ref[...] = jnp.full(m_ref.shape, NEG_INF, F32)
    l_ref[...] = jnp.zeros(l_ref.shape, F32)
    acc_ref[...] = jnp.zeros(acc_ref.shape, F32)

    def attn_body(c, carry):
        base = pl.multiple_of(c * ck, ck)
        key = keys_ref[:, pl.ds(base, ck)]
        sel = ((key > thb) | ((key == thb) & (lane + base < cutb))) & (key != INT_MIN)
        bias = jnp.where(sel, 0.0, NEG_INF)
        for n in range(ATT_KV_HEADS):
            kc = kb_ref[0, pl.ds(base, ck), n * ATT_HEAD_DIM:(n + 1) * ATT_HEAD_DIM]
            vc = vb_ref[0, pl.ds(base, ck), n * ATT_HEAD_DIM:(n + 1) * ATT_HEAD_DIM]
            s = lax.dot_general(qn[n], kc, NT_DIMS, preferred_element_type=F32)
            _flash_update(s, bias, vc, m_ref, l_ref, acc_ref, n)
        return carry
    lax.fori_loop(0, n_chunks, attn_body, 0)

    for n in range(ATT_KV_HEADS):
        o = acc_ref[n] / l_ref[n]
        for g in range(ATT_GROUP):
            h = n * ATT_GROUP + g
            o_ref[0, :, h * ATT_HEAD_DIM:(h + 1) * ATT_HEAD_DIM] = o[g * qb:(g + 1) * qb, :].astype(o_ref.dtype)


def _dsa_prompt(qs, qis, wi, kb, vb, kib, n_sel):
    b, t, _ = qs.shape
    ck = min(512, t)
    nqb = t // Q_BLOCK
    idx_bits = max(1, math.ceil(math.log2(t))) + 1
    blk = lambda n: pl.BlockSpec((1, Q_BLOCK, n), lambda bi, i: (bi, i, 0))
    res = lambda n: pl.BlockSpec((1, t, n), lambda bi, i: (bi, 0, 0))
    gr = ATT_GROUP * Q_BLOCK
    return pl.pallas_call(
        functools.partial(_dsa_prompt_kernel, ck=ck, n_sel=n_sel, idx_bits=idx_bits),
        grid=(b, nqb),
        in_specs=[blk(ATT_Q_DIM), blk(IDX_HEADS * IDX_DIM), blk(IDX_HEADS),
                  res(ATT_KV_DIM), res(ATT_KV_DIM), res(IDX_DIM)],
        out_specs=blk(ATT_Q_DIM),
        out_shape=jax.ShapeDtypeStruct((b, t, ATT_Q_DIM), BF16),
        scratch_shapes=[pltpu.VMEM((Q_BLOCK, t), I32), pltpu.VMEM((Q_BLOCK, LANES), I32),
                        pltpu.VMEM((ATT_KV_HEADS, gr, 1), F32), pltpu.VMEM((ATT_KV_HEADS, gr, 1), F32),
                        pltpu.VMEM((ATT_KV_HEADS, gr, ATT_HEAD_DIM), F32)],
        compiler_params=_cparams(("parallel", "arbitrary"), 56),
        name="dsa_prompt",
    )(qs, qis, wi, kb, vb, kib)


def _outproj_ln_kernel(o_ref, h_ref, w_ref, g_ref, b_ref, out_ref):
    y = jnp.dot(o_ref[...], w_ref[...], preferred_element_type=F32)
    out_ref[...] = _layer_norm(DEEPNORM_ALPHA * h_ref[...] + y, g_ref[...], b_ref[...])


def _outproj_ln(o2d, h2d, w_out, g, b):
    m, kdim = o2d.shape
    tm = min(m, 512)
    row = lambda n: pl.BlockSpec((tm, n), lambda i: (i, 0))
    return pl.pallas_call(
        _outproj_ln_kernel,
        grid=(m // tm,),
        in_specs=[row(kdim), row(D_MODEL), pl.BlockSpec((kdim, D_MODEL), lambda i: (0, 0)),
                  pl.BlockSpec((1, D_MODEL), lambda i: (0, 0)), pl.BlockSpec((1, D_MODEL), lambda i: (0, 0))],
        out_specs=row(D_MODEL),
        out_shape=jax.ShapeDtypeStruct((m, D_MODEL), F32),
        compiler_params=_cparams(("parallel",)),
        name="outproj_ln",
    )(o2d, h2d, w_out.astype(BF16), g.reshape(1, -1), b.reshape(1, -1))


def _dsa_sample_kernel(pt_ref, qbd_ref, qi_ref, wi_ref, kn_ref, vn_ref, kin_ref,
                       cki_hbm, ck_hbm, cv_hbm, o_ref,
                       ki_buf, k_buf, v_buf, s_ref, keys_ref, cut_ref, sem,
                       *, n_pages, t_new, n_sel, idx_bits, layer_off, lchunk):
    b = pl.program_id(0)
    past = n_pages * PAGE_SIZE
    ltot = past + LANES
    rows = keys_ref.shape[0]

    def page_copy(hbm, buf, p, s):
        page = pt_ref[b, p] + layer_off
        return pltpu.make_async_copy(hbm.at[page], buf.at[pl.ds(pl.multiple_of(p * PAGE_SIZE, PAGE_SIZE), PAGE_SIZE)],
                                     sem.at[s])

    def start_all(p, c):
        page_copy(cki_hbm, ki_buf, p, 0).start()
        page_copy(ck_hbm, k_buf, p, 1).start()
        page_copy(cv_hbm, v_buf, p, 2).start()
        return c
    lax.fori_loop(0, n_pages, start_all, 0)

    def wait_all(hbm, buf, s):
        def body(p, c):
            page_copy(hbm, buf, p, s).wait()
            return c
        lax.fori_loop(0, n_pages, body, 0)

    qi = qi_ref[0]
    wcol = wi_ref[0]
    wait_all(cki_hbm, ki_buf, 0)

    def idx_scores(kic):
        d = lax.dot_general(qi, kic, NT_DIMS, preferred_element_type=F32)
        d = jnp.maximum(d, 0.0) * wcol
        return jnp.sum(d.reshape(t_new, IDX_HEADS, d.shape[1]), axis=1)

    keys_ref[...] = jnp.full(keys_ref.shape, INT_MIN, I32)
    for c in range(past // lchunk):
        sc = idx_scores(ki_buf[c * lchunk:(c + 1) * lchunk, :].astype(BF16))
        keys_ref[0:t_new, c * lchunk:(c + 1) * lchunk] = _sort_key(sc)
    scn = idx_scores(kin_ref[0])
    tpos = lax.broadcasted_iota(I32, (t_new, LANES), 0)
    lpos = lax.broadcasted_iota(I32, (t_new, LANES), 1)
    keys_ref[0:t_new, past:ltot] = jnp.where((lpos <= tpos) & (lpos < t_new), _sort_key(scn), INT_MIN)

    theta = _select_threshold(keys_ref, cut_ref, ltot // LANES, LANES, rows, n_sel, idx_bits)
    key = keys_ref[...]
    lane = lax.broadcasted_iota(I32, (rows, ltot), 1)
    sel = ((key > theta) | ((key == theta) & (lane < cut_ref[:, :1]))) & (key != INT_MIN)
    bias = jnp.where(sel, 0.0, NEG_INF)[0:t_new]
    bias = jnp.broadcast_to(bias[:, None, :], (t_new, ATT_HEADS, ltot)).reshape(t_new * ATT_HEADS, ltot)

    qbd = qbd_ref[0]
    wait_all(ck_hbm, k_buf, 1)
    for c in range(past // lchunk):
        kc = k_buf[c * lchunk:(c + 1) * lchunk, :].astype(BF16)
        s_ref[:, c * lchunk:(c + 1) * lchunk] = lax.dot_general(qbd, kc, NT_DIMS, preferred_element_type=F32)
    s_ref[:, past:ltot] = lax.dot_general(qbd, kn_ref[0], NT_DIMS, preferred_element_type=F32)
    s = s_ref[...] + bias
    m = jnp.max(s, axis=1, keepdims=True)
    p = jnp.exp(s - m)
    linv = 1.0 / jnp.sum(p, axis=1, keepdims=True)
    s_ref[...] = p
    wait_all(cv_hbm, v_buf, 2)
    o = jnp.dot(s_ref[:, past:ltot].astype(BF16), vn_ref[0], preferred_element_type=F32)
    for c in range(past // lchunk):
        vc = v_buf[c * lchunk:(c + 1) * lchunk, :].astype(BF16)
        o = o + jnp.dot(s_ref[:, c * lchunk:(c + 1) * lchunk].astype(BF16), vc, preferred_element_type=F32)
    o = o * linv
    nrow = (lax.broadcasted_iota(I32, (t_new * ATT_HEADS, ATT_HEAD_DIM), 0) % ATT_HEADS) // ATT_GROUP
    out = jnp.zeros((t_new * ATT_HEADS, ATT_HEAD_DIM), F32)
    for n in range(ATT_KV_HEADS):
        out = out + jnp.where(nrow == n, o[:, n * ATT_HEAD_DIM:(n + 1) * ATT_HEAD_DIM], 0.0)
    o_ref[0] = out.astype(o_ref.dtype)


def _dsa_sample(page_table, qs, qis, wi, kb, vb, kib, cache_ki, cache_k, cache_v, layer_off, n_sel):
    b, t, _ = qs.shape
    n_pages = page_table.shape[1]
    past = n_pages * PAGE_SIZE
    ltot = past + LANES
    rows = 8
    q5 = qs.reshape(b, t, ATT_KV_HEADS, ATT_GROUP, 1, ATT_HEAD_DIM)
    eye = jnp.eye(ATT_KV_HEADS, dtype=qs.dtype).reshape(1, 1, ATT_KV_HEADS, 1, ATT_KV_HEADS, 1)
    qbd = (q5 * eye).reshape(b, t * ATT_HEADS, ATT_KV_DIM)
    qi = qis.reshape(b, t * IDX_HEADS, IDX_DIM)
    wcol = wi.reshape(b, t * IDX_HEADS, 1)
    padr = lambda a: jnp.pad(a, ((0, 0), (0, LANES - t), (0, 0)))
    kn, vn, kin = padr(kb), padr(vb), padr(kib)
    idx_bits = max(1, math.ceil(math.log2(ltot))) + 1
    lchunk = min(1024, past)
    bl = lambda a: pl.BlockSpec((1,) + a.shape[1:], lambda i, pt: (i, 0, 0))
    anyspec = pl.BlockSpec(memory_space=pl.ANY)
    grid_spec = pltpu.PrefetchScalarGridSpec(
        num_scalar_prefetch=1,
        grid=(b,),
        in_specs=[bl(qbd), bl(qi), bl(wcol), bl(kn), bl(vn), bl(kin), anyspec, anyspec, anyspec],
        out_specs=pl.BlockSpec((1, t * ATT_HEADS, ATT_HEAD_DIM), lambda i, pt: (i, 0, 0)),
        scratch_shapes=[pltpu.VMEM((past, IDX_DIM), F32), pltpu.VMEM((past, ATT_KV_DIM), F32),
                        pltpu.VMEM((past, ATT_KV_DIM), F32), pltpu.VMEM((t * ATT_HEADS, ltot), F32),
                        pltpu.VMEM((rows, ltot), I32), pltpu.VMEM((rows, LANES), I32),
                        pltpu.SemaphoreType.DMA((3,))],
    )
    o = pl.pallas_call(
        functools.partial(_dsa_sample_kernel, n_pages=n_pages, t_new=t, n_sel=n_sel, idx_bits=idx_bits,
                          layer_off=layer_off, lchunk=lchunk),
        grid_spec=grid_spec,
        out_shape=jax.ShapeDtypeStruct((b, t * ATT_HEADS, ATT_HEAD_DIM), BF16),
        compiler_params=_cparams(("arbitrary",), 56),
        name="dsa_sample",
    )(page_table, qbd, qi, wcol, kn, vn, kin, cache_ki, cache_k, cache_v)
    return o.reshape(b, t, ATT_Q_DIM)


def _gla_proj_kernel(x_ref, wq_ref, wk_ref, wv_ref, wg_ref, wr_ref, wup_ref, bg_ref,
                     q_ref, k_ref, v_ref, la_ref, r_ref):
    xb = x_ref[...].astype(BF16)
    q_ref[...] = jnp.dot(xb, wq_ref[...], preferred_element_type=F32) * (GLA_DK ** -0.5)
    k_ref[...] = jnp.dot(xb, wk_ref[...], preferred_element_type=F32)
    v_ref[...] = jnp.dot(xb, wv_ref[...], preferred_element_type=F32)
    r_ref[...] = jnp.dot(xb, wr_ref[...], preferred_element_type=F32)
    gd = jnp.dot(xb, wg_ref[...], preferred_element_type=F32)
    glogit = jnp.dot(gd.astype(BF16), wup_ref[...], preferred_element_type=F32) + bg_ref[...]
    la_ref[...] = jax.nn.log_sigmoid(glogit) / GLA_TAU


def _gla_proj(x2d, w_in, w_gate_up, b_gate):
    m = x2d.shape[0]
    tm = min(m, 512)
    dq = GLA_HEADS * GLA_DK
    dv = GLA_HEADS * GLA_DV
    wq = w_in[:, :dq].astype(BF16)
    wk = w_in[:, dq:2 * dq].astype(BF16)
    wv = w_in[:, 2 * dq:2 * dq + dv].astype(BF16)
    wg = jnp.pad(w_in[:, 2 * dq + dv:2 * dq + dv + GLA_GATE_RANK], ((0, 0), (0, LANES - GLA_GATE_RANK))).astype(BF16)
    wr = w_in[:, 2 * dq + dv + GLA_GATE_RANK:].astype(BF16)
    wup = jnp.pad(w_gate_up, ((0, LANES - GLA_GATE_RANK), (0, 0))).astype(BF16)
    row = lambda n: pl.BlockSpec((tm, n), lambda i: (i, 0))
    full = lambda a: pl.BlockSpec(a.shape, lambda i: (0, 0))
    bg = b_gate.reshape(1, -1)
    outs = [dq, dq, dv, dq, dv]
    return pl.pallas_call(
        _gla_proj_kernel,
        grid=(m // tm,),
        in_specs=[row(D_MODEL), full(wq), full(wk), full(wv), full(wg), full(wr), full(wup), full(bg)],
        out_specs=[row(n) for n in outs],
        out_shape=[jax.ShapeDtypeStruct((m, n), F32) for n in outs],
        compiler_params=_cparams(("parallel",)),
        name="gla_proj",
    )(x2d, wq, wk, wv, wg, wr, wup, bg)


def _gla_chunk(q, k, v, g, st, c):
    sub = GLA_SUB
    n_sub = c // sub
    ri = lax.broadcasted_iota(I32, (c, c), 0)
    ci = lax.broadcasted_iota(I32, (c, c), 1)
    tri = jnp.where(ci <= ri, 1.0, 0.0).astype(F32)
    bc = jnp.dot(tri, g, preferred_element_type=F32, precision=lax.Precision.HIGHEST)
    o = lax.dot_general((q * jnp.exp(bc)).astype(BF16), st.astype(BF16), NT_DIMS, preferred_element_type=F32)
    if n_sub > 1:
        mrow = jnp.concatenate([jnp.zeros((sub, GLA_DK), F32)] +
                               [jnp.broadcast_to(bc[i * sub - 1:i * sub], (sub, GLA_DK)) for i in range(1, n_sub)], axis=0)
        qh = (q * jnp.exp(bc - mrow)).astype(BF16)
        parts = [jnp.zeros((sub, GLA_DV), F32)]
        for i in range(1, n_sub):
            kh = (k[:i * sub] * jnp.exp(bc[i * sub - 1:i * sub] - bc[:i * sub])).astype(BF16)
            a = lax.dot_general(qh[i * sub:(i + 1) * sub], kh, NT_DIMS, preferred_element_type=F32)
            parts.append(jnp.dot(a.astype(BF16), v[:i * sub].astype(BF16), preferred_element_type=F32))
        o = o + jnp.concatenate(parts, axis=0)
    rsub = lax.broadcasted_iota(I32, (c, 1), 0) % sub
    for dlt in range(sub):
        ks = k if dlt == 0 else pltpu.roll(k, dlt, 0)
        bs = bc if dlt == 0 else pltpu.roll(bc, dlt, 0)
        vs = v if dlt == 0 else pltpu.roll(v, dlt, 0)
        w = jnp.sum(q * ks * jnp.exp(jnp.minimum(bc - bs, 0.0)), axis=1, keepdims=True)
        o = o + jnp.where(rsub >= dlt, w, 0.0) * vs
    bl = bc[c - 1:c]
    kt = (k * jnp.exp(bl - bc)).astype(BF16)
    st_new = st * jnp.exp(bl) + lax.dot_general(v.astype(BF16), kt, TN_DIMS, preferred_element_type=F32)
    return o, st_new


def _gla_kernel(q_ref, k_ref, v_ref, g_ref, r_ref, ng_ref, s0_ref, o_ref, sf_ref, st_ref, *, c, n_c):
    tb = pl.program_id(2)

    @pl.when(tb == 0)
    def _():
        st_ref[...] = s0_ref[0, 0].T

    st = st_ref[...]
    for ci in range(n_c):
        sl = slice(ci * c, (ci + 1) * c)
        o, st = _gla_chunk(q_ref[0, sl, :], k_ref[0, sl, :], v_ref[0, sl, :], g_ref[0, sl, :], st, c)
        o = o * lax.rsqrt(jnp.mean(o * o, axis=-1, keepdims=True) + LN_EPS) * ng_ref[...]
        o_ref[0, sl, :] = (o * jax.nn.silu(r_ref[0, sl, :])).astype(o_ref.dtype)
    st_ref[...] = st

    @pl.when(tb == pl.num_programs(2) - 1)
    def _():
        sf_ref[0, 0] = st.T


def _gla(q, k, v, la, r, norm_g, s0):
    b, t, _ = q.shape
    c = math.gcd(t, GLA_CHUNK)
    tblk = math.gcd(t, 4 * GLA_CHUNK)
    n_c = tblk // c
    kspec = pl.BlockSpec((1, tblk, GLA_DK), lambda bi, h, i: (bi, i, h))
    vspec = pl.BlockSpec((1, tblk, GLA_DV), lambda bi, h, i: (bi, i, h))
    sspec = pl.BlockSpec((1, 1, GLA_DK, GLA_DV), lambda bi, h, i: (bi, h, 0, 0))
    return pl.pallas_call(
        functools.partial(_gla_kernel, c=c, n_c=n_c),
        grid=(b, GLA_HEADS, t // tblk),
        in_specs=[kspec, kspec, vspec, kspec, vspec, pl.BlockSpec((1, GLA_DV), lambda bi, h, i: (0, 0)), sspec],
        out_specs=[vspec, sspec],
        out_shape=[jax.ShapeDtypeStruct((b, t, GLA_HEADS * GLA_DV), BF16),
                   jax.ShapeDtypeStruct((b, GLA_HEADS, GLA_DK, GLA_DV), F32)],
        scratch_shapes=[pltpu.VMEM((GLA_DV, GLA_DK), F32)],
        compiler_params=_cparams(("parallel", "parallel", "arbitrary")),
        name="gla_scan",
    )(q, k, v, la, r, norm_g.reshape(1, -1), s0)


def _gla_mixer(x, s0, w_in, w_gate_up, b_gate, norm_g):
    b, t, _ = x.shape
    q, k, v, la, r = _gla_proj(x.reshape(b * t, D_MODEL), w_in, w_gate_up, b_gate)
    tp = -(-t // GLA_SUB) * GLA_SUB
    r3 = lambda a: jnp.pad(a.reshape(b, t, -1), ((0, 0), (0, tp - t), (0, 0)))
    o, sf = _gla(r3(q), r3(k), r3(v), r3(la), r3(r), norm_g, s0)
    return o[:, :t], sf


def _top_values(s, k):
    vals = []
    cur = s
    for _ in range(k):
        mx = jnp.max(cur, axis=0, keepdims=True)
        vals.append(mx)
        cur = jnp.where(cur == mx, NEG_INF, cur)
    return vals


def _peer_select_kernel(x_ref, wq_ref, k1_ref, k2_ref, a_ref, b_ref, p_ref, e_ref, th_ref):
    xb = x_ref[...].astype(BF16)
    q = jnp.dot(xb, wq_ref[...], preferred_element_type=F32).astype(BF16)
    tt = q.shape[0]
    k1 = k1_ref[...]
    k2 = k2_ref[...]
    sub8 = lax.broadcasted_iota(I32, (8, tt), 0)
    thetas = []
    for h in range(PEER_HEADS):
        s1 = lax.dot_general(k1, q[:, h * PEER_D_KEY:h * PEER_D_KEY + PEER_HALF], NT_DIMS,
                             preferred_element_type=F32)
        s2 = lax.dot_general(k2, q[:, h * PEER_D_KEY + PEER_HALF:(h + 1) * PEER_D_KEY], NT_DIMS,
                             preferred_element_type=F32)
        t1 = _top_values(s1, PEER_TOPK)
        t2 = _top_values(s2, PEER_TOPK)
        t2a = jnp.concatenate(t2, axis=0)
        cands = [t1[0] + t2a]
        for a in range(1, PEER_TOPK):
            nb = PEER_TOPK // (a + 1)
            cands.append(jnp.where(sub8 < nb, t1[a] + t2a[:8], NEG_INF))
        best = _top_values(jnp.concatenate(cands, axis=0), PEER_TOPK)
        theta = best[PEER_TOPK - 1]
        z = best[0] * 0.0
        for bv in best:
            z = z + jnp.exp(bv - best[0])
        in1 = s1 >= t1[PEER_TOPK - 1]
        in2 = s2 >= t2[PEER_TOPK - 1]
        a_ref[h] = jnp.where(in1, s1, NEG_INF)
        b_ref[h] = jnp.where(in2, s2, NEG_INF)
        p_ref[h] = jnp.where(in1, jnp.exp(s1 - t1[0]) / z, 0.0)
        e_ref[h] = jnp.where(in2, jnp.exp(s2 - t2[0]), 0.0)
        thetas.append(theta)
    th_ref[...] = jnp.concatenate(thetas, axis=0)


def _peer_select(x2d, w_q, keys1, keys2):
    n = x2d.shape[0]
    tt = min(n, 256)
    wq = w_q.astype(BF16)
    k1 = keys1.astype(BF16)
    k2 = keys2.astype(BF16)
    full = lambda a: pl.BlockSpec(a.shape, lambda i: (0,) * a.ndim)
    hspec = pl.BlockSpec((PEER_HEADS, PEER_N_KEYS, tt), lambda i: (0, 0, i))
    big = jax.ShapeDtypeStruct((PEER_HEADS, PEER_N_KEYS, n), F32)
    return pl.pallas_call(
        _peer_select_kernel,
        grid=(n // tt,),
        in_specs=[pl.BlockSpec((tt, D_MODEL), lambda i: (i, 0)), full(wq), full(k1), full(k2)],
        out_specs=[hspec, hspec, hspec, hspec, pl.BlockSpec((PEER_HEADS, tt), lambda i: (0, i))],
        out_shape=[big, big, big, big, jax.ShapeDtypeStruct((PEER_HEADS, n), F32)],
        compiler_params=_cparams(("parallel",)),
        name="peer_select",
    )(x2d, wq, k1, k2)


def _peer_mix_kernel(h_ref, a_ref, b_ref, p_ref, e_ref, th_ref, u_ref, vt_ref, g_ref, bb_ref, out_ref,
                     xb_ref, ht_ref, wt_ref, yt_ref, *, n_i1):
    eb = pl.program_id(1)
    tt = h_ref.shape[0]

    @pl.when(eb == 0)
    def _():
        xb_ref[...] = h_ref[...].astype(BF16)
        yt_ref[...] = jnp.zeros(yt_ref.shape, F32)

    ht_ref[...] = lax.dot_general(u_ref[...], xb_ref[...], NT_DIMS, preferred_element_type=F32)

    i1_base = pl.multiple_of(eb * n_i1, n_i1)
    for lg in range(tt // LANES):
        ls = slice(lg * LANES, (lg + 1) * LANES)
        a8 = [a_ref[h, pl.ds(i1_base, n_i1), ls] for h in range(PEER_HEADS)]
        p8 = [p_ref[h, pl.ds(i1_base, n_i1), ls] for h in range(PEER_HEADS)]
        for j in range(n_i1):
            acc = jnp.zeros((PEER_N_KEYS, LANES), F32)
            for h in range(PEER_HEADS):
                hit = (a8[h][j:j + 1] + b_ref[h, :, ls]) >= th_ref[h:h + 1, ls]
                acc = acc + jnp.where(hit, p8[h][j:j + 1] * e_ref[h, :, ls], 0.0)
            rs = slice(j * PEER_N_KEYS, (j + 1) * PEER_N_KEYS)
            x = ht_ref[rs, ls]
            gelu = 0.5 * x * (1.0 + lax.erf(x * (2.0 ** -0.5)))
            wt_ref[rs, ls] = (gelu * acc).astype(BF16)

    yt_ref[...] += jnp.dot(vt_ref[...], wt_ref[...], preferred_element_type=F32)

    @pl.when(eb == pl.num_programs(1) - 1)
    def _():
        y = yt_ref[...].T
        out_ref[...] = _layer_norm(DEEPNORM_ALPHA * h_ref[...] + y, g_ref[...], bb_ref[...])


def _peer_ln(h2d, w_q, keys1, keys2, u, v, g, b):
    n_real = h2d.shape[0]
    if n_real % LANES:
        h2d = jnp.pad(h2d, ((0, LANES - n_real % LANES), (0, 0)))
        return _peer_ln(h2d, w_q, keys1, keys2, u, v, g, b)[:n_real]
    n = n_real
    a, bm, p, e, th = _peer_select(h2d, w_q, keys1, keys2)
    tt = min(n, 512)
    eblk = 1024
    n_e = u.shape[0]
    ub = u.astype(BF16)
    vt = v.T.astype(BF16)
    hspec = pl.BlockSpec((PEER_HEADS, PEER_N_KEYS, tt), lambda i, j: (0, 0, i))
    vec = pl.BlockSpec((1, D_MODEL), lambda i, j: (0, 0))
    return pl.pallas_call(
        functools.partial(_peer_mix_kernel, n_i1=eblk // PEER_N_KEYS),
        grid=(n // tt, n_e // eblk),
        in_specs=[pl.BlockSpec((tt, D_MODEL), lambda i, j: (i, 0)), hspec, hspec, hspec, hspec,
                  pl.BlockSpec((PEER_HEADS, tt), lambda i, j: (0, i)),
                  pl.BlockSpec((eblk, D_MODEL), lambda i, j: (j, 0)),
                  pl.BlockSpec((D_MODEL, eblk), lambda i, j: (0, j)), vec, vec],
        out_specs=pl.BlockSpec((tt, D_MODEL), lambda i, j: (i, 0)),
        out_shape=jax.ShapeDtypeStruct((n, D_MODEL), F32),
        scratch_shapes=[pltpu.VMEM((tt, D_MODEL), BF16), pltpu.VMEM((eblk, tt), F32),
                        pltpu.VMEM((eblk, tt), BF16), pltpu.VMEM((D_MODEL, tt), F32)],
        compiler_params=_cparams(("parallel", "arbitrary"), 56),
        name="peer_mix",
    )(h2d, a, bm, p, e, th, ub, vt, g.reshape(1, -1), b.reshape(1, -1))


def kernel(x_prompt, x_sample, cache_k, cache_v, cache_kidx, state_gla, page_table,
           attn_w_in, attn_w_out, gla_w_in, gla_w_gate_up, gla_b_gate, gla_norm_g, gla_w_out,
           peer_w_q, peer_keys1, peer_keys2, peer_u, peer_v, ln1_g, ln1_b, ln2_g, ln2_b):
    bp, tp, _ = x_prompt.shape
    bs, ts, _ = x_sample.shape
    n_pool = cache_k.shape[1]
    past = page_table.shape[1] * PAGE_SIZE
    ck = cache_k.reshape(-1, PAGE_SIZE, ATT_KV_DIM)
    cv = cache_v.reshape(-1, PAGE_SIZE, ATT_KV_DIM)
    cki = cache_kidx.reshape(-1, PAGE_SIZE, IDX_DIM)
    hp = x_prompt.reshape(bp * tp, D_MODEL)
    hs = x_sample.reshape(bs * ts, D_MODEL)
    kp_l, vp_l, kip_l, sp_l = [], [], [], []
    ks_l, vs_l, kis_l, ss_l = [], [], [], []
    for i in range(DEPTH):
        j = i // 2
        if i % 2 == 0:
            qs, k, v, kb, vb, qis, ki, kib, wi = _attn_proj(hp, attn_w_in[j])
            r3 = lambda a: a.reshape(bp, tp, -1)
            op = _dsa_prompt(r3(qs), r3(qis), r3(wi), r3(kb), r3(vb), r3(kib), min(TOPK_MAX, tp // 4))
            kp_l.append(k.reshape(bp, tp, ATT_KV_HEADS, ATT_HEAD_DIM))
            vp_l.append(v.reshape(bp, tp, ATT_KV_HEADS, ATT_HEAD_DIM))
            kip_l.append(ki.reshape(bp, tp, IDX_DIM))
            qs, k, v, kb, vb, qis, ki, kib, wi = _attn_proj(hs, attn_w_in[j])
            r3 = lambda a: a.reshape(bs, ts, -1)
            os_ = _dsa_sample(page_table, r3(qs), r3(qis), r3(wi), r3(kb), r3(vb), r3(kib), cki, ck, cv,
                              j * n_pool, min(TOPK_MAX, (past + ts) // 4))
            ks_l.append(k.reshape(bs, ts, ATT_KV_HEADS, ATT_HEAD_DIM))
            vs_l.append(v.reshape(bs, ts, ATT_KV_HEADS, ATT_HEAD_DIM))
            kis_l.append(ki.reshape(bs, ts, IDX_DIM))
            w_out = attn_w_out[j]
        else:
            s0 = jnp.zeros((bp, GLA_HEADS, GLA_DK, GLA_DV), F32)
            op, sp = _gla_mixer(hp.reshape(bp, tp, D_MODEL), s0, gla_w_in[j], gla_w_gate_up[j], gla_b_gate[j],
                                gla_norm_g[j])
            os_, ss = _gla_mixer(hs.reshape(bs, ts, D_MODEL), state_gla[j], gla_w_in[j], gla_w_gate_up[j],
                                 gla_b_gate[j], gla_norm_g[j])
            sp_l.append(sp)
            ss_l.append(ss)
            w_out = gla_w_out[j]
        hp = _outproj_ln(op.reshape(bp * tp, -1), hp, w_out, ln1_g[i], ln1_b[i])
        hs = _outproj_ln(os_.reshape(bs * ts, -1), hs, w_out, ln1_g[i], ln1_b[i])
        pw = (peer_w_q[i], peer_keys1[i], peer_keys2[i], peer_u[i], peer_v[i], ln2_g[i], ln2_b[i])
        hp = _peer_ln(hp, *pw)
        hs = _peer_ln(hs, *pw)
    return (hp.reshape(bp, tp, D_MODEL), hs.reshape(bs, ts, D_MODEL),
            jnp.stack(kp_l), jnp.stack(vp_l), jnp.stack(kip_l), jnp.stack(sp_l),
            jnp.stack(ks_l), jnp.stack(vs_l), jnp.stack(kis_l), jnp.stack(ss_l))
```

```python
import functools
import math

import jax
import jax.numpy as jnp
from jax import lax
from jax.experimental import pallas as pl
from jax.experimental.pallas import tpu as pltpu

F32 = jnp.float32
BF16 = jnp.bfloat16
I32 = jnp.int32

D_MODEL = 1024
DEPTH = 2
PAGE_SIZE = 128
DEEPNORM_ALPHA = (2.0 * DEPTH) ** 0.25
LN_EPS = 1e-5

ATT_HEADS = 16
ATT_KV_HEADS = 4
ATT_HEAD_DIM = 64
ATT_GROUP = ATT_HEADS // ATT_KV_HEADS
IDX_HEADS = 8
IDX_DIM = 64
TOPK_MAX = 256
Q_BLOCK = 128
ATT_Q_DIM = ATT_HEADS * ATT_HEAD_DIM
ATT_KV_DIM = ATT_KV_HEADS * ATT_HEAD_DIM

GLA_HEADS = 4
GLA_DK = D_MODEL // 2 // GLA_HEADS
GLA_DV = D_MODEL // GLA_HEADS
GLA_GATE_RANK = 16
GLA_TAU = 16.0
GLA_CHUNK = 64
GLA_SUB = 16

PEER_HEADS = 8
PEER_N_KEYS = 128
PEER_D_KEY = 256
PEER_HALF = PEER_D_KEY // 2
PEER_TOPK = 16

LANES = 128
BF16_ROWS = 16
LOG2_E = 1.4426950408889634
INT_MIN = -(2 ** 31)
NEG_INF = float("-inf")

NT_DIMS = (((1,), (1,)), ((), ()))
TN_DIMS = (((0,), (0,)), ((), ()))


def _cparams(sem, vmem_mib=None):
    kw = dict(dimension_semantics=sem)
    if vmem_mib is not None:
        kw["vmem_limit_bytes"] = vmem_mib * 1024 * 1024
    return pltpu.CompilerParams(**kw)


def _sort_key(x):
    b = pltpu.bitcast(x + 0.0, I32)
    return b ^ ((b >> 31) & 0x7FFFFFFF)


def _layer_norm(z, g, b):
    mu = jnp.mean(z, axis=-1, keepdims=True)
    zc = z - mu
    var = jnp.mean(zc * zc, axis=-1, keepdims=True)
    return zc * lax.rsqrt(var + LN_EPS) * g + b


def _row_count(keys_ref, n_chunks, ck, rows, pred):
    def body(c, acc):
        base = pl.multiple_of(c * ck, ck)
        blk = keys_ref[:, pl.ds(base, ck)]
        for j in range(ck // LANES):
            hit = pred(blk[:, j * LANES:(j + 1) * LANES], base + j * LANES)
            acc = acc + jnp.where(hit, 1.0, 0.0)
        return acc
    acc = lax.fori_loop(0, n_chunks, body, jnp.zeros((rows, LANES), F32))
    return jnp.sum(acc, axis=1, keepdims=True)


def _select_threshold(keys_ref, cut_ref, n_chunks, ck, rows, kth, idx_bits):
    kth_f = float(kth)

    def count_ge(t):
        tb = jnp.broadcast_to(t, (rows, LANES))
        return _row_count(keys_ref, n_chunks, ck, rows, lambda blk, base: blk >= tb)

    theta = jnp.full((rows, 1), INT_MIN, I32)
    zero = jnp.zeros((rows, 1), I32)
    theta = jnp.where(count_ge(zero) >= kth_f, zero, theta)

    def bit_body(i, t):
        cand = t | lax.shift_left(jnp.int32(1), jnp.int32(30) - i)
        return jnp.where(count_ge(cand) >= kth_f, cand, t)
    theta = lax.fori_loop(0, 31, bit_body, theta)

    n_gt = count_ge(theta + 1)
    n_ge = count_ge(theta)
    need = kth_f - n_gt
    ambiguous = (n_ge - n_gt > need) & (theta > INT_MIN)
    cut_ref[...] = jnp.full(cut_ref.shape, 2 ** idx_bits, I32)

    @pl.when(jnp.max(jnp.where(ambiguous, 1.0, 0.0)) > 0.5)
    def _():
        thb = jnp.broadcast_to(theta, (rows, LANES))
        lane = lax.broadcasted_iota(I32, (rows, LANES), 1)

        def ties_before(jc):
            jb = jnp.broadcast_to(jc, (rows, LANES))
            return _row_count(keys_ref, n_chunks, ck, rows,
                              lambda blk, base: (blk == thb) & (lane + base < jb))

        def jbit(i, jcur):
            cand = jcur | lax.shift_left(jnp.int32(1), jnp.int32(idx_bits - 1) - i)
            return jnp.where(ties_before(cand) <= need, cand, jcur)
        jfin = lax.fori_loop(0, idx_bits, jbit, jnp.zeros((rows, 1), I32))
        jfin = jnp.where(ambiguous, jfin, 2 ** idx_bits)
        cut_ref[...] = jnp.broadcast_to(jfin, cut_ref.shape)
    return theta


def _attn_proj_kernel(x_ref, wq_ref, wkv_ref, wqi_ref, wkw_ref,
                      q_ref, k_ref, v_ref, kb_ref, vb_ref, qi_ref, ki_ref, kib_ref, wi_ref):
    xb = x_ref[...].astype(BF16)
    q = jnp.dot(xb, wq_ref[...], preferred_element_type=F32)
    q_ref[...] = (q * (ATT_HEAD_DIM ** -0.5)).astype(BF16)
    kv = jnp.dot(xb, wkv_ref[...], preferred_element_type=F32)
    k = kv[:, :ATT_KV_DIM]
    v = kv[:, ATT_KV_DIM:]
    k_ref[...] = k
    v_ref[...] = v
    kb_ref[...] = k.astype(BF16)
    vb_ref[...] = v.astype(BF16)
    qi = jnp.dot(xb, wqi_ref[...], preferred_element_type=F32)
    qi_ref[...] = (qi * (IDX_DIM ** -0.5)).astype(BF16)
    kw = jnp.dot(xb, wkw_ref[...], preferred_element_type=F32)
    ki = kw[:, :IDX_DIM]
    ki_ref[...] = ki
    kib_ref[...] = ki.astype(BF16)
    wi_ref[...] = kw[:, IDX_DIM:IDX_DIM + IDX_HEADS] * (IDX_HEADS ** -0.5)


def _attn_weights(w_in):
    s0 = ATT_Q_DIM
    s2 = s0 + 2 * ATT_KV_DIM
    s3 = s2 + IDX_HEADS * IDX_DIM
    wkw = jnp.pad(w_in[:, s3:], ((0, 0), (0, LANES - (w_in.shape[1] - s3))))
    return (w_in[:, :s0].astype(BF16), w_in[:, s0:s2].astype(BF16), w_in[:, s2:s3].astype(BF16),
            wkw.astype(BF16))


def _attn_proj(x2d, w_in):
    m = x2d.shape[0]
    tm = min(m, 512)
    wq, wkv, wqi, wkw = _attn_weights(w_in)
    row = lambda n: pl.BlockSpec((tm, n), lambda i: (i, 0))
    full = lambda a: pl.BlockSpec(a.shape, lambda i: (0, 0))
    outs = [(ATT_Q_DIM, BF16), (ATT_KV_DIM, F32), (ATT_KV_DIM, F32), (ATT_KV_DIM, BF16),
            (ATT_KV_DIM, BF16), (IDX_HEADS * IDX_DIM, BF16), (IDX_DIM, F32), (IDX_DIM, BF16),
            (IDX_HEADS, F32)]
    return pl.pallas_call(
        _attn_proj_kernel,
        grid=(m // tm,),
        in_specs=[row(D_MODEL), full(wq), full(wkv), full(wqi), full(wkw)],
        out_specs=[row(n) for n, _ in outs],
        out_shape=[jax.ShapeDtypeStruct((m, n), dt) for n, dt in outs],
        compiler_params=_cparams(("parallel",)),
        name="attn_proj",
    )(x2d, wq, wkv, wqi, wkw)


def _attn_proj_t_kernel(x_ref, wq_ref, wkv_ref, wqi_ref, wkw_ref,
                        k_ref, v_ref, ki_ref, kb_ref, kib_ref, qt_ref, qit_ref, wit_ref, vbt_ref):
    xb = x_ref[...].astype(BF16)
    q = jnp.dot(xb, wq_ref[...], preferred_element_type=F32)
    qt_ref[...] = (q * (ATT_HEAD_DIM ** -0.5 * LOG2_E)).T.astype(BF16)
    kv = jnp.dot(xb, wkv_ref[...], preferred_element_type=F32)
    k = kv[:, :ATT_KV_DIM]
    v = kv[:, ATT_KV_DIM:]
    k_ref[...] = k
    v_ref[...] = v
    kb_ref[...] = k.astype(BF16)
    vbt_ref[...] = v.T.astype(BF16)
    qi = jnp.dot(xb, wqi_ref[...], preferred_element_type=F32)
    qit_ref[...] = (qi * (IDX_DIM ** -0.5)).T.astype(BF16)
    kw = jnp.dot(xb, wkw_ref[...], preferred_element_type=F32)
    ki = kw[:, :IDX_DIM]
    ki_ref[...] = ki
    kib_ref[...] = ki.astype(BF16)
    wit_ref[...] = kw.T[IDX_DIM:IDX_DIM + IDX_HEADS, :] * (IDX_HEADS ** -0.5)


def _attn_proj_t(x2d, w_in):
    m = x2d.shape[0]
    tm = min(m, 512)
    wq, wkv, wqi, wkw = _attn_weights(w_in)
    row = lambda n: pl.BlockSpec((tm, n), lambda i: (i, 0))
    col = lambda n: pl.BlockSpec((n, tm), lambda i: (0, i))
    full = lambda a: pl.BlockSpec(a.shape, lambda i: (0, 0))
    nat = [(ATT_KV_DIM, F32), (ATT_KV_DIM, F32), (IDX_DIM, F32), (ATT_KV_DIM, BF16), (IDX_DIM, BF16)]
    tr = [(ATT_Q_DIM, BF16), (IDX_HEADS * IDX_DIM, BF16), (IDX_HEADS, F32), (ATT_KV_DIM, BF16)]
    return pl.pallas_call(
        _attn_proj_t_kernel,
        grid=(m // tm,),
        in_specs=[row(D_MODEL), full(wq), full(wkv), full(wqi), full(wkw)],
        out_specs=[row(n) for n, _ in nat] + [col(n) for n, _ in tr],
        out_shape=[jax.ShapeDtypeStruct((m, n), dt) for n, dt in nat] +
                  [jax.ShapeDtypeStruct((n, m), dt) for n, dt in tr],
        compiler_params=_cparams(("parallel",)),
        name="attn_proj_t",
    )(x2d, wq, wkv, wqi, wkw)


def _col_count(keys_ref, n_chunks, ck, pred):
    width = keys_ref.shape[1]
    par = math.gcd(ck, 64)

    def body(c, acc):
        base = pl.multiple_of(c * ck, ck)
        hit = pred(keys_ref[pl.ds(base, ck), :], base)
        return acc + jnp.sum(jnp.where(hit, 1.0, 0.0).reshape(ck // par, par, width), axis=0)
    acc = lax.fori_loop(0, n_chunks, body, jnp.zeros((par, width), F32))
    return jnp.sum(acc, axis=0, keepdims=True)


def _select_threshold_cols(keys_ref, n_chunks, ck, kth, idx_bits):
    width = keys_ref.shape[1]
    kth_f = float(kth)

    def count_ge(t):
        return _col_count(keys_ref, n_chunks, ck, lambda blk, base: blk >= t)

    theta = jnp.full((1, width), INT_MIN, I32)
    zero = jnp.zeros((1, width), I32)
    theta = jnp.where(count_ge(zero) >= kth_f, zero, theta)

    def bit_body(i, t):
        cand = t | lax.shift_left(jnp.int32(1), jnp.int32(30) - i)
        return jnp.where(count_ge(cand) >= kth_f, cand, t)
    theta = lax.fori_loop(0, 31, bit_body, theta)

    n_gt = count_ge(theta + 1)
    n_ge = count_ge(theta)
    need = kth_f - n_gt
    ambiguous = (n_ge - n_gt > need) & (theta > INT_MIN)
    no_cut = jnp.full((1, width), 2 ** idx_bits, I32)

    def tie_search():
        row = lax.broadcasted_iota(I32, (ck, width), 0)

        def ties_before(jc):
            return _col_count(keys_ref, n_chunks, ck, lambda blk, base: (blk == theta) & (row + base < jc))

        def jbit(i, jcur):
            cand = jcur | lax.shift_left(jnp.int32(1), jnp.int32(idx_bits - 1) - i)
            return jnp.where(ties_before(cand) <= need, cand, jcur)
        jfin = lax.fori_loop(0, idx_bits, jbit, jnp.zeros((1, width), I32))
        return jnp.where(ambiguous, jfin, no_cut)

    cut = lax.cond(jnp.max(jnp.where(ambiguous, 1.0, 0.0)) > 0.5, tie_search, lambda: no_cut)
    return theta, cut


def _dsa_prompt_kernel(qt_ref, qit_ref, wit_ref, kb_ref, vbt_ref, kib_ref, o_ref,
                       keys_ref, m_ref, acc_ref, bias_ref, s_ref, p_ref, *, ck, n_sel, idx_bits):
    i = pl.program_id(1)
    qb = Q_BLOCK
    n_chunks = ((i + 1) * qb + ck - 1) // ck
    q_pos = i * qb + lax.broadcasted_iota(I32, (ck, qb), 1)
    row = lax.broadcasted_iota(I32, (ck, qb), 0)

    qit = qit_ref[...]
    qi_all = jnp.concatenate([qit[h * IDX_DIM:(h + 1) * IDX_DIM, :] for h in range(IDX_HEADS)], axis=1)
    wit = wit_ref[...]

    def score_body(c, carry):
        base = pl.multiple_of(c * ck, ck)
        d = jnp.dot(kib_ref[0, pl.ds(base, ck), :], qi_all, preferred_element_type=F32)
        acc = jnp.zeros((ck, qb), F32)
        for h in range(IDX_HEADS):
            acc = acc + jnp.maximum(d[:, h * qb:(h + 1) * qb], 0.0) * wit[h:h + 1, :]
        keys_ref[pl.ds(base, ck), :] = jnp.where(row + base <= q_pos, _sort_key(acc), INT_MIN)
        return carry
    lax.fori_loop(0, n_chunks, score_body, 0)

    theta, cut = _select_threshold_cols(keys_ref, n_chunks, ck, n_sel, idx_bits)

    qt = qt_ref[...]
    qn = [jnp.concatenate([qt[(n * ATT_GROUP + g) * ATT_HEAD_DIM:(n * ATT_GROUP + g + 1) * ATT_HEAD_DIM, :]
                           for g in range(ATT_GROUP)], axis=1) for n in range(ATT_KV_HEADS)]
    m_ref[...] = jnp.full(m_ref.shape, NEG_INF, F32)
    acc_ref[...] = jnp.zeros(acc_ref.shape, F32)
    hd = ATT_HEAD_DIM
    ones = jnp.ones((acc_ref.shape[1] - hd, ck), BF16)
    par = math.gcd(ck, 64)

    def scores(kbase, n):
        kc = kb_ref[0, pl.ds(kbase, ck), n * hd:(n + 1) * hd]
        s_ref[n % 2] = jnp.dot(kc, qn[n], preferred_element_type=F32)
    scores(0, 0)

    def attn_body(c, carry):
        base = pl.multiple_of(c * ck, ck)
        key = keys_ref[pl.ds(base, ck), :]
        sel = ((key > theta) | ((key == theta) & (row + base < cut))) & (key != INT_MIN)
        bias_ref[...] = jnp.where(sel, 0.0, NEG_INF)

        for n in range(ATT_KV_HEADS):
            if n + 1 < ATT_KV_HEADS:
                scores(base, n + 1)
            else:
                scores(pl.multiple_of(jnp.minimum(c + 1, n_chunks - 1) * ck, ck), 0)
            vtc = jnp.concatenate([vbt_ref[n * hd:(n + 1) * hd, pl.ds(base, ck)], ones], axis=0)
            sn_ref = s_ref.at[n % 2]

            def masked(j):
                rs = slice(j * par, (j + 1) * par)
                return sn_ref[rs, :] + jnp.concatenate([bias_ref[rs, :]] * ATT_GROUP, axis=1)
            mx = masked(0)
            for j in range(1, ck // par):
                mx = jnp.maximum(mx, masked(j))
            m_prev = m_ref[n]
            m_new = jnp.maximum(m_prev, jnp.max(mx, axis=0, keepdims=True))
            m_safe = jnp.where(m_new == NEG_INF, 0.0, m_new)
            alpha = jnp.exp2(m_prev - m_safe)
            for j in range(ck // par):
                p_ref[j * par:(j + 1) * par, :] = jnp.exp2((masked(j) - m_safe).astype(BF16))
            acc_ref[n] = alpha * acc_ref[n] + jnp.dot(vtc, p_ref[...], preferred_element_type=F32)
            m_ref[n] = m_new
        return carry
    lax.fori_loop(0, n_chunks, attn_body, 0)

    ot = jnp.concatenate([acc_ref[n, :hd] / acc_ref[n, hd:hd + 1] for n in range(ATT_KV_HEADS)], axis=0)
    for g in range(ATT_GROUP):
        og = ot[:, g * qb:(g + 1) * qb].T
        for n in range(ATT_KV_HEADS):
            h = n * ATT_GROUP + g
            o_ref[:, h * ATT_HEAD_DIM:(h + 1) * ATT_HEAD_DIM] = (
                og[:, n * ATT_HEAD_DIM:(n + 1) * ATT_HEAD_DIM].astype(o_ref.dtype))


def _dsa_prompt(qt, qit, wit, kb, vbt, kib, b, n_sel):
    t = kb.shape[1]
    ck = min(512, t)
    nqb = t // Q_BLOCK
    idx_bits = max(1, math.ceil(math.log2(t))) + 1
    colblk = lambda n: pl.BlockSpec((n, Q_BLOCK), lambda bi, i: (0, bi * nqb + i))
    res = lambda n: pl.BlockSpec((1, t, n), lambda bi, i: (bi, 0, 0))
    gq = ATT_GROUP * Q_BLOCK
    return pl.pallas_call(
        functools.partial(_dsa_prompt_kernel, ck=ck, n_sel=n_sel, idx_bits=idx_bits),
        grid=(b, nqb),
        in_specs=[colblk(ATT_Q_DIM), colblk(IDX_HEADS * IDX_DIM), colblk(IDX_HEADS),
                  res(ATT_KV_DIM), pl.BlockSpec((ATT_KV_DIM, t), lambda bi, i: (0, bi)), res(IDX_DIM)],
        out_specs=pl.BlockSpec((Q_BLOCK, ATT_Q_DIM), lambda bi, i: (bi * nqb + i, 0)),
        out_shape=jax.ShapeDtypeStruct((b * t, ATT_Q_DIM), BF16),
        scratch_shapes=[pltpu.VMEM((t, Q_BLOCK), I32),
                        pltpu.VMEM((ATT_KV_HEADS, 1, gq), F32),
                        pltpu.VMEM((ATT_KV_HEADS, ATT_HEAD_DIM + BF16_ROWS, gq), F32),
                        pltpu.VMEM((ck, Q_BLOCK), F32), pltpu.VMEM((2, ck, gq), F32),
                        pltpu.VMEM((ck, gq), BF16)],
        compiler_params=_cparams(("parallel", "arbitrary"), 56),
        name="dsa_prompt",
    )(qt, qit, wit, kb, vbt, kib)


def _outproj_ln_kernel(o_ref, h_ref, w_ref, g_ref, b_ref, out_ref):
    y = jnp.dot(o_ref[...], w_ref[...], preferred_element_type=F32)
    out_ref[...] = _layer_norm(DEEPNORM_ALPHA * h_ref[...] + y, g_ref[...], b_ref[...])


def _outproj_ln(o2d, h2d, w_out, g, b):
    m, kdim = o2d.shape
    tm = min(m, 512)
    row = lambda n: pl.BlockSpec((tm, n), lambda i: (i, 0))
    return pl.pallas_call(
        _outproj_ln_kernel,
        grid=(m // tm,),
        in_specs=[row(kdim), row(D_MODEL), pl.BlockSpec((kdim, D_MODEL), lambda i: (0, 0)),
                  pl.BlockSpec((1, D_MODEL), lambda i: (0, 0)), pl.BlockSpec((1, D_MODEL), lambda i: (0, 0))],
        out_specs=row(D_MODEL),
        out_shape=jax.ShapeDtypeStruct((m, D_MODEL), F32),
        compiler_params=_cparams(("parallel",)),
        name="outproj_ln",
    )(o2d, h2d, w_out.astype(BF16), g.reshape(1, -1), b.reshape(1, -1))


def _dsa_sample_kernel(pt_ref, qbd_ref, qi_ref, wi_ref, kn_ref, vn_ref, kin_ref,
                       cki_hbm, ck_hbm, cv_hbm, o_ref,
                       ki_buf, k_buf, v_buf, s_ref, keys_ref, cut_ref, sem,
                       *, n_pages, t_new, n_sel, idx_bits, layer_off, lchunk):
    b = pl.program_id(0)
    past = n_pages * PAGE_SIZE
    ltot = past + LANES
    rows = keys_ref.shape[0]

    def page_copy(hbm, buf, p, s):
        page = pt_ref[b, p] + layer_off
        return pltpu.make_async_copy(hbm.at[page], buf.at[pl.ds(pl.multiple_of(p * PAGE_SIZE, PAGE_SIZE), PAGE_SIZE)],
                                     sem.at[s])

    def start_all(p, c):
        page_copy(cki_hbm, ki_buf, p, 0).start()
        page_copy(ck_hbm, k_buf, p, 1).start()
        page_copy(cv_hbm, v_buf, p, 2).start()
        return c
    lax.fori_loop(0, n_pages, start_all, 0)

    def wait_all(hbm, buf, s):
        def body(p, c):
            page_copy(hbm, buf, p, s).wait()
            return c
        lax.fori_loop(0, n_pages, body, 0)

    qi = qi_ref[0]
    wcol = wi_ref[0]
    wait_all(cki_hbm, ki_buf, 0)

    def idx_scores(kic):
        d = lax.dot_general(qi, kic, NT_DIMS, preferred_element_type=F32)
        d = jnp.maximum(d, 0.0) * wcol
        return jnp.sum(d.reshape(t_new, IDX_HEADS, d.shape[1]), axis=1)

    keys_ref[...] = jnp.full(keys_ref.shape, INT_MIN, I32)
    for c in range(past // lchunk):
        sc = idx_scores(ki_buf[c * lchunk:(c + 1) * lchunk, :].astype(BF16))
        keys_ref[0:t_new, c * lchunk:(c + 1) * lchunk] = _sort_key(sc)
    scn = idx_scores(kin_ref[0])
    tpos = lax.broadcasted_iota(I32, (t_new, LANES), 0)
    lpos = lax.broadcasted_iota(I32, (t_new, LANES), 1)
    keys_ref[0:t_new, past:ltot] = jnp.where((lpos <= tpos) & (lpos < t_new), _sort_key(scn), INT_MIN)

    theta = _select_threshold(keys_ref, cut_ref, ltot // LANES, LANES, rows, n_sel, idx_bits)
    key = keys_ref[...]
    lane = lax.broadcasted_iota(I32, (rows, ltot), 1)
    sel = ((key > theta) | ((key == theta) & (lane < cut_ref[:, :1]))) & (key != INT_MIN)
    bias = jnp.where(sel, 0.0, NEG_INF)[0:t_new]
    bias = jnp.broadcast_to(bias[:, None, :], (t_new, ATT_HEADS, ltot)).reshape(t_new * ATT_HEADS, ltot)

    qbd = qbd_ref[0]
    wait_all(ck_hbm, k_buf, 1)
    for c in range(past // lchunk):
        kc = k_buf[c * lchunk:(c + 1) * lchunk, :].astype(BF16)
        s_ref[:, c * lchunk:(c + 1) * lchunk] = lax.dot_general(qbd, kc, NT_DIMS, preferred_element_type=F32)
    s_ref[:, past:ltot] = lax.dot_general(qbd, kn_ref[0], NT_DIMS, preferred_element_type=F32)
    s = s_ref[...] + bias
    m = jnp.max(s, axis=1, keepdims=True)
    p = jnp.exp(s - m)
    linv = 1.0 / jnp.sum(p, axis=1, keepdims=True)
    s_ref[...] = p
    wait_all(cv_hbm, v_buf, 2)
    o = jnp.dot(s_ref[:, past:ltot].astype(BF16), vn_ref[0], preferred_element_type=F32)
    for c in range(past // lchunk):
        vc = v_buf[c * lchunk:(c + 1) * lchunk, :].astype(BF16)
        o = o + jnp.dot(s_ref[:, c * lchunk:(c + 1) * lchunk].astype(BF16), vc, preferred_element_type=F32)
    o = o * linv
    nrow = (lax.broadcasted_iota(I32, (t_new * ATT_HEADS, ATT_HEAD_DIM), 0) % ATT_HEADS) // ATT_GROUP
    out = jnp.zeros((t_new * ATT_HEADS, ATT_HEAD_DIM), F32)
    for n in range(ATT_KV_HEADS):
        out = out + jnp.where(nrow == n, o[:, n * ATT_HEAD_DIM:(n + 1) * ATT_HEAD_DIM], 0.0)
    o_ref[0] = out.astype(o_ref.dtype)


def _dsa_sample(page_table, qs, qis, wi, kb, vb, kib, cache_ki, cache_k, cache_v, layer_off, n_sel):
    b, t, _ = qs.shape
    n_pages = page_table.shape[1]
    past = n_pages * PAGE_SIZE
    ltot = past + LANES
    rows = 8
    q5 = qs.reshape(b, t, ATT_KV_HEADS, ATT_GROUP, 1, ATT_HEAD_DIM)
    eye = jnp.eye(ATT_KV_HEADS, dtype=qs.dtype).reshape(1, 1, ATT_KV_HEADS, 1, ATT_KV_HEADS, 1)
    qbd = (q5 * eye).reshape(b, t * ATT_HEADS, ATT_KV_DIM)
    qi = qis.reshape(b, t * IDX_HEADS, IDX_DIM)
    wcol = wi.reshape(b, t * IDX_HEADS, 1)
    padr = lambda a: jnp.pad(a, ((0, 0), (0, LANES - t), (0, 0)))
    kn, vn, kin = padr(kb), padr(vb), padr(kib)
    idx_bits = max(1, math.ceil(math.log2(ltot))) + 1
    lchunk = min(1024, past)
    bl = lambda a: pl.BlockSpec((1,) + a.shape[1:], lambda i, pt: (i, 0, 0))
    anyspec = pl.BlockSpec(memory_space=pl.ANY)
    grid_spec = pltpu.PrefetchScalarGridSpec(
        num_scalar_prefetch=1,
        grid=(b,),
        in_specs=[bl(qbd), bl(qi), bl(wcol), bl(kn), bl(vn), bl(kin), anyspec, anyspec, anyspec],
        out_specs=pl.BlockSpec((1, t * ATT_HEADS, ATT_HEAD_DIM), lambda i, pt: (i, 0, 0)),
        scratch_shapes=[pltpu.VMEM((past, IDX_DIM), F32), pltpu.VMEM((past, ATT_KV_DIM), F32),
                        pltpu.VMEM((past, ATT_KV_DIM), F32), pltpu.VMEM((t * ATT_HEADS, ltot), F32),
                        pltpu.VMEM((rows, ltot), I32), pltpu.VMEM((rows, LANES), I32),
                        pltpu.SemaphoreType.DMA((3,))],
    )
    o = pl.pallas_call(
        functools.partial(_dsa_sample_kernel, n_pages=n_pages, t_new=t, n_sel=n_sel, idx_bits=idx_bits,
                          layer_off=layer_off, lchunk=lchunk),
        grid_spec=grid_spec,
        out_shape=jax.ShapeDtypeStruct((b, t * ATT_HEADS, ATT_HEAD_DIM), BF16),
        compiler_params=_cparams(("arbitrary",), 56),
        name="dsa_sample",
    )(page_table, qbd, qi, wcol, kn, vn, kin, cache_ki, cache_k, cache_v)
    return o.reshape(b, t, ATT_Q_DIM)


def _gla_proj_kernel(x_ref, wq_ref, wk_ref, wv_ref, wg_ref, wr_ref, wup_ref, bg_ref,
                     q_ref, k_ref, v_ref, la_ref, r_ref):
    xb = x_ref[...].astype(BF16)
    q_ref[...] = jnp.dot(xb, wq_ref[...], preferred_element_type=F32) * (GLA_DK ** -0.5)
    k_ref[...] = jnp.dot(xb, wk_ref[...], preferred_element_type=F32)
    v_ref[...] = jnp.dot(xb, wv_ref[...], preferred_element_type=F32)
    r_ref[...] = jnp.dot(xb, wr_ref[...], preferred_element_type=F32)
    gd = jnp.dot(xb, wg_ref[...], preferred_element_type=F32)
    glogit = jnp.dot(gd.astype(BF16), wup_ref[...], preferred_element_type=F32) + bg_ref[...]
    la_ref[...] = jax.nn.log_sigmoid(glogit) / GLA_TAU


def _gla_proj(x2d, w_in, w_gate_up, b_gate):
    m = x2d.shape[0]
    tm = min(m, 512)
    dq = GLA_HEADS * GLA_DK
    dv = GLA_HEADS * GLA_DV
    wq = w_in[:, :dq].astype(BF16)
    wk = w_in[:, dq:2 * dq].astype(BF16)
    wv = w_in[:, 2 * dq:2 * dq + dv].astype(BF16)
    wg = jnp.pad(w_in[:, 2 * dq + dv:2 * dq + dv + GLA_GATE_RANK], ((0, 0), (0, LANES - GLA_GATE_RANK))).astype(BF16)
    wr = w_in[:, 2 * dq + dv + GLA_GATE_RANK:].astype(BF16)
    wup = jnp.pad(w_gate_up, ((0, LANES - GLA_GATE_RANK), (0, 0))).astype(BF16)
    row = lambda n: pl.BlockSpec((tm, n), lambda i: (i, 0))
    full = lambda a: pl.BlockSpec(a.shape, lambda i: (0, 0))
    bg = b_gate.reshape(1, -1)
    outs = [dq, dq, dv, dq, dv]
    return pl.pallas_call(
        _gla_proj_kernel,
        grid=(m // tm,),
        in_specs=[row(D_MODEL), full(wq), full(wk), full(wv), full(wg), full(wr), full(wup), full(bg)],
        out_specs=[row(n) for n in outs],
        out_shape=[jax.ShapeDtypeStruct((m, n), F32) for n in outs],
        compiler_params=_cparams(("parallel",)),
        name="gla_proj",
    )(x2d, wq, wk, wv, wg, wr, wup, bg)


def _gla_chunk(q, k, v, g, st, c):
    sub = GLA_SUB
    n_sub = c // sub
    ri = lax.broadcasted_iota(I32, (c, c), 0)
    ci = lax.broadcasted_iota(I32, (c, c), 1)
    tri = jnp.where(ci <= ri, 1.0, 0.0).astype(F32)
    bc = jnp.dot(tri, g, preferred_element_type=F32, precision=lax.Precision.HIGHEST)
    o = lax.dot_general((q * jnp.exp(bc)).astype(BF16), st.astype(BF16), NT_DIMS, preferred_element_type=F32)
    if n_sub > 1:
        mrow = jnp.concatenate([jnp.zeros((sub, GLA_DK), F32)] +
                               [jnp.broadcast_to(bc[i * sub - 1:i * sub], (sub, GLA_DK)) for i in range(1, n_sub)], axis=0)
        qh = (q * jnp.exp(bc - mrow)).astype(BF16)
        parts = [jnp.zeros((sub, GLA_DV), F32)]
        for i in range(1, n_sub):
            kh = (k[:i * sub] * jnp.exp(bc[i * sub - 1:i * sub] - bc[:i * sub])).astype(BF16)
            a = lax.dot_general(qh[i * sub:(i + 1) * sub], kh, NT_DIMS, preferred_element_type=F32)
            parts.append(jnp.dot(a.astype(BF16), v[:i * sub].astype(BF16), preferred_element_type=F32))
        o = o + jnp.concatenate(parts, axis=0)
    rsub = lax.broadcasted_iota(I32, (c, 1), 0) % sub
    for dlt in range(sub):
        ks = k if dlt == 0 else pltpu.roll(k, dlt, 0)
        bs = bc if dlt == 0 else pltpu.roll(bc, dlt, 0)
        vs = v if dlt == 0 else pltpu.roll(v, dlt, 0)
        w = jnp.sum(q * ks * jnp.exp(jnp.minimum(bc - bs, 0.0)), axis=1, keepdims=True)
        o = o + jnp.where(rsub >= dlt, w, 0.0) * vs
    bl = bc[c - 1:c]
    kt = (k * jnp.exp(bl - bc)).astype(BF16)
    st_new = st * jnp.exp(bl) + lax.dot_general(v.astype(BF16), kt, TN_DIMS, preferred_element_type=F32)
    return o, st_new


def _gla_kernel(q_ref, k_ref, v_ref, g_ref, r_ref, ng_ref, s0_ref, o_ref, sf_ref, st_ref, *, c, n_c):
    tb = pl.program_id(2)

    @pl.when(tb == 0)
    def _():
        st_ref[...] = s0_ref[0, 0].T

    st = st_ref[...]
    for ci in range(n_c):
        sl = slice(ci * c, (ci + 1) * c)
        o, st = _gla_chunk(q_ref[0, sl, :], k_ref[0, sl, :], v_ref[0, sl, :], g_ref[0, sl, :], st, c)
        o = o * lax.rsqrt(jnp.mean(o * o, axis=-1, keepdims=True) + LN_EPS) * ng_ref[...]
        o_ref[0, sl, :] = (o * jax.nn.silu(r_ref[0, sl, :])).astype(o_ref.dtype)
    st_ref[...] = st

    @pl.when(tb == pl.num_programs(2) - 1)
    def _():
        sf_ref[0, 0] = st.T


def _gla(q, k, v, la, r, norm_g, s0):
    b, t, _ = q.shape
    c = math.gcd(t, GLA_CHUNK)
    tblk = math.gcd(t, 4 * GLA_CHUNK)
    n_c = tblk // c
    kspec = pl.BlockSpec((1, tblk, GLA_DK), lambda bi, h, i: (bi, i, h))
    vspec = pl.BlockSpec((1, tblk, GLA_DV), lambda bi, h, i: (bi, i, h))
    sspec = pl.BlockSpec((1, 1, GLA_DK, GLA_DV), lambda bi, h, i: (bi, h, 0, 0))
    return pl.pallas_call(
        functools.partial(_gla_kernel, c=c, n_c=n_c),
        grid=(b, GLA_HEADS, t // tblk),
        in_specs=[kspec, kspec, vspec, kspec, vspec, pl.BlockSpec((1, GLA_DV), lambda bi, h, i: (0, 0)), sspec],
        out_specs=[vspec, sspec],
        out_shape=[jax.ShapeDtypeStruct((b, t, GLA_HEADS * GLA_DV), BF16),
                   jax.ShapeDtypeStruct((b, GLA_HEADS, GLA_DK, GLA_DV), F32)],
        scratch_shapes=[pltpu.VMEM((GLA_DV, GLA_DK), F32)],
        compiler_params=_cparams(("parallel", "parallel", "arbitrary")),
        name="gla_scan",
    )(q, k, v, la, r, norm_g.reshape(1, -1), s0)


def _gla_mixer(x, s0, w_in, w_gate_up, b_gate, norm_g):
    b, t, _ = x.shape
    q, k, v, la, r = _gla_proj(x.reshape(b * t, D_MODEL), w_in, w_gate_up, b_gate)
    tp = -(-t // GLA_SUB) * GLA_SUB
    r3 = lambda a: jnp.pad(a.reshape(b, t, -1), ((0, 0), (0, tp - t), (0, 0)))
    o, sf = _gla(r3(q), r3(k), r3(v), r3(la), r3(r), norm_g, s0)
    return o[:, :t], sf


def _top_values(s, k):
    vals = []
    cur = s
    for _ in range(k):
        mx = jnp.max(cur, axis=0, keepdims=True)
        vals.append(mx)
        cur = jnp.where(cur == mx, NEG_INF, cur)
    return vals


def _top_values_ranked(s, k):
    vals = []
    cur = s
    rank = jnp.full(s.shape, float(k), F32)
    for i in range(k):
        mx = jnp.max(cur, axis=0, keepdims=True)
        vals.append(mx)
        top = cur == mx
        rank = jnp.where(top, float(i), rank)
        cur = jnp.where(top, NEG_INF, cur)
    return vals, rank


def _peer_select_kernel(x_ref, wq_ref, k1_ref, k2_ref, c1_ref, p_ref, r2_ref, e_ref):
    xb = x_ref[...].astype(BF16)
    q = jnp.dot(xb, wq_ref[...], preferred_element_type=F32).astype(BF16)
    tt = q.shape[0]
    k1 = k1_ref[...]
    k2 = k2_ref[...]
    sub8 = lax.broadcasted_iota(I32, (8, tt), 0)
    for h in range(PEER_HEADS):
        s1 = lax.dot_general(k1, q[:, h * PEER_D_KEY:h * PEER_D_KEY + PEER_HALF], NT_DIMS,
                             preferred_element_type=F32)
        s2 = lax.dot_general(k2, q[:, h * PEER_D_KEY + PEER_HALF:(h + 1) * PEER_D_KEY], NT_DIMS,
                             preferred_element_type=F32)
        t1, rank1 = _top_values_ranked(s1, PEER_TOPK)
        t2, rank2 = _top_values_ranked(s2, PEER_TOPK)
        t2a = jnp.concatenate(t2, axis=0)
        cands = [t1[0] + t2a]
        for a in range(1, PEER_TOPK):
            nb = PEER_TOPK // (a + 1)
            cands.append(jnp.where(sub8 < nb, t1[a] + t2a[:8], NEG_INF))
        best = _top_values(jnp.concatenate(cands, axis=0), PEER_TOPK)
        theta = best[PEER_TOPK - 1]
        z = best[0] * 0.0
        for bv in best:
            z = z + jnp.exp(bv - best[0])
        count1 = jnp.zeros(s1.shape, F32)
        for a in range(PEER_TOPK):
            cnt_a = jnp.sum(jnp.where(t1[a] + t2a >= theta, 1.0, 0.0), axis=0, keepdims=True)
            count1 = jnp.where(rank1 == float(a), cnt_a, count1)
        c1_ref[h] = count1
        p_ref[h] = jnp.exp(s1 - t1[0]) * (0.5 / z)
        r2_ref[h] = rank2.astype(r2_ref.dtype)
        e_ref[h] = jnp.exp(s2 - t2[0]).astype(e_ref.dtype)


def _peer_select(x2d, w_q, keys1, keys2):
    n = x2d.shape[0]
    tt = min(n, 256)
    wq = w_q.astype(BF16)
    k1 = keys1.astype(BF16)
    k2 = keys2.astype(BF16)
    full = lambda a: pl.BlockSpec(a.shape, lambda i: (0,) * a.ndim)
    hspec = pl.BlockSpec((PEER_HEADS, PEER_N_KEYS, tt), lambda i: (0, 0, i))
    big = lambda dt: jax.ShapeDtypeStruct((PEER_HEADS, PEER_N_KEYS, n), dt)
    return pl.pallas_call(
        _peer_select_kernel,
        grid=(n // tt,),
        in_specs=[pl.BlockSpec((tt, D_MODEL), lambda i: (i, 0)), full(wq), full(k1), full(k2)],
        out_specs=[hspec, hspec, hspec, hspec],
        out_shape=[big(F32), big(F32), big(F32), big(F32)],
        compiler_params=_cparams(("parallel",)),
        name="peer_select",
    )(x2d, wq, k1, k2)


def _peer_mix_kernel(h_ref, c1_ref, p_ref, r2_ref, e_ref, u_ref, vt_ref, g_ref, bb_ref, out_ref,
                     xb_ref, ht_ref, wt_ref, yt_ref, *, n_i1):
    eb = pl.program_id(1)
    tt = h_ref.shape[0]

    @pl.when(eb == 0)
    def _():
        xb_ref[...] = h_ref[...].T.astype(BF16)
        yt_ref[...] = jnp.zeros(yt_ref.shape, F32)

    ht_ref[...] = jnp.dot(u_ref[...], xb_ref[...], preferred_element_type=F32)

    i1_base = pl.multiple_of(eb * n_i1, n_i1)
    for lg in range(tt // LANES):
        ls = slice(lg * LANES, (lg + 1) * LANES)
        c8 = [c1_ref[h, pl.ds(i1_base, n_i1), ls] for h in range(PEER_HEADS)]
        p8 = [p_ref[h, pl.ds(i1_base, n_i1), ls] for h in range(PEER_HEADS)]
        for j in range(n_i1):
            cb = [jnp.broadcast_to(c8[h][j:j + 1], (BF16_ROWS, LANES)) for h in range(PEER_HEADS)]
            pb = [jnp.broadcast_to(p8[h][j:j + 1], (BF16_ROWS, LANES)) for h in range(PEER_HEADS)]
            for r in range(PEER_N_KEYS // BF16_ROWS):
                ks = slice(r * BF16_ROWS, (r + 1) * BF16_ROWS)
                acc = jnp.zeros((BF16_ROWS, LANES), F32)
                for h in range(PEER_HEADS):
                    acc = acc + jnp.where(r2_ref[h, ks, ls] < cb[h], pb[h] * e_ref[h, ks, ls], 0.0)
                rs = slice(j * PEER_N_KEYS + r * BF16_ROWS, j * PEER_N_KEYS + (r + 1) * BF16_ROWS)
                x = ht_ref[rs, ls]
                gelu2 = x * (1.0 + lax.erf(x * (2.0 ** -0.5)))
                wt_ref[rs, ls] = (gelu2 * acc).astype(BF16)

    yt_ref[...] += jnp.dot(vt_ref[...], wt_ref[...], preferred_element_type=F32)

    @pl.when(eb == pl.num_programs(1) - 1)
    def _():
        y = yt_ref[...].T
        out_ref[...] = _layer_norm(DEEPNORM_ALPHA * h_ref[...] + y, g_ref[...], bb_ref[...])


def _peer_ln(h2d, w_q, keys1, keys2, u, v, g, b):
    n_real = h2d.shape[0]
    if n_real % LANES:
        h2d = jnp.pad(h2d, ((0, LANES - n_real % LANES), (0, 0)))
        return _peer_ln(h2d, w_q, keys1, keys2, u, v, g, b)[:n_real]
    n = n_real
    c1, p, r2, e = _peer_select(h2d, w_q, keys1, keys2)
    tt = min(n, 512)
    eblk = 1024
    n_e = u.shape[0]
    ub = u.astype(BF16)
    vt = v.T.astype(BF16)
    hspec = pl.BlockSpec((PEER_HEADS, PEER_N_KEYS, tt), lambda i, j: (0, 0, i))
    vec = pl.BlockSpec((1, D_MODEL), lambda i, j: (0, 0))
    return pl.pallas_call(
        functools.partial(_peer_mix_kernel, n_i1=eblk // PEER_N_KEYS),
        grid=(n // tt, n_e // eblk),
        in_specs=[pl.BlockSpec((tt, D_MODEL), lambda i, j: (i, 0)), hspec, hspec, hspec, hspec,
                  pl.BlockSpec((eblk, D_MODEL), lambda i, j: (j, 0)),
                  pl.BlockSpec((D_MODEL, eblk), lambda i, j: (0, j)), vec, vec],
        out_specs=pl.BlockSpec((tt, D_MODEL), lambda i, j: (i, 0)),
        out_shape=jax.ShapeDtypeStruct((n, D_MODEL), F32),
        scratch_shapes=[pltpu.VMEM((D_MODEL, tt), BF16), pltpu.VMEM((eblk, tt), F32),
                        pltpu.VMEM((eblk, tt), BF16), pltpu.VMEM((D_MODEL, tt), F32)],
        compiler_params=_cparams(("parallel", "arbitrary"), 56),
        name="peer_mix",
    )(h2d, c1, p, r2, e, ub, vt, g.reshape(1, -1), b.reshape(1, -1))


def kernel(x_prompt, x_sample, cache_k, cache_v, cache_kidx, state_gla, page_table,
           attn_w_in, attn_w_out, gla_w_in, gla_w_gate_up, gla_b_gate, gla_norm_g, gla_w_out,
           peer_w_q, peer_keys1, peer_keys2, peer_u, peer_v, ln1_g, ln1_b, ln2_g, ln2_b):
    bp, tp, _ = x_prompt.shape
    bs, ts, _ = x_sample.shape
    n_pool = cache_k.shape[1]
    past = page_table.shape[1] * PAGE_SIZE
    ck = cache_k.reshape(-1, PAGE_SIZE, ATT_KV_DIM)
    cv = cache_v.reshape(-1, PAGE_SIZE, ATT_KV_DIM)
    cki = cache_kidx.reshape(-1, PAGE_SIZE, IDX_DIM)
    hp = x_prompt.reshape(bp * tp, D_MODEL)
    hs = x_sample.reshape(bs * ts, D_MODEL)
    kp_l, vp_l, kip_l, sp_l = [], [], [], []
    ks_l, vs_l, kis_l, ss_l = [], [], [], []
    for i in range(DEPTH):
        j = i // 2
        if i % 2 == 0:
            k, v, ki, kb, kib, qt, qit, wit, vbt = _attn_proj_t(hp, attn_w_in[j])
            r3 = lambda a: a.reshape(bp, tp, -1)
            op = _dsa_prompt(qt, qit, wit, r3(kb), vbt, r3(kib), bp, min(TOPK_MAX, tp // 4))
            kp_l.append(k.reshape(bp, tp, ATT_KV_HEADS, ATT_HEAD_DIM))
            vp_l.append(v.reshape(bp, tp, ATT_KV_HEADS, ATT_HEAD_DIM))
            kip_l.append(ki.reshape(bp, tp, IDX_DIM))
            qs, k, v, kb, vb, qis, ki, kib, wi = _attn_proj(hs, attn_w_in[j])
            r3 = lambda a: a.reshape(bs, ts, -1)
            os_ = _dsa_sample(page_table, r3(qs), r3(qis), r3(wi), r3(kb), r3(vb), r3(kib), cki, ck, cv,
                              j * n_pool, min(TOPK_MAX, (past + ts) // 4))
            ks_l.append(k.reshape(bs, ts, ATT_KV_HEADS, ATT_HEAD_DIM))
            vs_l.append(v.reshape(bs, ts, ATT_KV_HEADS, ATT_HEAD_DIM))
            kis_l.append(ki.reshape(bs, ts, IDX_DIM))
            w_out = attn_w_out[j]
        else:
            s0 = jnp.zeros((bp, GLA_HEADS, GLA_DK, GLA_DV), F32)
            op, sp = _gla_mixer(hp.reshape(bp, tp, D_MODEL), s0, gla_w_in[j], gla_w_gate_up[j], gla_b_gate[j],
                                gla_norm_g[j])
            os_, ss = _gla_mixer(hs.reshape(bs, ts, D_MODEL), state_gla[j], gla_w_in[j], gla_w_gate_up[j],
                                 gla_b_gate[j], gla_norm_g[j])
            sp_l.append(sp)
            ss_l.append(ss)
            w_out = gla_w_out[j]
        hp = _outproj_ln(op.reshape(bp * tp, -1), hp, w_out, ln1_g[i], ln1_b[i])
        hs = _outproj_ln(os_.reshape(bs * ts, -1), hs, w_out, ln1_g[i], ln1_b[i])
        pw = (peer_w_q[i], peer_keys1[i], peer_keys2[i], peer_u[i], peer_v[i], ln2_g[i], ln2_b[i])
        hp = _peer_ln(hp, *pw)
        hs = _peer_ln(hs, *pw)
    return (hp.reshape(bp, tp, D_MODEL), hs.reshape(bs, ts, D_MODEL),
            jnp.stack(kp_l), jnp.stack(vp_l), jnp.stack(kip_l), jnp.stack(sp_l),
            jnp.stack(ks_l), jnp.stack(vs_l), jnp.stack(kis_l), jnp.stack(ss_l))
```

```python
import functools
import math

import jax
import jax.numpy as jnp
from jax import lax
from jax.experimental import pallas as pl
from jax.experimental.pallas import tpu as pltpu

F32 = jnp.float32
BF16 = jnp.bfloat16
I32 = jnp.int32

D_MODEL = 1024
DEPTH = 2
PAGE_SIZE = 128
DEEPNORM_ALPHA = (2.0 * DEPTH) ** 0.25
LN_EPS = 1e-5

ATT_HEADS = 16
ATT_KV_HEADS = 4
ATT_HEAD_DIM = 64
ATT_GROUP = ATT_HEADS // ATT_KV_HEADS
IDX_HEADS = 8
IDX_DIM = 64
TOPK_MAX = 256
Q_BLOCK = 128
ATT_Q_DIM = ATT_HEADS * ATT_HEAD_DIM
ATT_KV_DIM = ATT_KV_HEADS * ATT_HEAD_DIM

GLA_HEADS = 4
GLA_DK = D_MODEL // 2 // GLA_HEADS
GLA_DV = D_MODEL // GLA_HEADS
GLA_GATE_RANK = 16
GLA_TAU = 16.0
GLA_CHUNK = 64
GLA_SUB = 16

PEER_HEADS = 8
PEER_N_KEYS = 128
PEER_D_KEY = 256
PEER_HALF = PEER_D_KEY // 2
PEER_TOPK = 16

LANES = 128
BF16_ROWS = 16
LOG2_E = 1.4426950408889634
INT_MIN = -(2 ** 31)
NEG_INF = float("-inf")

NN_DIMS = (((1,), (0,)), ((), ()))
NT_DIMS = (((1,), (1,)), ((), ()))
TN_DIMS = (((0,), (0,)), ((), ()))


def _cparams(sem, vmem_mib=None, flags=None):
    kw = dict(dimension_semantics=sem)
    if vmem_mib is not None:
        kw["vmem_limit_bytes"] = vmem_mib * 1024 * 1024
    if flags:
        kw["flags"] = flags
    return pltpu.CompilerParams(**kw)


def _sort_key(x):
    b = pltpu.bitcast(x + 0.0, I32)
    return b ^ ((b >> 31) & 0x7FFFFFFF)


def _layer_norm(z, g, b):
    mu = jnp.mean(z, axis=-1, keepdims=True)
    zc = z - mu
    var = jnp.mean(zc * zc, axis=-1, keepdims=True)
    return zc * lax.rsqrt(var + LN_EPS) * g + b


def _row_count(keys_ref, n_chunks, ck, rows, pred):
    def body(c, acc):
        base = pl.multiple_of(c * ck, ck)
        blk = keys_ref[:, pl.ds(base, ck)]
        for j in range(ck // LANES):
            hit = pred(blk[:, j * LANES:(j + 1) * LANES], base + j * LANES)
            acc = acc + jnp.where(hit, 1.0, 0.0)
        return acc
    acc = lax.fori_loop(0, n_chunks, body, jnp.zeros((rows, LANES), F32))
    return jnp.sum(acc, axis=1, keepdims=True)


def _select_threshold(keys_ref, cut_ref, n_chunks, ck, rows, kth, idx_bits):
    kth_f = float(kth)

    def count_ge(t):
        tb = jnp.broadcast_to(t, (rows, LANES))
        return _row_count(keys_ref, n_chunks, ck, rows, lambda blk, base: blk >= tb)

    theta = jnp.full((rows, 1), INT_MIN, I32)
    zero = jnp.zeros((rows, 1), I32)
    theta = jnp.where(count_ge(zero) >= kth_f, zero, theta)

    def bit_body(i, t):
        cand = t | lax.shift_left(jnp.int32(1), jnp.int32(30) - i)
        return jnp.where(count_ge(cand) >= kth_f, cand, t)
    theta = lax.fori_loop(0, 31, bit_body, theta)

    n_gt = count_ge(theta + 1)
    n_ge = count_ge(theta)
    need = kth_f - n_gt
    ambiguous = (n_ge - n_gt > need) & (theta > INT_MIN)
    cut_ref[...] = jnp.full(cut_ref.shape, 2 ** idx_bits, I32)

    @pl.when(jnp.max(jnp.where(ambiguous, 1.0, 0.0)) > 0.5)
    def _():
        thb = jnp.broadcast_to(theta, (rows, LANES))
        lane = lax.broadcasted_iota(I32, (rows, LANES), 1)

        def ties_before(jc):
            jb = jnp.broadcast_to(jc, (rows, LANES))
            return _row_count(keys_ref, n_chunks, ck, rows,
                              lambda blk, base: (blk == thb) & (lane + base < jb))

        def jbit(i, jcur):
            cand = jcur | lax.shift_left(jnp.int32(1), jnp.int32(idx_bits - 1) - i)
            return jnp.where(ties_before(cand) <= need, cand, jcur)
        jfin = lax.fori_loop(0, idx_bits, jbit, jnp.zeros((rows, 1), I32))
        jfin = jnp.where(ambiguous, jfin, 2 ** idx_bits)
        cut_ref[...] = jnp.broadcast_to(jfin, cut_ref.shape)
    return theta


def _attn_proj_kernel(x_ref, wq_ref, wkv_ref, wqi_ref, wkw_ref,
                      q_ref, k_ref, v_ref, kb_ref, vb_ref, qi_ref, ki_ref, kib_ref, wi_ref):
    xb = x_ref[...].astype(BF16)
    q = jnp.dot(xb, wq_ref[...], preferred_element_type=F32)
    q_ref[...] = (q * (ATT_HEAD_DIM ** -0.5)).astype(BF16)
    kv = jnp.dot(xb, wkv_ref[...], preferred_element_type=F32)
    k = kv[:, :ATT_KV_DIM]
    v = kv[:, ATT_KV_DIM:]
    k_ref[...] = k
    v_ref[...] = v
    kb_ref[...] = k.astype(BF16)
    vb_ref[...] = v.astype(BF16)
    qi = jnp.dot(xb, wqi_ref[...], preferred_element_type=F32)
    qi_ref[...] = (qi * (IDX_DIM ** -0.5)).astype(BF16)
    kw = jnp.dot(xb, wkw_ref[...], preferred_element_type=F32)
    ki = kw[:, :IDX_DIM]
    ki_ref[...] = ki
    kib_ref[...] = ki.astype(BF16)
    wi_ref[...] = kw[:, IDX_DIM:IDX_DIM + IDX_HEADS] * (IDX_HEADS ** -0.5)


def _attn_weights(w_in):
    s0 = ATT_Q_DIM
    s2 = s0 + 2 * ATT_KV_DIM
    s3 = s2 + IDX_HEADS * IDX_DIM
    wkw = jnp.pad(w_in[:, s3:], ((0, 0), (0, LANES - (w_in.shape[1] - s3))))
    return (w_in[:, :s0].astype(BF16), w_in[:, s0:s2].astype(BF16), w_in[:, s2:s3].astype(BF16),
            wkw.astype(BF16))


def _attn_proj(x2d, w_in):
    m = x2d.shape[0]
    tm = min(m, 512)
    wq, wkv, wqi, wkw = _attn_weights(w_in)
    row = lambda n: pl.BlockSpec((tm, n), lambda i: (i, 0))
    full = lambda a: pl.BlockSpec(a.shape, lambda i: (0, 0))
    outs = [(ATT_Q_DIM, BF16), (ATT_KV_DIM, F32), (ATT_KV_DIM, F32), (ATT_KV_DIM, BF16),
            (ATT_KV_DIM, BF16), (IDX_HEADS * IDX_DIM, BF16), (IDX_DIM, F32), (IDX_DIM, BF16),
            (IDX_HEADS, F32)]
    return pl.pallas_call(
        _attn_proj_kernel,
        grid=(m // tm,),
        in_specs=[row(D_MODEL), full(wq), full(wkv), full(wqi), full(wkw)],
        out_specs=[row(n) for n, _ in outs],
        out_shape=[jax.ShapeDtypeStruct((m, n), dt) for n, dt in outs],
        compiler_params=_cparams(("parallel",)),
        name="attn_proj",
    )(x2d, wq, wkv, wqi, wkw)


def _attn_proj_t_kernel(x_ref, wq_ref, wkv_ref, wqi_ref, wkw_ref,
                        k_ref, v_ref, ki_ref, kb_ref, kib_ref, qt_ref, qit_ref, wit_ref, vbt_ref):
    xb = x_ref[...].astype(BF16)
    q = jnp.dot(xb, wq_ref[...], preferred_element_type=F32)
    qt_ref[...] = (q * (ATT_HEAD_DIM ** -0.5 * LOG2_E)).T.astype(BF16)
    kv = jnp.dot(xb, wkv_ref[...], preferred_element_type=F32)
    k = kv[:, :ATT_KV_DIM]
    v = kv[:, ATT_KV_DIM:]
    k_ref[...] = k
    v_ref[...] = v
    kb_ref[...] = k.astype(BF16)
    vbt_ref[...] = v.T.astype(BF16)
    qi = jnp.dot(xb, wqi_ref[...], preferred_element_type=F32)
    qit_ref[...] = (qi * (IDX_DIM ** -0.5)).T.astype(BF16)
    kw = jnp.dot(xb, wkw_ref[...], preferred_element_type=F32)
    ki = kw[:, :IDX_DIM]
    ki_ref[...] = ki
    kib_ref[...] = ki.astype(BF16)
    wit_ref[...] = kw.T[IDX_DIM:IDX_DIM + IDX_HEADS, :] * (IDX_HEADS ** -0.5)


def _attn_proj_t(x2d, w_in):
    m = x2d.shape[0]
    tm = min(m, 512)
    wq, wkv, wqi, wkw = _attn_weights(w_in)
    row = lambda n: pl.BlockSpec((tm, n), lambda i: (i, 0))
    col = lambda n: pl.BlockSpec((n, tm), lambda i: (0, i))
    full = lambda a: pl.BlockSpec(a.shape, lambda i: (0, 0))
    nat = [(ATT_KV_DIM, F32), (ATT_KV_DIM, F32), (IDX_DIM, F32), (ATT_KV_DIM, BF16), (IDX_DIM, BF16)]
    tr = [(ATT_Q_DIM, BF16), (IDX_HEADS * IDX_DIM, BF16), (IDX_HEADS, F32), (ATT_KV_DIM, BF16)]
    return pl.pallas_call(
        _attn_proj_t_kernel,
        grid=(m // tm,),
        in_specs=[row(D_MODEL), full(wq), full(wkv), full(wqi), full(wkw)],
        out_specs=[row(n) for n, _ in nat] + [col(n) for n, _ in tr],
        out_shape=[jax.ShapeDtypeStruct((m, n), dt) for n, dt in nat] +
                  [jax.ShapeDtypeStruct((n, m), dt) for n, dt in tr],
        compiler_params=_cparams(("parallel",)),
        name="attn_proj_t",
    )(x2d, wq, wkv, wqi, wkw)


def _col_count(keys_ref, n_chunks, ck, pred):
    width = keys_ref.shape[1]
    par = math.gcd(ck, 64)

    def body(c, acc):
        base = pl.multiple_of(c * ck, ck)
        hit = pred(keys_ref[pl.ds(base, ck), :], base)
        return acc + jnp.sum(jnp.where(hit, 1.0, 0.0).reshape(ck // par, par, width), axis=0)
    acc = lax.fori_loop(0, n_chunks, body, jnp.zeros((par, width), F32))
    return jnp.sum(acc, axis=0, keepdims=True)


def _select_threshold_cols(keys_ref, n_chunks, ck, kth, idx_bits):
    width = keys_ref.shape[1]
    kth_f = float(kth)

    def count_ge(t):
        return _col_count(keys_ref, n_chunks, ck, lambda blk, base: blk >= t)

    theta = jnp.full((1, width), INT_MIN, I32)
    zero = jnp.zeros((1, width), I32)
    theta = jnp.where(count_ge(zero) >= kth_f, zero, theta)

    def bit_body(i, t):
        cand = t | lax.shift_left(jnp.int32(1), jnp.int32(30) - i)
        return jnp.where(count_ge(cand) >= kth_f, cand, t)
    theta = lax.fori_loop(0, 31, bit_body, theta)

    n_gt = count_ge(theta + 1)
    n_ge = count_ge(theta)
    need = kth_f - n_gt
    ambiguous = (n_ge - n_gt > need) & (theta > INT_MIN)
    no_cut = jnp.full((1, width), 2 ** idx_bits, I32)

    def tie_search():
        row = lax.broadcasted_iota(I32, (ck, width), 0)

        def ties_before(jc):
            return _col_count(keys_ref, n_chunks, ck, lambda blk, base: (blk == theta) & (row + base < jc))

        def jbit(i, jcur):
            cand = jcur | lax.shift_left(jnp.int32(1), jnp.int32(idx_bits - 1) - i)
            return jnp.where(ties_before(cand) <= need, cand, jcur)
        jfin = lax.fori_loop(0, idx_bits, jbit, jnp.zeros((1, width), I32))
        return jnp.where(ambiguous, jfin, no_cut)

    cut = lax.cond(jnp.max(jnp.where(ambiguous, 1.0, 0.0)) > 0.5, tie_search, lambda: no_cut)
    return theta, cut


def _dsa_prompt_kernel(qt_ref, qit_ref, wit_ref, kb_ref, vbt_ref, kib_ref, o_ref,
                       keys_ref, m_ref, acc_ref, bias_ref, s_ref, p_ref, *, ck, n_sel, idx_bits):
    i = pl.program_id(1)
    qb = Q_BLOCK
    n_chunks = ((i + 1) * qb + ck - 1) // ck
    q_pos = i * qb + lax.broadcasted_iota(I32, (ck, qb), 1)
    row = lax.broadcasted_iota(I32, (ck, qb), 0)

    qit = qit_ref[...]
    qi_all = jnp.concatenate([qit[h * IDX_DIM:(h + 1) * IDX_DIM, :] for h in range(IDX_HEADS)], axis=1)
    wit = wit_ref[...]

    def score_body(c, carry):
        base = pl.multiple_of(c * ck, ck)
        d = jnp.dot(kib_ref[0, pl.ds(base, ck), :], qi_all, preferred_element_type=F32)
        acc = jnp.zeros((ck, qb), F32)
        for h in range(IDX_HEADS):
            acc = acc + jnp.maximum(d[:, h * qb:(h + 1) * qb], 0.0) * wit[h:h + 1, :]
        keys_ref[pl.ds(base, ck), :] = jnp.where(row + base <= q_pos, _sort_key(acc), INT_MIN)
        return carry
    lax.fori_loop(0, n_chunks, score_body, 0)

    theta, cut = _select_threshold_cols(keys_ref, n_chunks, ck, n_sel, idx_bits)

    qt = qt_ref[...]
    qn = [jnp.concatenate([qt[(n * ATT_GROUP + g) * ATT_HEAD_DIM:(n * ATT_GROUP + g + 1) * ATT_HEAD_DIM, :]
                           for g in range(ATT_GROUP)], axis=1) for n in range(ATT_KV_HEADS)]
    m_ref[...] = jnp.full(m_ref.shape, NEG_INF, F32)
    acc_ref[...] = jnp.zeros(acc_ref.shape, F32)
    hd = ATT_HEAD_DIM
    ones = jnp.ones((acc_ref.shape[1] - hd, ck), BF16)
    par = math.gcd(ck, 64)

    def scores(kbase, n):
        kc = kb_ref[0, pl.ds(kbase, ck), n * hd:(n + 1) * hd]
        s_ref[n % 2] = jnp.dot(kc, qn[n], preferred_element_type=F32)
    scores(0, 0)

    def attn_body(c, carry):
        base = pl.multiple_of(c * ck, ck)
        key = keys_ref[pl.ds(base, ck), :]
        sel = ((key > theta) | ((key == theta) & (row + base < cut))) & (key != INT_MIN)
        bias_ref[...] = jnp.where(sel, 0.0, NEG_INF)

        for n in range(ATT_KV_HEADS):
            if n + 1 < ATT_KV_HEADS:
                scores(base, n + 1)
            else:
                scores(pl.multiple_of(jnp.minimum(c + 1, n_chunks - 1) * ck, ck), 0)
            vtc = jnp.concatenate([vbt_ref[n * hd:(n + 1) * hd, pl.ds(base, ck)], ones], axis=0)
            sn_ref = s_ref.at[n % 2]

            def masked(j):
                rs = slice(j * par, (j + 1) * par)
                return sn_ref[rs, :] + jnp.concatenate([bias_ref[rs, :]] * ATT_GROUP, axis=1)
            mx = masked(0)
            for j in range(1, ck // par):
                mx = jnp.maximum(mx, masked(j))
            m_prev = m_ref[n]
            m_new = jnp.maximum(m_prev, jnp.max(mx, axis=0, keepdims=True))
            m_safe = jnp.where(m_new == NEG_INF, 0.0, m_new)
            alpha = jnp.exp2(m_prev - m_safe)
            for j in range(ck // par):
                p_ref[j * par:(j + 1) * par, :] = jnp.exp2((masked(j) - m_safe).astype(BF16))
            acc_ref[n] = alpha * acc_ref[n] + jnp.dot(vtc, p_ref[...], preferred_element_type=F32)
            m_ref[n] = m_new
        return carry
    lax.fori_loop(0, n_chunks, attn_body, 0)

    ot = jnp.concatenate([acc_ref[n, :hd] / acc_ref[n, hd:hd + 1] for n in range(ATT_KV_HEADS)], axis=0)
    for g in range(ATT_GROUP):
        og = ot[:, g * qb:(g + 1) * qb].T
        for n in range(ATT_KV_HEADS):
            h = n * ATT_GROUP + g
            o_ref[:, h * ATT_HEAD_DIM:(h + 1) * ATT_HEAD_DIM] = (
                og[:, n * ATT_HEAD_DIM:(n + 1) * ATT_HEAD_DIM].astype(o_ref.dtype))


def _dsa_prompt(qt, qit, wit, kb, vbt, kib, b, n_sel):
    t = kb.shape[1]
    ck = min(512, t)
    nqb = t // Q_BLOCK
    idx_bits = max(1, math.ceil(math.log2(t))) + 1
    colblk = lambda n: pl.BlockSpec((n, Q_BLOCK), lambda bi, i: (0, bi * nqb + i))
    res = lambda n: pl.BlockSpec((1, t, n), lambda bi, i: (bi, 0, 0))
    gq = ATT_GROUP * Q_BLOCK
    return pl.pallas_call(
        functools.partial(_dsa_prompt_kernel, ck=ck, n_sel=n_sel, idx_bits=idx_bits),
        grid=(b, nqb),
        in_specs=[colblk(ATT_Q_DIM), colblk(IDX_HEADS * IDX_DIM), colblk(IDX_HEADS),
                  res(ATT_KV_DIM), pl.BlockSpec((ATT_KV_DIM, t), lambda bi, i: (0, bi)), res(IDX_DIM)],
        out_specs=pl.BlockSpec((Q_BLOCK, ATT_Q_DIM), lambda bi, i: (bi * nqb + i, 0)),
        out_shape=jax.ShapeDtypeStruct((b * t, ATT_Q_DIM), BF16),
        scratch_shapes=[pltpu.VMEM((t, Q_BLOCK), I32),
                        pltpu.VMEM((ATT_KV_HEADS, 1, gq), F32),
                        pltpu.VMEM((ATT_KV_HEADS, ATT_HEAD_DIM + BF16_ROWS, gq), F32),
                        pltpu.VMEM((ck, Q_BLOCK), F32), pltpu.VMEM((2, ck, gq), F32),
                        pltpu.VMEM((ck, gq), BF16)],
        compiler_params=_cparams(("parallel", "arbitrary"), 56),
        name="dsa_prompt",
    )(qt, qit, wit, kb, vbt, kib)


def _outproj_ln_kernel(o_ref, h_ref, w_ref, g_ref, b_ref, out_ref):
    y = jnp.dot(o_ref[...], w_ref[...], preferred_element_type=F32)
    out_ref[...] = _layer_norm(DEEPNORM_ALPHA * h_ref[...] + y, g_ref[...], b_ref[...])


def _outproj_ln(o2d, h2d, w_out, g, b):
    m, kdim = o2d.shape
    tm = min(m, 512)
    row = lambda n: pl.BlockSpec((tm, n), lambda i: (i, 0))
    return pl.pallas_call(
        _outproj_ln_kernel,
        grid=(m // tm,),
        in_specs=[row(kdim), row(D_MODEL), pl.BlockSpec((kdim, D_MODEL), lambda i: (0, 0)),
                  pl.BlockSpec((1, D_MODEL), lambda i: (0, 0)), pl.BlockSpec((1, D_MODEL), lambda i: (0, 0))],
        out_specs=row(D_MODEL),
        out_shape=jax.ShapeDtypeStruct((m, D_MODEL), F32),
        compiler_params=_cparams(("parallel",)),
        name="outproj_ln",
    )(o2d, h2d, w_out.astype(BF16), g.reshape(1, -1), b.reshape(1, -1))


def _dsa_sample_kernel(pt_ref, qbd_ref, qi_ref, wi_ref, kn_ref, vn_ref, kin_ref,
                       cki_hbm, ck_hbm, cv_hbm, o_ref,
                       ki_buf, k_buf, v_buf, s_ref, keys_ref, cut_ref, sem,
                       *, n_pages, t_new, n_sel, idx_bits, layer_off, lchunk):
    b = pl.program_id(0)
    past = n_pages * PAGE_SIZE
    ltot = past + LANES
    rows = keys_ref.shape[0]

    def page_copy(hbm, buf, p, s):
        page = pt_ref[b, p] + layer_off
        return pltpu.make_async_copy(hbm.at[page],
                                     buf.at[:, pl.ds(pl.multiple_of(p * PAGE_SIZE, PAGE_SIZE), PAGE_SIZE)],
                                     sem.at[s])

    def start_all(p, c):
        page_copy(cki_hbm, ki_buf, p, 0).start()
        page_copy(ck_hbm, k_buf, p, 1).start()
        page_copy(cv_hbm, v_buf, p, 2).start()
        return c
    lax.fori_loop(0, n_pages, start_all, 0)

    def wait_all(hbm, buf, s):
        def body(p, c):
            page_copy(hbm, buf, p, s).wait()
            return c
        lax.fori_loop(0, n_pages, body, 0)

    qi = qi_ref[0]
    wcol = wi_ref[0]
    wait_all(cki_hbm, ki_buf, 0)

    def idx_scores(kic, dims=NN_DIMS):
        d = lax.dot_general(qi, kic, dims, preferred_element_type=F32)
        d = jnp.maximum(d, 0.0) * wcol
        return jnp.sum(d.reshape(t_new, IDX_HEADS, d.shape[1]), axis=1)

    keys_ref[...] = jnp.full(keys_ref.shape, INT_MIN, I32)
    for c in range(past // lchunk):
        sc = idx_scores(ki_buf[:, c * lchunk:(c + 1) * lchunk].astype(BF16))
        keys_ref[0:t_new, c * lchunk:(c + 1) * lchunk] = _sort_key(sc)
    scn = idx_scores(kin_ref[0], NT_DIMS)
    tpos = lax.broadcasted_iota(I32, (t_new, LANES), 0)
    lpos = lax.broadcasted_iota(I32, (t_new, LANES), 1)
    keys_ref[0:t_new, past:ltot] = jnp.where((lpos <= tpos) & (lpos < t_new), _sort_key(scn), INT_MIN)

    theta = _select_threshold(keys_ref, cut_ref, ltot // LANES, LANES, rows, n_sel, idx_bits)
    key = keys_ref[...]
    lane = lax.broadcasted_iota(I32, (rows, ltot), 1)
    sel = ((key > theta) | ((key == theta) & (lane < cut_ref[:, :1]))) & (key != INT_MIN)
    bias = jnp.where(sel, 0.0, NEG_INF)[0:t_new]
    bias = jnp.broadcast_to(bias[:, None, :], (t_new, ATT_HEADS, ltot)).reshape(t_new * ATT_HEADS, ltot)

    qbd = qbd_ref[0]
    wait_all(ck_hbm, k_buf, 1)
    for c in range(past // lchunk):
        kc = k_buf[:, c * lchunk:(c + 1) * lchunk].astype(BF16)
        s_ref[:, c * lchunk:(c + 1) * lchunk] = jnp.dot(qbd, kc, preferred_element_type=F32)
    s_ref[:, past:ltot] = lax.dot_general(qbd, kn_ref[0], NT_DIMS, preferred_element_type=F32)
    s = s_ref[...] + bias
    m = jnp.max(s, axis=1, keepdims=True)
    p = jnp.exp(s - m)
    linv = 1.0 / jnp.sum(p, axis=1, keepdims=True)
    s_ref[...] = p
    wait_all(cv_hbm, v_buf, 2)
    o = jnp.dot(s_ref[:, past:ltot].astype(BF16), vn_ref[0], preferred_element_type=F32)
    for c in range(past // lchunk):
        vc = v_buf[:, c * lchunk:(c + 1) * lchunk].astype(BF16)
        o = o + lax.dot_general(s_ref[:, c * lchunk:(c + 1) * lchunk].astype(BF16), vc, NT_DIMS,
                                preferred_element_type=F32)
    o = o * linv
    nrow = (lax.broadcasted_iota(I32, (t_new * ATT_HEADS, ATT_HEAD_DIM), 0) % ATT_HEADS) // ATT_GROUP
    out = jnp.zeros((t_new * ATT_HEADS, ATT_HEAD_DIM), F32)
    for n in range(ATT_KV_HEADS):
        out = out + jnp.where(nrow == n, o[:, n * ATT_HEAD_DIM:(n + 1) * ATT_HEAD_DIM], 0.0)
    o_ref[0] = out.astype(o_ref.dtype)


def _dsa_sample(page_table, qs, qis, wi, kb, vb, kib, cache_ki, cache_k, cache_v, layer_off, n_sel):
    b, t, _ = qs.shape
    n_pages = page_table.shape[1]
    past = n_pages * PAGE_SIZE
    ltot = past + LANES
    rows = 8
    q5 = qs.reshape(b, t, ATT_KV_HEADS, ATT_GROUP, 1, ATT_HEAD_DIM)
    eye = jnp.eye(ATT_KV_HEADS, dtype=qs.dtype).reshape(1, 1, ATT_KV_HEADS, 1, ATT_KV_HEADS, 1)
    qbd = (q5 * eye).reshape(b, t * ATT_HEADS, ATT_KV_DIM)
    qi = qis.reshape(b, t * IDX_HEADS, IDX_DIM)
    wcol = wi.reshape(b, t * IDX_HEADS, 1)
    padr = lambda a: jnp.pad(a, ((0, 0), (0, LANES - t), (0, 0)))
    kn, vn, kin = padr(kb), padr(vb), padr(kib)
    idx_bits = max(1, math.ceil(math.log2(ltot))) + 1
    lchunk = min(1024, past)
    bl = lambda a: pl.BlockSpec((1,) + a.shape[1:], lambda i, pt: (i, 0, 0))
    anyspec = pl.BlockSpec(memory_space=pl.ANY)
    grid_spec = pltpu.PrefetchScalarGridSpec(
        num_scalar_prefetch=1,
        grid=(b,),
        in_specs=[bl(qbd), bl(qi), bl(wcol), bl(kn), bl(vn), bl(kin), anyspec, anyspec, anyspec],
        out_specs=pl.BlockSpec((1, t * ATT_HEADS, ATT_HEAD_DIM), lambda i, pt: (i, 0, 0)),
        scratch_shapes=[pltpu.VMEM((IDX_DIM, past), F32), pltpu.VMEM((ATT_KV_DIM, past), F32),
                        pltpu.VMEM((ATT_KV_DIM, past), F32), pltpu.VMEM((t * ATT_HEADS, ltot), F32),
                        pltpu.VMEM((rows, ltot), I32), pltpu.VMEM((rows, LANES), I32),
                        pltpu.SemaphoreType.DMA((3,))],
    )
    o = pl.pallas_call(
        functools.partial(_dsa_sample_kernel, n_pages=n_pages, t_new=t, n_sel=n_sel, idx_bits=idx_bits,
                          layer_off=layer_off, lchunk=lchunk),
        grid_spec=grid_spec,
        out_shape=jax.ShapeDtypeStruct((b, t * ATT_HEADS, ATT_HEAD_DIM), BF16),
        compiler_params=_cparams(("arbitrary",), 56),
        name="dsa_sample",
    )(page_table, qbd, qi, wcol, kn, vn, kin, cache_ki, cache_k, cache_v)
    return o.reshape(b, t, ATT_Q_DIM)


def _gla_proj_kernel(x_ref, wq_ref, wk_ref, wv_ref, wg_ref, wr_ref, wup_ref, bg_ref,
                     q_ref, k_ref, v_ref, la_ref, r_ref):
    xb = x_ref[...].astype(BF16)
    q_ref[...] = jnp.dot(xb, wq_ref[...], preferred_element_type=F32) * (GLA_DK ** -0.5)
    k_ref[...] = jnp.dot(xb, wk_ref[...], preferred_element_type=F32)
    v_ref[...] = jnp.dot(xb, wv_ref[...], preferred_element_type=F32)
    r_ref[...] = jnp.dot(xb, wr_ref[...], preferred_element_type=F32)
    gd = jnp.dot(xb, wg_ref[...], preferred_element_type=F32)
    glogit = jnp.dot(gd.astype(BF16), wup_ref[...], preferred_element_type=F32) + bg_ref[...]
    la_ref[...] = jax.nn.log_sigmoid(glogit) / GLA_TAU


def _gla_proj(x2d, w_in, w_gate_up, b_gate):
    m = x2d.shape[0]
    tm = min(m, 512)
    dq = GLA_HEADS * GLA_DK
    dv = GLA_HEADS * GLA_DV
    wq = w_in[:, :dq].astype(BF16)
    wk = w_in[:, dq:2 * dq].astype(BF16)
    wv = w_in[:, 2 * dq:2 * dq + dv].astype(BF16)
    wg = jnp.pad(w_in[:, 2 * dq + dv:2 * dq + dv + GLA_GATE_RANK], ((0, 0), (0, LANES - GLA_GATE_RANK))).astype(BF16)
    wr = w_in[:, 2 * dq + dv + GLA_GATE_RANK:].astype(BF16)
    wup = jnp.pad(w_gate_up, ((0, LANES - GLA_GATE_RANK), (0, 0))).astype(BF16)
    row = lambda n: pl.BlockSpec((tm, n), lambda i: (i, 0))
    full = lambda a: pl.BlockSpec(a.shape, lambda i: (0, 0))
    bg = b_gate.reshape(1, -1)
    outs = [dq, dq, dv, dq, dv]
    return pl.pallas_call(
        _gla_proj_kernel,
        grid=(m // tm,),
        in_specs=[row(D_MODEL), full(wq), full(wk), full(wv), full(wg), full(wr), full(wup), full(bg)],
        out_specs=[row(n) for n in outs],
        out_shape=[jax.ShapeDtypeStruct((m, n), F32) for n in outs],
        compiler_params=_cparams(("parallel",)),
        name="gla_proj",
    )(x2d, wq, wk, wv, wg, wr, wup, bg)


def _gla_chunk(q, k, v, g, st, c):
    sub = GLA_SUB
    n_sub = c // sub
    ri = lax.broadcasted_iota(I32, (c, c), 0)
    ci = lax.broadcasted_iota(I32, (c, c), 1)
    tri = jnp.where(ci <= ri, 1.0, 0.0).astype(F32)
    bc = jnp.dot(tri, g, preferred_element_type=F32, precision=lax.Precision.HIGHEST)
    o = lax.dot_general((q * jnp.exp(bc)).astype(BF16), st.astype(BF16), NT_DIMS, preferred_element_type=F32)
    if n_sub > 1:
        mrow = jnp.concatenate([jnp.zeros((sub, GLA_DK), F32)] +
                               [jnp.broadcast_to(bc[i * sub - 1:i * sub], (sub, GLA_DK)) for i in range(1, n_sub)], axis=0)
        qh = (q * jnp.exp(bc - mrow)).astype(BF16)
        parts = [jnp.zeros((sub, GLA_DV), F32)]
        for i in range(1, n_sub):
            kh = (k[:i * sub] * jnp.exp(bc[i * sub - 1:i * sub] - bc[:i * sub])).astype(BF16)
            a = lax.dot_general(qh[i * sub:(i + 1) * sub], kh, NT_DIMS, preferred_element_type=F32)
            parts.append(jnp.dot(a.astype(BF16), v[:i * sub].astype(BF16), preferred_element_type=F32))
        o = o + jnp.concatenate(parts, axis=0)
    rsub = lax.broadcasted_iota(I32, (c, 1), 0) % sub
    for dlt in range(sub):
        ks = k if dlt == 0 else pltpu.roll(k, dlt, 0)
        bs = bc if dlt == 0 else pltpu.roll(bc, dlt, 0)
        vs = v if dlt == 0 else pltpu.roll(v, dlt, 0)
        w = jnp.sum(q * ks * jnp.exp(jnp.minimum(bc - bs, 0.0)), axis=1, keepdims=True)
        o = o + jnp.where(rsub >= dlt, w, 0.0) * vs
    bl = bc[c - 1:c]
    kt = (k * jnp.exp(bl - bc)).astype(BF16)
    st_new = st * jnp.exp(bl) + lax.dot_general(v.astype(BF16), kt, TN_DIMS, preferred_element_type=F32)
    return o, st_new


def _gla_kernel(q_ref, k_ref, v_ref, g_ref, r_ref, ng_ref, s0_ref, o_ref, sf_ref, st_ref, *, c, n_c):
    tb = pl.program_id(2)

    @pl.when(tb == 0)
    def _():
        st_ref[...] = s0_ref[0, 0].T

    st = st_ref[...]
    for ci in range(n_c):
        sl = slice(ci * c, (ci + 1) * c)
        o, st = _gla_chunk(q_ref[0, sl, :], k_ref[0, sl, :], v_ref[0, sl, :], g_ref[0, sl, :], st, c)
        o = o * lax.rsqrt(jnp.mean(o * o, axis=-1, keepdims=True) + LN_EPS) * ng_ref[...]
        o_ref[0, sl, :] = (o * jax.nn.silu(r_ref[0, sl, :])).astype(o_ref.dtype)
    st_ref[...] = st

    @pl.when(tb == pl.num_programs(2) - 1)
    def _():
        sf_ref[0, 0] = st.T


def _gla(q, k, v, la, r, norm_g, s0):
    b, t, _ = q.shape
    c = math.gcd(t, GLA_CHUNK)
    tblk = math.gcd(t, 4 * GLA_CHUNK)
    n_c = tblk // c
    kspec = pl.BlockSpec((1, tblk, GLA_DK), lambda bi, h, i: (bi, i, h))
    vspec = pl.BlockSpec((1, tblk, GLA_DV), lambda bi, h, i: (bi, i, h))
    sspec = pl.BlockSpec((1, 1, GLA_DK, GLA_DV), lambda bi, h, i: (bi, h, 0, 0))
    return pl.pallas_call(
        functools.partial(_gla_kernel, c=c, n_c=n_c),
        grid=(b, GLA_HEADS, t // tblk),
        in_specs=[kspec, kspec, vspec, kspec, vspec, pl.BlockSpec((1, GLA_DV), lambda bi, h, i: (0, 0)), sspec],
        out_specs=[vspec, sspec],
        out_shape=[jax.ShapeDtypeStruct((b, t, GLA_HEADS * GLA_DV), BF16),
                   jax.ShapeDtypeStruct((b, GLA_HEADS, GLA_DK, GLA_DV), F32)],
        scratch_shapes=[pltpu.VMEM((GLA_DV, GLA_DK), F32)],
        compiler_params=_cparams(("parallel", "parallel", "arbitrary")),
        name="gla_scan",
    )(q, k, v, la, r, norm_g.reshape(1, -1), s0)


def _gla_mixer(x, s0, w_in, w_gate_up, b_gate, norm_g):
    b, t, _ = x.shape
    q, k, v, la, r = _gla_proj(x.reshape(b * t, D_MODEL), w_in, w_gate_up, b_gate)
    tp = -(-t // GLA_SUB) * GLA_SUB
    r3 = lambda a: jnp.pad(a.reshape(b, t, -1), ((0, 0), (0, tp - t), (0, 0)))
    o, sf = _gla(r3(q), r3(k), r3(v), r3(la), r3(r), norm_g, s0)
    return o[:, :t], sf


def _top_values(s, k):
    vals = []
    cur = s
    for _ in range(k):
        mx = jnp.max(cur, axis=0, keepdims=True)
        vals.append(mx)
        cur = jnp.where(cur == mx, NEG_INF, cur)
    return vals


def _top_values_ranked(s, k):
    vals = []
    cur = s
    rank = jnp.full(s.shape, float(k), F32)
    for i in range(k):
        mx = jnp.max(cur, axis=0, keepdims=True)
        vals.append(mx)
        top = cur == mx
        rank = jnp.where(top, float(i), rank)
        cur = jnp.where(top, NEG_INF, cur)
    return vals, rank


def _peer_select_kernel(x_ref, wq_ref, k1_ref, k2_ref, c1_ref, p_ref, r2_ref, e_ref):
    xb = x_ref[...].astype(BF16)
    q = jnp.dot(xb, wq_ref[...], preferred_element_type=F32).astype(BF16)
    tt = q.shape[0]
    k1 = k1_ref[...]
    k2 = k2_ref[...]
    sub8 = lax.broadcasted_iota(I32, (8, tt), 0)
    for h in range(PEER_HEADS):
        s1 = lax.dot_general(k1, q[:, h * PEER_D_KEY:h * PEER_D_KEY + PEER_HALF], NT_DIMS,
                             preferred_element_type=F32)
        s2 = lax.dot_general(k2, q[:, h * PEER_D_KEY + PEER_HALF:(h + 1) * PEER_D_KEY], NT_DIMS,
                             preferred_element_type=F32)
        t1, rank1 = _top_values_ranked(s1, PEER_TOPK)
        t2, rank2 = _top_values_ranked(s2, PEER_TOPK)
        t2a = jnp.concatenate(t2, axis=0)
        cands = [t1[0] + t2a]
        for a in range(1, PEER_TOPK):
            nb = PEER_TOPK // (a + 1)
            cands.append(jnp.where(sub8 < nb, t1[a] + t2a[:8], NEG_INF))
        best = _top_values(jnp.concatenate(cands, axis=0), PEER_TOPK)
        theta = best[PEER_TOPK - 1]
        z = best[0] * 0.0
        for bv in best:
            z = z + jnp.exp(bv - best[0])
        count1 = jnp.zeros(s1.shape, F32)
        for a in range(PEER_TOPK):
            cnt_a = jnp.sum(jnp.where(t1[a] + t2a >= theta, 1.0, 0.0), axis=0, keepdims=True)
            count1 = jnp.where(rank1 == float(a), cnt_a, count1)
        gate1 = jnp.exp(s1 - t1[0]) * (0.5 / z)
        gate2 = jnp.exp(s2 - t2[0])
        for lg in range(tt // LANES):
            ls = slice(lg * LANES, (lg + 1) * LANES)
            c1_ref[lg, h] = count1[:, ls]
            p_ref[lg, h] = gate1[:, ls]
            r2_ref[lg, h] = rank2[:, ls]
            e_ref[lg, h] = gate2[:, ls]


def _peer_select(x2d, w_q, keys1, keys2):
    n = x2d.shape[0]
    tt = min(n, 256)
    wq = w_q.astype(BF16)
    k1 = keys1.astype(BF16)
    k2 = keys2.astype(BF16)
    full = lambda a: pl.BlockSpec(a.shape, lambda i: (0,) * a.ndim)
    hspec = pl.BlockSpec((tt // LANES, PEER_HEADS, PEER_N_KEYS, LANES), lambda i: (i, 0, 0, 0))
    big = lambda dt: jax.ShapeDtypeStruct((n // LANES, PEER_HEADS, PEER_N_KEYS, LANES), dt)
    return pl.pallas_call(
        _peer_select_kernel,
        grid=(n // tt,),
        in_specs=[pl.BlockSpec((tt, D_MODEL), lambda i: (i, 0)), full(wq), full(k1), full(k2)],
        out_specs=[hspec, hspec, hspec, hspec],
        out_shape=[big(F32), big(F32), big(F32), big(F32)],
        compiler_params=_cparams(("parallel",)),
        name="peer_select",
    )(x2d, wq, k1, k2)


def _peer_mix_kernel(h_ref, c1_ref, p_ref, r2_ref, e_ref, u_ref, vt_ref, g_ref, bb_ref, out_ref,
                     xb_ref, ht0_ref, ht1_ref, wt0_ref, wt1_ref, yt_ref, *, n_i1):
    eb = pl.program_id(1)
    n_blk = pl.num_programs(1) - 2
    tt = h_ref.shape[0]

    @pl.when(eb == 0)
    def _():
        xb_ref[...] = h_ref[...].T.astype(BF16)
        yt_ref[...] = jnp.zeros(yt_ref.shape, F32)
        ht1_ref[...] = jnp.zeros(ht1_ref.shape, F32)
        wt0_ref[...] = jnp.zeros(wt0_ref.shape, BF16)

    i1_base = pl.multiple_of(jnp.clip(eb - 1, 0, n_blk - 1) * n_i1, n_i1)

    def gate_rows(cb, pb, ks, lg):
        terms = [jnp.where(r2_ref[lg, h, ks, :] < cb[h], pb[h] * e_ref[lg, h, ks, :], 0.0)
                 for h in range(PEER_HEADS)]
        while len(terms) > 1:
            terms = [terms[i] + terms[i + 1] for i in range(0, len(terms), 2)]
        return terms[0]

    def stages(q):
        ht_w, ht_r = (ht0_ref, ht1_ref) if q == 0 else (ht1_ref, ht0_ref)
        wt_r, wt_w = (wt0_ref, wt1_ref) if q == 0 else (wt1_ref, wt0_ref)
        n_lg = tt // LANES
        eblk = ht_w.shape[0]

        def matmul_piece(k):
            if k % 2 == 0:
                rows = slice((k // 2) * (eblk // n_lg), (k // 2 + 1) * (eblk // n_lg))
                ht_w[rows, :] = jnp.dot(u_ref[rows, :], xb_ref[...], preferred_element_type=F32)
            else:
                rows = slice((k // 2) * (D_MODEL // n_lg), (k // 2 + 1) * (D_MODEL // n_lg))
                yt_ref[rows, :] += jnp.dot(vt_ref[rows, :], wt_r[...], preferred_element_type=F32)

        for lg in range(n_lg):
            ls = slice(lg * LANES, (lg + 1) * LANES)
            c8 = [c1_ref[lg, h, pl.ds(i1_base, n_i1), :] for h in range(PEER_HEADS)]
            p8 = [p_ref[lg, h, pl.ds(i1_base, n_i1), :] for h in range(PEER_HEADS)]
            for j in range(n_i1):
                if j % (n_i1 // 2) == 0:
                    matmul_piece(2 * lg + j // (n_i1 // 2))
                cb = [jnp.broadcast_to(c8[h][j:j + 1], (8, LANES)) for h in range(PEER_HEADS)]
                pb = [jnp.broadcast_to(p8[h][j:j + 1], (8, LANES)) for h in range(PEER_HEADS)]
                for r in range(PEER_N_KEYS // BF16_ROWS):
                    k0 = r * BF16_ROWS
                    acc = jnp.concatenate([gate_rows(cb, pb, slice(k0, k0 + 8), lg),
                                           gate_rows(cb, pb, slice(k0 + 8, k0 + 16), lg)], axis=0)
                    rs = slice(j * PEER_N_KEYS + k0, j * PEER_N_KEYS + k0 + BF16_ROWS)
                    x = ht_r[rs, ls]
                    gelu2 = x * (1.0 + lax.erf(x * (2.0 ** -0.5)))
                    wt_w[rs, ls] = (gelu2 * acc).astype(BF16)

    pl.when(eb % 2 == 0)(functools.partial(stages, 0))
    pl.when(eb % 2 == 1)(functools.partial(stages, 1))

    @pl.when(eb == pl.num_programs(1) - 1)
    def _():
        y = yt_ref[...].T
        out_ref[...] = _layer_norm(DEEPNORM_ALPHA * h_ref[...] + y, g_ref[...], bb_ref[...])


def _peer_ln(h2d, w_q, keys1, keys2, u, v, g, b):
    n_real = h2d.shape[0]
    if n_real % LANES:
        h2d = jnp.pad(h2d, ((0, LANES - n_real % LANES), (0, 0)))
        return _peer_ln(h2d, w_q, keys1, keys2, u, v, g, b)[:n_real]
    n = n_real
    c1, p, r2, e = _peer_select(h2d, w_q, keys1, keys2)
    tt = min(n, 512)
    eblk = 1024
    n_e = u.shape[0]
    ub = u.astype(BF16)
    vt = v.T.astype(BF16)
    hspec = pl.BlockSpec((tt // LANES, PEER_HEADS, PEER_N_KEYS, LANES), lambda i, j: (i, 0, 0, 0))
    vec = pl.BlockSpec((1, D_MODEL), lambda i, j: (0, 0))
    n_blk = n_e // eblk
    return pl.pallas_call(
        functools.partial(_peer_mix_kernel, n_i1=eblk // PEER_N_KEYS),
        grid=(n // tt, n_blk + 2),
        in_specs=[pl.BlockSpec((tt, D_MODEL), lambda i, j: (i, 0)), hspec, hspec, hspec, hspec,
                  pl.BlockSpec((eblk, D_MODEL), lambda i, j: (jnp.minimum(j, n_blk - 1), 0)),
                  pl.BlockSpec((D_MODEL, eblk), lambda i, j: (0, jnp.clip(j - 2, 0, n_blk - 1))), vec, vec],
        out_specs=pl.BlockSpec((tt, D_MODEL), lambda i, j: (i, 0)),
        out_shape=jax.ShapeDtypeStruct((n, D_MODEL), F32),
        scratch_shapes=[pltpu.VMEM((D_MODEL, tt), BF16), pltpu.VMEM((eblk, tt), F32), pltpu.VMEM((eblk, tt), F32),
                        pltpu.VMEM((eblk, tt), BF16), pltpu.VMEM((eblk, tt), BF16),
                        pltpu.VMEM((D_MODEL, tt), F32)],
        compiler_params=_cparams(("parallel", "arbitrary"), 56),
        name="peer_mix",
    )(h2d, c1, p, r2, e, ub, vt, g.reshape(1, -1), b.reshape(1, -1))


def kernel(x_prompt, x_sample, cache_k, cache_v, cache_kidx, state_gla, page_table,
           attn_w_in, attn_w_out, gla_w_in, gla_w_gate_up, gla_b_gate, gla_norm_g, gla_w_out,
           peer_w_q, peer_keys1, peer_keys2, peer_u, peer_v, ln1_g, ln1_b, ln2_g, ln2_b):
    bp, tp, _ = x_prompt.shape
    bs, ts, _ = x_sample.shape
    n_pool = cache_k.shape[1]
    past = page_table.shape[1] * PAGE_SIZE
    ck = jnp.swapaxes(cache_k.reshape(-1, PAGE_SIZE, ATT_KV_DIM), 1, 2)
    cv = jnp.swapaxes(cache_v.reshape(-1, PAGE_SIZE, ATT_KV_DIM), 1, 2)
    cki = jnp.swapaxes(cache_kidx.reshape(-1, PAGE_SIZE, IDX_DIM), 1, 2)
    hp = x_prompt.reshape(bp * tp, D_MODEL)
    hs = x_sample.reshape(bs * ts, D_MODEL)
    kp_l, vp_l, kip_l, sp_l = [], [], [], []
    ks_l, vs_l, kis_l, ss_l = [], [], [], []
    for i in range(DEPTH):
        j = i // 2
        if i % 2 == 0:
            k, v, ki, kb, kib, qt, qit, wit, vbt = _attn_proj_t(hp, attn_w_in[j])
            r3 = lambda a: a.reshape(bp, tp, -1)
            op = _dsa_prompt(qt, qit, wit, r3(kb), vbt, r3(kib), bp, min(TOPK_MAX, tp // 4))
            kp_l.append(k.reshape(bp, tp, ATT_KV_HEADS, ATT_HEAD_DIM))
            vp_l.append(v.reshape(bp, tp, ATT_KV_HEADS, ATT_HEAD_DIM))
            kip_l.append(ki.reshape(bp, tp, IDX_DIM))
            qs, k, v, kb, vb, qis, ki, kib, wi = _attn_proj(hs, attn_w_in[j])
            r3 = lambda a: a.reshape(bs, ts, -1)
            os_ = _dsa_sample(page_table, r3(qs), r3(qis), r3(wi), r3(kb), r3(vb), r3(kib), cki, ck, cv,
                              j * n_pool, min(TOPK_MAX, (past + ts) // 4))
            ks_l.append(k.reshape(bs, ts, ATT_KV_HEADS, ATT_HEAD_DIM))
            vs_l.append(v.reshape(bs, ts, ATT_KV_HEADS, ATT_HEAD_DIM))
            kis_l.append(ki.reshape(bs, ts, IDX_DIM))
            w_out = attn_w_out[j]
        else:
            s0 = jnp.zeros((bp, GLA_HEADS, GLA_DK, GLA_DV), F32)
            op, sp = _gla_mixer(hp.reshape(bp, tp, D_MODEL), s0, gla_w_in[j], gla_w_gate_up[j], gla_b_gate[j],
                                gla_norm_g[j])
            os_, ss = _gla_mixer(hs.reshape(bs, ts, D_MODEL), state_gla[j], gla_w_in[j], gla_w_gate_up[j],
                                 gla_b_gate[j], gla_norm_g[j])
            sp_l.append(sp)
            ss_l.append(ss)
            w_out = gla_w_out[j]
        hp = _outproj_ln(op.reshape(bp * tp, -1), hp, w_out, ln1_g[i], ln1_b[i])
        hs = _outproj_ln(os_.reshape(bs * ts, -1), hs, w_out, ln1_g[i], ln1_b[i])
        pw = (peer_w_q[i], peer_keys1[i], peer_keys2[i], peer_u[i], peer_v[i], ln2_g[i], ln2_b[i])
        hp = _peer_ln(hp, *pw)
        hs = _peer_ln(hs, *pw)
    return (hp.reshape(bp, tp, D_MODEL), hs.reshape(bs, ts, D_MODEL),
            jnp.stack(kp_l), jnp.stack(vp_l), jnp.stack(kip_l), jnp.stack(sp_l),
            jnp.stack(ks_l), jnp.stack(vs_l), jnp.stack(kis_l), jnp.stack(ss_l))
```

```python
import functools
import math

import jax
import jax.numpy as jnp
from jax import lax
from jax.experimental import pallas as pl
from jax.experimental.pallas import tpu as pltpu

F32 = jnp.float32
BF16 = jnp.bfloat16
I32 = jnp.int32

D_MODEL = 1024
DEPTH = 2
PAGE_SIZE = 128
DEEPNORM_ALPHA = (2.0 * DEPTH) ** 0.25
LN_EPS = 1e-5

ATT_HEADS = 16
ATT_KV_HEADS = 4
ATT_HEAD_DIM = 64
ATT_GROUP = ATT_HEADS // ATT_KV_HEADS
IDX_HEADS = 8
IDX_DIM = 64
TOPK_MAX = 256
Q_BLOCK = 128
ATT_Q_DIM = ATT_HEADS * ATT_HEAD_DIM
ATT_KV_DIM = ATT_KV_HEADS * ATT_HEAD_DIM

GLA_HEADS = 4
GLA_DK = D_MODEL // 2 // GLA_HEADS
GLA_DV = D_MODEL // GLA_HEADS
GLA_GATE_RANK = 16
GLA_TAU = 16.0
GLA_CHUNK = 64
GLA_SUB = 16

PEER_HEADS = 8
PEER_N_KEYS = 128
PEER_D_KEY = 256
PEER_HALF = PEER_D_KEY // 2
PEER_TOPK = 16

LANES = 128
BF16_ROWS = 16
LOG2_E = 1.4426950408889634
INT_MIN = -(2 ** 31)
NEG_INF = float("-inf")

NN_DIMS = (((1,), (0,)), ((), ()))
NT_DIMS = (((1,), (1,)), ((), ()))
TN_DIMS = (((0,), (0,)), ((), ()))


def _cparams(sem, vmem_mib=None, flags=None):
    kw = dict(dimension_semantics=sem)
    if vmem_mib is not None:
        kw["vmem_limit_bytes"] = vmem_mib * 1024 * 1024
    if flags:
        kw["flags"] = flags
    return pltpu.CompilerParams(**kw)


def _sort_key(x):
    b = pltpu.bitcast(x + 0.0, I32)
    return b ^ ((b >> 31) & 0x7FFFFFFF)


def _layer_norm(z, g, b):
    mu = jnp.mean(z, axis=-1, keepdims=True)
    zc = z - mu
    var = jnp.mean(zc * zc, axis=-1, keepdims=True)
    return zc * lax.rsqrt(var + LN_EPS) * g + b


def _row_count(keys_ref, n_chunks, ck, rows, pred):
    def body(c, acc):
        base = pl.multiple_of(c * ck, ck)
        blk = keys_ref[:, pl.ds(base, ck)]
        for j in range(ck // LANES):
            hit = pred(blk[:, j * LANES:(j + 1) * LANES], base + j * LANES)
            acc = acc + jnp.where(hit, 1.0, 0.0)
        return acc
    acc = lax.fori_loop(0, n_chunks, body, jnp.zeros((rows, LANES), F32))
    return jnp.sum(acc, axis=1, keepdims=True)


def _select_threshold(keys_ref, cut_ref, n_chunks, ck, rows, kth, idx_bits):
    kth_f = float(kth)

    def count_ge(t):
        tb = jnp.broadcast_to(t, (rows, LANES))
        return _row_count(keys_ref, n_chunks, ck, rows, lambda blk, base: blk >= tb)

    theta = jnp.full((rows, 1), INT_MIN, I32)
    zero = jnp.zeros((rows, 1), I32)
    theta = jnp.where(count_ge(zero) >= kth_f, zero, theta)

    def bit_body(i, t):
        cand = t | lax.shift_left(jnp.int32(1), jnp.int32(30) - i)
        return jnp.where(count_ge(cand) >= kth_f, cand, t)
    theta = lax.fori_loop(0, 31, bit_body, theta)

    n_gt = count_ge(theta + 1)
    n_ge = count_ge(theta)
    need = kth_f - n_gt
    ambiguous = (n_ge - n_gt > need) & (theta > INT_MIN)
    cut_ref[...] = jnp.full(cut_ref.shape, 2 ** idx_bits, I32)

    @pl.when(jnp.max(jnp.where(ambiguous, 1.0, 0.0)) > 0.5)
    def _():
        thb = jnp.broadcast_to(theta, (rows, LANES))
        lane = lax.broadcasted_iota(I32, (rows, LANES), 1)

        def ties_before(jc):
            jb = jnp.broadcast_to(jc, (rows, LANES))
            return _row_count(keys_ref, n_chunks, ck, rows,
                              lambda blk, base: (blk == thb) & (lane + base < jb))

        def jbit(i, jcur):
            cand = jcur | lax.shift_left(jnp.int32(1), jnp.int32(idx_bits - 1) - i)
            return jnp.where(ties_before(cand) <= need, cand, jcur)
        jfin = lax.fori_loop(0, idx_bits, jbit, jnp.zeros((rows, 1), I32))
        jfin = jnp.where(ambiguous, jfin, 2 ** idx_bits)
        cut_ref[...] = jnp.broadcast_to(jfin, cut_ref.shape)
    return theta


def _attn_proj_kernel(x_ref, wq_ref, wkv_ref, wqi_ref, wkw_ref,
                      q_ref, k_ref, v_ref, kb_ref, vb_ref, qi_ref, ki_ref, kib_ref, wi_ref):
    xb = x_ref[...].astype(BF16)
    q = jnp.dot(xb, wq_ref[...], preferred_element_type=F32)
    q_ref[...] = (q * (ATT_HEAD_DIM ** -0.5)).astype(BF16)
    kv = jnp.dot(xb, wkv_ref[...], preferred_element_type=F32)
    k = kv[:, :ATT_KV_DIM]
    v = kv[:, ATT_KV_DIM:]
    k_ref[...] = k
    v_ref[...] = v
    kb_ref[...] = k.astype(BF16)
    vb_ref[...] = v.astype(BF16)
    qi = jnp.dot(xb, wqi_ref[...], preferred_element_type=F32)
    qi_ref[...] = (qi * (IDX_DIM ** -0.5)).astype(BF16)
    kw = jnp.dot(xb, wkw_ref[...], preferred_element_type=F32)
    ki = kw[:, :IDX_DIM]
    ki_ref[...] = ki
    kib_ref[...] = ki.astype(BF16)
    wi_ref[...] = kw[:, IDX_DIM:IDX_DIM + IDX_HEADS] * (IDX_HEADS ** -0.5)


def _attn_weights(w_in):
    s0 = ATT_Q_DIM
    s2 = s0 + 2 * ATT_KV_DIM
    s3 = s2 + IDX_HEADS * IDX_DIM
    wkw = jnp.pad(w_in[:, s3:], ((0, 0), (0, LANES - (w_in.shape[1] - s3))))
    return (w_in[:, :s0].astype(BF16), w_in[:, s0:s2].astype(BF16), w_in[:, s2:s3].astype(BF16),
            wkw.astype(BF16))


def _attn_proj(x2d, w_in):
    m = x2d.shape[0]
    tm = min(m, 512)
    wq, wkv, wqi, wkw = _attn_weights(w_in)
    row = lambda n: pl.BlockSpec((tm, n), lambda i: (i, 0))
    full = lambda a: pl.BlockSpec(a.shape, lambda i: (0, 0))
    outs = [(ATT_Q_DIM, BF16), (ATT_KV_DIM, F32), (ATT_KV_DIM, F32), (ATT_KV_DIM, BF16),
            (ATT_KV_DIM, BF16), (IDX_HEADS * IDX_DIM, BF16), (IDX_DIM, F32), (IDX_DIM, BF16),
            (IDX_HEADS, F32)]
    return pl.pallas_call(
        _attn_proj_kernel,
        grid=(m // tm,),
        in_specs=[row(D_MODEL), full(wq), full(wkv), full(wqi), full(wkw)],
        out_specs=[row(n) for n, _ in outs],
        out_shape=[jax.ShapeDtypeStruct((m, n), dt) for n, dt in outs],
        compiler_params=_cparams(("parallel",)),
        name="attn_proj",
    )(x2d, wq, wkv, wqi, wkw)


def _attn_proj_t_kernel(x_ref, wq_ref, wkv_ref, wqi_ref, wkw_ref,
                        k_ref, v_ref, ki_ref, kb_ref, kib_ref, qt_ref, qit_ref, wit_ref, vbt_ref):
    xb = x_ref[...].astype(BF16)
    q = jnp.dot(xb, wq_ref[...], preferred_element_type=F32)
    qt_ref[...] = (q * (ATT_HEAD_DIM ** -0.5 * LOG2_E)).T.astype(BF16)
    kv = jnp.dot(xb, wkv_ref[...], preferred_element_type=F32)
    k = kv[:, :ATT_KV_DIM]
    v = kv[:, ATT_KV_DIM:]
    k_ref[...] = k
    v_ref[...] = v
    kb_ref[...] = k.astype(BF16)
    vbt_ref[...] = v.T.astype(BF16)
    qi = jnp.dot(xb, wqi_ref[...], preferred_element_type=F32)
    qit_ref[...] = (qi * (IDX_DIM ** -0.5)).T.astype(BF16)
    kw = jnp.dot(xb, wkw_ref[...], preferred_element_type=F32)
    ki = kw[:, :IDX_DIM]
    ki_ref[...] = ki
    kib_ref[...] = ki.astype(BF16)
    wit_ref[...] = kw.T[IDX_DIM:IDX_DIM + IDX_HEADS, :] * (IDX_HEADS ** -0.5)


def _attn_proj_t(x2d, w_in):
    m = x2d.shape[0]
    tm = min(m, 512)
    wq, wkv, wqi, wkw = _attn_weights(w_in)
    row = lambda n: pl.BlockSpec((tm, n), lambda i: (i, 0))
    col = lambda n: pl.BlockSpec((n, tm), lambda i: (0, i))
    full = lambda a: pl.BlockSpec(a.shape, lambda i: (0, 0))
    nat = [(ATT_KV_DIM, F32), (ATT_KV_DIM, F32), (IDX_DIM, F32), (ATT_KV_DIM, BF16), (IDX_DIM, BF16)]
    tr = [(ATT_Q_DIM, BF16), (IDX_HEADS * IDX_DIM, BF16), (IDX_HEADS, F32), (ATT_KV_DIM, BF16)]
    return pl.pallas_call(
        _attn_proj_t_kernel,
        grid=(m // tm,),
        in_specs=[row(D_MODEL), full(wq), full(wkv), full(wqi), full(wkw)],
        out_specs=[row(n) for n, _ in nat] + [col(n) for n, _ in tr],
        out_shape=[jax.ShapeDtypeStruct((m, n), dt) for n, dt in nat] +
                  [jax.ShapeDtypeStruct((n, m), dt) for n, dt in tr],
        compiler_params=_cparams(("parallel",)),
        name="attn_proj_t",
    )(x2d, wq, wkv, wqi, wkw)


def _col_count(keys_ref, n_chunks, ck, pred):
    width = keys_ref.shape[1]
    par = math.gcd(ck, 64)

    def body(c, acc):
        base = pl.multiple_of(c * ck, ck)
        hit = pred(keys_ref[pl.ds(base, ck), :], base)
        return acc + jnp.sum(jnp.where(hit, 1.0, 0.0).reshape(ck // par, par, width), axis=0)
    acc = lax.fori_loop(0, n_chunks, body, jnp.zeros((par, width), F32))
    return jnp.sum(acc, axis=0, keepdims=True)


def _select_threshold_cols(keys_ref, n_chunks, ck, kth, idx_bits):
    width = keys_ref.shape[1]
    kth_f = float(kth)

    def count_ge(t):
        return _col_count(keys_ref, n_chunks, ck, lambda blk, base: blk >= t)

    theta = jnp.full((1, width), INT_MIN, I32)
    zero = jnp.zeros((1, width), I32)
    theta = jnp.where(count_ge(zero) >= kth_f, zero, theta)

    def bit_body(i, t):
        cand = t | lax.shift_left(jnp.int32(1), jnp.int32(30) - i)
        return jnp.where(count_ge(cand) >= kth_f, cand, t)
    theta = lax.fori_loop(0, 31, bit_body, theta)

    n_gt = count_ge(theta + 1)
    n_ge = count_ge(theta)
    need = kth_f - n_gt
    ambiguous = (n_ge - n_gt > need) & (theta > INT_MIN)
    no_cut = jnp.full((1, width), 2 ** idx_bits, I32)

    def tie_search():
        row = lax.broadcasted_iota(I32, (ck, width), 0)

        def ties_before(jc):
            return _col_count(keys_ref, n_chunks, ck, lambda blk, base: (blk == theta) & (row + base < jc))

        def jbit(i, jcur):
            cand = jcur | lax.shift_left(jnp.int32(1), jnp.int32(idx_bits - 1) - i)
            return jnp.where(ties_before(cand) <= need, cand, jcur)
        jfin = lax.fori_loop(0, idx_bits, jbit, jnp.zeros((1, width), I32))
        return jnp.where(ambiguous, jfin, no_cut)

    cut = lax.cond(jnp.max(jnp.where(ambiguous, 1.0, 0.0)) > 0.5, tie_search, lambda: no_cut)
    return theta, cut


def _dsa_prompt_kernel(qt_ref, qit_ref, wit_ref, kb_ref, vbt_ref, kib_ref, o_ref,
                       keys_ref, m_ref, acc_ref, bias_ref, s_ref, p_ref, *, ck, n_sel, idx_bits):
    i = pl.program_id(1)
    qb = Q_BLOCK
    n_chunks = ((i + 1) * qb + ck - 1) // ck
    q_pos = i * qb + lax.broadcasted_iota(I32, (ck, qb), 1)
    row = lax.broadcasted_iota(I32, (ck, qb), 0)

    qit = qit_ref[...]
    qi_all = jnp.concatenate([qit[h * IDX_DIM:(h + 1) * IDX_DIM, :] for h in range(IDX_HEADS)], axis=1)
    wit = wit_ref[...]

    def score_body(c, carry):
        base = pl.multiple_of(c * ck, ck)
        d = jnp.dot(kib_ref[0, pl.ds(base, ck), :], qi_all, preferred_element_type=F32)
        acc = jnp.zeros((ck, qb), F32)
        for h in range(IDX_HEADS):
            acc = acc + jnp.maximum(d[:, h * qb:(h + 1) * qb], 0.0) * wit[h:h + 1, :]
        keys_ref[pl.ds(base, ck), :] = jnp.where(row + base <= q_pos, _sort_key(acc), INT_MIN)
        return carry
    lax.fori_loop(0, n_chunks, score_body, 0)

    theta, cut = _select_threshold_cols(keys_ref, n_chunks, ck, n_sel, idx_bits)

    qt = qt_ref[...]
    qn = [jnp.concatenate([qt[(n * ATT_GROUP + g) * ATT_HEAD_DIM:(n * ATT_GROUP + g + 1) * ATT_HEAD_DIM, :]
                           for g in range(ATT_GROUP)], axis=1) for n in range(ATT_KV_HEADS)]
    m_ref[...] = jnp.full(m_ref.shape, NEG_INF, F32)
    acc_ref[...] = jnp.zeros(acc_ref.shape, F32)
    hd = ATT_HEAD_DIM
    ones = jnp.ones((acc_ref.shape[1] - hd, ck), BF16)
    par = math.gcd(ck, 64)

    def scores(kbase, n):
        kc = kb_ref[0, pl.ds(kbase, ck), n * hd:(n + 1) * hd]
        s_ref[n % 2] = jnp.dot(kc, qn[n], preferred_element_type=F32)
    scores(0, 0)

    def attn_body(c, carry):
        base = pl.multiple_of(c * ck, ck)
        key = keys_ref[pl.ds(base, ck), :]
        sel = ((key > theta) | ((key == theta) & (row + base < cut))) & (key != INT_MIN)
        bias_ref[...] = jnp.where(sel, 0.0, NEG_INF)

        for n in range(ATT_KV_HEADS):
            if n + 1 < ATT_KV_HEADS:
                scores(base, n + 1)
            else:
                scores(pl.multiple_of(jnp.minimum(c + 1, n_chunks - 1) * ck, ck), 0)
            vtc = jnp.concatenate([vbt_ref[n * hd:(n + 1) * hd, pl.ds(base, ck)], ones], axis=0)
            sn_ref = s_ref.at[n % 2]

            def masked(j):
                rs = slice(j * par, (j + 1) * par)
                return sn_ref[rs, :] + jnp.concatenate([bias_ref[rs, :]] * ATT_GROUP, axis=1)
            mx = masked(0)
            for j in range(1, ck // par):
                mx = jnp.maximum(mx, masked(j))
            m_prev = m_ref[n]
            m_new = jnp.maximum(m_prev, jnp.max(mx, axis=0, keepdims=True))
            m_safe = jnp.where(m_new == NEG_INF, 0.0, m_new)
            alpha = jnp.exp2(m_prev - m_safe)
            for j in range(ck // par):
                p_ref[j * par:(j + 1) * par, :] = jnp.exp2((masked(j) - m_safe).astype(BF16))
            acc_ref[n] = alpha * acc_ref[n] + jnp.dot(vtc, p_ref[...], preferred_element_type=F32)
            m_ref[n] = m_new
        return carry
    lax.fori_loop(0, n_chunks, attn_body, 0)

    ot = jnp.concatenate([acc_ref[n, :hd] / acc_ref[n, hd:hd + 1] for n in range(ATT_KV_HEADS)], axis=0)
    for g in range(ATT_GROUP):
        og = ot[:, g * qb:(g + 1) * qb].T
        for n in range(ATT_KV_HEADS):
            h = n * ATT_GROUP + g
            o_ref[:, h * ATT_HEAD_DIM:(h + 1) * ATT_HEAD_DIM] = (
                og[:, n * ATT_HEAD_DIM:(n + 1) * ATT_HEAD_DIM].astype(o_ref.dtype))


def _dsa_prompt(qt, qit, wit, kb, vbt, kib, b, n_sel):
    t = kb.shape[1]
    ck = min(512, t)
    nqb = t // Q_BLOCK
    idx_bits = max(1, math.ceil(math.log2(t))) + 1
    colblk = lambda n: pl.BlockSpec((n, Q_BLOCK), lambda bi, i: (0, bi * nqb + i))
    res = lambda n: pl.BlockSpec((1, t, n), lambda bi, i: (bi, 0, 0))
    gq = ATT_GROUP * Q_BLOCK
    return pl.pallas_call(
        functools.partial(_dsa_prompt_kernel, ck=ck, n_sel=n_sel, idx_bits=idx_bits),
        grid=(b, nqb),
        in_specs=[colblk(ATT_Q_DIM), colblk(IDX_HEADS * IDX_DIM), colblk(IDX_HEADS),
                  res(ATT_KV_DIM), pl.BlockSpec((ATT_KV_DIM, t), lambda bi, i: (0, bi)), res(IDX_DIM)],
        out_specs=pl.BlockSpec((Q_BLOCK, ATT_Q_DIM), lambda bi, i: (bi * nqb + i, 0)),
        out_shape=jax.ShapeDtypeStruct((b * t, ATT_Q_DIM), BF16),
        scratch_shapes=[pltpu.VMEM((t, Q_BLOCK), I32),
                        pltpu.VMEM((ATT_KV_HEADS, 1, gq), F32),
                        pltpu.VMEM((ATT_KV_HEADS, ATT_HEAD_DIM + BF16_ROWS, gq), F32),
                        pltpu.VMEM((ck, Q_BLOCK), F32), pltpu.VMEM((2, ck, gq), F32),
                        pltpu.VMEM((ck, gq), BF16)],
        compiler_params=_cparams(("parallel", "arbitrary"), 56),
        name="dsa_prompt",
    )(qt, qit, wit, kb, vbt, kib)


def _outproj_ln_kernel(o_ref, h_ref, w_ref, g_ref, b_ref, out_ref):
    y = jnp.dot(o_ref[...], w_ref[...], preferred_element_type=F32)
    out_ref[...] = _layer_norm(DEEPNORM_ALPHA * h_ref[...] + y, g_ref[...], b_ref[...])


def _outproj_ln(o2d, h2d, w_out, g, b):
    m, kdim = o2d.shape
    tm = min(m, 512)
    row = lambda n: pl.BlockSpec((tm, n), lambda i: (i, 0))
    return pl.pallas_call(
        _outproj_ln_kernel,
        grid=(m // tm,),
        in_specs=[row(kdim), row(D_MODEL), pl.BlockSpec((kdim, D_MODEL), lambda i: (0, 0)),
                  pl.BlockSpec((1, D_MODEL), lambda i: (0, 0)), pl.BlockSpec((1, D_MODEL), lambda i: (0, 0))],
        out_specs=row(D_MODEL),
        out_shape=jax.ShapeDtypeStruct((m, D_MODEL), F32),
        compiler_params=_cparams(("parallel",)),
        name="outproj_ln",
    )(o2d, h2d, w_out.astype(BF16), g.reshape(1, -1), b.reshape(1, -1))


def _dsa_sample_kernel(pt_ref, qbd_ref, qi_ref, wi_ref, kn_ref, vn_ref, kin_ref,
                       cki_hbm, ck_hbm, cv_hbm, o_ref,
                       ki_buf, k_buf, v_buf, s_ref, keys_ref, cut_ref, sem,
                       *, n_pages, t_new, n_sel, idx_bits, layer_off, lchunk):
    b = pl.program_id(0)
    past = n_pages * PAGE_SIZE
    ltot = past + LANES
    rows = keys_ref.shape[0]

    def page_copy(hbm, buf, p, s):
        page = pt_ref[b, p] + layer_off
        return pltpu.make_async_copy(hbm.at[page],
                                     buf.at[:, pl.ds(pl.multiple_of(p * PAGE_SIZE, PAGE_SIZE), PAGE_SIZE)],
                                     sem.at[s])

    def start_all(p, c):
        page_copy(cki_hbm, ki_buf, p, 0).start()
        page_copy(ck_hbm, k_buf, p, 1).start()
        page_copy(cv_hbm, v_buf, p, 2).start()
        return c
    lax.fori_loop(0, n_pages, start_all, 0)

    def wait_all(hbm, buf, s):
        def body(p, c):
            page_copy(hbm, buf, p, s).wait()
            return c
        lax.fori_loop(0, n_pages, body, 0)

    qi = qi_ref[0]
    wcol = wi_ref[0]
    wait_all(cki_hbm, ki_buf, 0)

    def idx_scores(kic, dims=NN_DIMS):
        d = lax.dot_general(qi, kic, dims, preferred_element_type=F32)
        d = jnp.maximum(d, 0.0) * wcol
        return jnp.sum(d.reshape(t_new, IDX_HEADS, d.shape[1]), axis=1)

    keys_ref[...] = jnp.full(keys_ref.shape, INT_MIN, I32)
    for c in range(past // lchunk):
        sc = idx_scores(ki_buf[:, c * lchunk:(c + 1) * lchunk].astype(BF16))
        keys_ref[0:t_new, c * lchunk:(c + 1) * lchunk] = _sort_key(sc)
    scn = idx_scores(kin_ref[0], NT_DIMS)
    tpos = lax.broadcasted_iota(I32, (t_new, LANES), 0)
    lpos = lax.broadcasted_iota(I32, (t_new, LANES), 1)
    keys_ref[0:t_new, past:ltot] = jnp.where((lpos <= tpos) & (lpos < t_new), _sort_key(scn), INT_MIN)

    theta = _select_threshold(keys_ref, cut_ref, ltot // LANES, LANES, rows, n_sel, idx_bits)
    key = keys_ref[...]
    lane = lax.broadcasted_iota(I32, (rows, ltot), 1)
    sel = ((key > theta) | ((key == theta) & (lane < cut_ref[:, :1]))) & (key != INT_MIN)
    bias = jnp.where(sel, 0.0, NEG_INF)[0:t_new]
    bias = jnp.broadcast_to(bias[:, None, :], (t_new, ATT_HEADS, ltot)).reshape(t_new * ATT_HEADS, ltot)

    qbd = qbd_ref[0]
    wait_all(ck_hbm, k_buf, 1)
    for c in range(past // lchunk):
        kc = k_buf[:, c * lchunk:(c + 1) * lchunk].astype(BF16)
        s_ref[:, c * lchunk:(c + 1) * lchunk] = jnp.dot(qbd, kc, preferred_element_type=F32)
    s_ref[:, past:ltot] = lax.dot_general(qbd, kn_ref[0], NT_DIMS, preferred_element_type=F32)
    s = s_ref[...] + bias
    m = jnp.max(s, axis=1, keepdims=True)
    p = jnp.exp(s - m)
    linv = 1.0 / jnp.sum(p, axis=1, keepdims=True)
    s_ref[...] = p
    wait_all(cv_hbm, v_buf, 2)
    o = jnp.dot(s_ref[:, past:ltot].astype(BF16), vn_ref[0], preferred_element_type=F32)
    for c in range(past // lchunk):
        vc = v_buf[:, c * lchunk:(c + 1) * lchunk].astype(BF16)
        o = o + lax.dot_general(s_ref[:, c * lchunk:(c + 1) * lchunk].astype(BF16), vc, NT_DIMS,
                                preferred_element_type=F32)
    o = o * linv
    nrow = (lax.broadcasted_iota(I32, (t_new * ATT_HEADS, ATT_HEAD_DIM), 0) % ATT_HEADS) // ATT_GROUP
    out = jnp.zeros((t_new * ATT_HEADS, ATT_HEAD_DIM), F32)
    for n in range(ATT_KV_HEADS):
        out = out + jnp.where(nrow == n, o[:, n * ATT_HEAD_DIM:(n + 1) * ATT_HEAD_DIM], 0.0)
    o_ref[0] = out.astype(o_ref.dtype)


def _dsa_sample(page_table, qs, qis, wi, kb, vb, kib, cache_ki, cache_k, cache_v, layer_off, n_sel):
    b, t, _ = qs.shape
    n_pages = page_table.shape[1]
    past = n_pages * PAGE_SIZE
    ltot = past + LANES
    rows = 8
    q5 = qs.reshape(b, t, ATT_KV_HEADS, ATT_GROUP, 1, ATT_HEAD_DIM)
    eye = jnp.eye(ATT_KV_HEADS, dtype=qs.dtype).reshape(1, 1, ATT_KV_HEADS, 1, ATT_KV_HEADS, 1)
    qbd = (q5 * eye).reshape(b, t * ATT_HEADS, ATT_KV_DIM)
    qi = qis.reshape(b, t * IDX_HEADS, IDX_DIM)
    wcol = wi.reshape(b, t * IDX_HEADS, 1)
    padr = lambda a: jnp.pad(a, ((0, 0), (0, LANES - t), (0, 0)))
    kn, vn, kin = padr(kb), padr(vb), padr(kib)
    idx_bits = max(1, math.ceil(math.log2(ltot))) + 1
    lchunk = min(1024, past)
    bl = lambda a: pl.BlockSpec((1,) + a.shape[1:], lambda i, pt: (i, 0, 0))
    anyspec = pl.BlockSpec(memory_space=pl.ANY)
    grid_spec = pltpu.PrefetchScalarGridSpec(
        num_scalar_prefetch=1,
        grid=(b,),
        in_specs=[bl(qbd), bl(qi), bl(wcol), bl(kn), bl(vn), bl(kin), anyspec, anyspec, anyspec],
        out_specs=pl.BlockSpec((1, t * ATT_HEADS, ATT_HEAD_DIM), lambda i, pt: (i, 0, 0)),
        scratch_shapes=[pltpu.VMEM((IDX_DIM, past), F32), pltpu.VMEM((ATT_KV_DIM, past), F32),
                        pltpu.VMEM((ATT_KV_DIM, past), F32), pltpu.VMEM((t * ATT_HEADS, ltot), F32),
                        pltpu.VMEM((rows, ltot), I32), pltpu.VMEM((rows, LANES), I32),
                        pltpu.SemaphoreType.DMA((3,))],
    )
    o = pl.pallas_call(
        functools.partial(_dsa_sample_kernel, n_pages=n_pages, t_new=t, n_sel=n_sel, idx_bits=idx_bits,
                          layer_off=layer_off, lchunk=lchunk),
        grid_spec=grid_spec,
        out_shape=jax.ShapeDtypeStruct((b, t * ATT_HEADS, ATT_HEAD_DIM), BF16),
        compiler_params=_cparams(("arbitrary",), 56),
        name="dsa_sample",
    )(page_table, qbd, qi, wcol, kn, vn, kin, cache_ki, cache_k, cache_v)
    return o.reshape(b, t, ATT_Q_DIM)


def _gla_proj_kernel(x_ref, wq_ref, wk_ref, wv_ref, wg_ref, wr_ref, wup_ref, bg_ref,
                     q_ref, k_ref, v_ref, la_ref, r_ref):
    xb = x_ref[...].astype(BF16)
    q_ref[...] = jnp.dot(xb, wq_ref[...], preferred_element_type=F32) * (GLA_DK ** -0.5)
    k_ref[...] = jnp.dot(xb, wk_ref[...], preferred_element_type=F32)
    v_ref[...] = jnp.dot(xb, wv_ref[...], preferred_element_type=F32)
    r_ref[...] = jnp.dot(xb, wr_ref[...], preferred_element_type=F32)
    gd = jnp.dot(xb, wg_ref[...], preferred_element_type=F32)
    glogit = jnp.dot(gd.astype(BF16), wup_ref[...], preferred_element_type=F32) + bg_ref[...]
    la_ref[...] = jax.nn.log_sigmoid(glogit) / GLA_TAU


def _gla_proj(x2d, w_in, w_gate_up, b_gate):
    m = x2d.shape[0]
    tm = min(m, 512)
    dq = GLA_HEADS * GLA_DK
    dv = GLA_HEADS * GLA_DV
    wq = w_in[:, :dq].astype(BF16)
    wk = w_in[:, dq:2 * dq].astype(BF16)
    wv = w_in[:, 2 * dq:2 * dq + dv].astype(BF16)
    wg = jnp.pad(w_in[:, 2 * dq + dv:2 * dq + dv + GLA_GATE_RANK], ((0, 0), (0, LANES - GLA_GATE_RANK))).astype(BF16)
    wr = w_in[:, 2 * dq + dv + GLA_GATE_RANK:].astype(BF16)
    wup = jnp.pad(w_gate_up, ((0, LANES - GLA_GATE_RANK), (0, 0))).astype(BF16)
    row = lambda n: pl.BlockSpec((tm, n), lambda i: (i, 0))
    full = lambda a: pl.BlockSpec(a.shape, lambda i: (0, 0))
    bg = b_gate.reshape(1, -1)
    outs = [dq, dq, dv, dq, dv]
    return pl.pallas_call(
        _gla_proj_kernel,
        grid=(m // tm,),
        in_specs=[row(D_MODEL), full(wq), full(wk), full(wv), full(wg), full(wr), full(wup), full(bg)],
        out_specs=[row(n) for n in outs],
        out_shape=[jax.ShapeDtypeStruct((m, n), F32) for n in outs],
        compiler_params=_cparams(("parallel",)),
        name="gla_proj",
    )(x2d, wq, wk, wv, wg, wr, wup, bg)


def _gla_chunk(q, k, v, g, st, c):
    sub = GLA_SUB
    n_sub = c // sub
    ri = lax.broadcasted_iota(I32, (c, c), 0)
    ci = lax.broadcasted_iota(I32, (c, c), 1)
    tri = jnp.where(ci <= ri, 1.0, 0.0).astype(F32)
    bc = jnp.dot(tri, g, preferred_element_type=F32, precision=lax.Precision.HIGHEST)
    o = lax.dot_general((q * jnp.exp(bc)).astype(BF16), st.astype(BF16), NT_DIMS, preferred_element_type=F32)
    if n_sub > 1:
        mrow = jnp.concatenate([jnp.zeros((sub, GLA_DK), F32)] +
                               [jnp.broadcast_to(bc[i * sub - 1:i * sub], (sub, GLA_DK)) for i in range(1, n_sub)], axis=0)
        qh = (q * jnp.exp(bc - mrow)).astype(BF16)
        parts = [jnp.zeros((sub, GLA_DV), F32)]
        for i in range(1, n_sub):
            kh = (k[:i * sub] * jnp.exp(bc[i * sub - 1:i * sub] - bc[:i * sub])).astype(BF16)
            a = lax.dot_general(qh[i * sub:(i + 1) * sub], kh, NT_DIMS, preferred_element_type=F32)
            parts.append(jnp.dot(a.astype(BF16), v[:i * sub].astype(BF16), preferred_element_type=F32))
        o = o + jnp.concatenate(parts, axis=0)
    rsub = lax.broadcasted_iota(I32, (c, 1), 0) % sub
    for dlt in range(sub):
        ks = k if dlt == 0 else pltpu.roll(k, dlt, 0)
        bs = bc if dlt == 0 else pltpu.roll(bc, dlt, 0)
        vs = v if dlt == 0 else pltpu.roll(v, dlt, 0)
        w = jnp.sum(q * ks * jnp.exp(jnp.minimum(bc - bs, 0.0)), axis=1, keepdims=True)
        o = o + jnp.where(rsub >= dlt, w, 0.0) * vs
    bl = bc[c - 1:c]
    kt = (k * jnp.exp(bl - bc)).astype(BF16)
    st_new = st * jnp.exp(bl) + lax.dot_general(v.astype(BF16), kt, TN_DIMS, preferred_element_type=F32)
    return o, st_new


def _gla_kernel(q_ref, k_ref, v_ref, g_ref, r_ref, ng_ref, s0_ref, o_ref, sf_ref, st_ref, *, c, n_c):
    tb = pl.program_id(2)

    @pl.when(tb == 0)
    def _():
        st_ref[...] = s0_ref[0, 0].T

    st = st_ref[...]
    for ci in range(n_c):
        sl = slice(ci * c, (ci + 1) * c)
        o, st = _gla_chunk(q_ref[0, sl, :], k_ref[0, sl, :], v_ref[0, sl, :], g_ref[0, sl, :], st, c)
        o = o * lax.rsqrt(jnp.mean(o * o, axis=-1, keepdims=True) + LN_EPS) * ng_ref[...]
        o_ref[0, sl, :] = (o * jax.nn.silu(r_ref[0, sl, :])).astype(o_ref.dtype)
    st_ref[...] = st

    @pl.when(tb == pl.num_programs(2) - 1)
    def _():
        sf_ref[0, 0] = st.T


def _gla(q, k, v, la, r, norm_g, s0):
    b, t, _ = q.shape
    c = math.gcd(t, GLA_CHUNK)
    tblk = math.gcd(t, 4 * GLA_CHUNK)
    n_c = tblk // c
    kspec = pl.BlockSpec((1, tblk, GLA_DK), lambda bi, h, i: (bi, i, h))
    vspec = pl.BlockSpec((1, tblk, GLA_DV), lambda bi, h, i: (bi, i, h))
    sspec = pl.BlockSpec((1, 1, GLA_DK, GLA_DV), lambda bi, h, i: (bi, h, 0, 0))
    return pl.pallas_call(
        functools.partial(_gla_kernel, c=c, n_c=n_c),
        grid=(b, GLA_HEADS, t // tblk),
        in_specs=[kspec, kspec, vspec, kspec, vspec, pl.BlockSpec((1, GLA_DV), lambda bi, h, i: (0, 0)), sspec],
        out_specs=[vspec, sspec],
        out_shape=[jax.ShapeDtypeStruct((b, t, GLA_HEADS * GLA_DV), BF16),
                   jax.ShapeDtypeStruct((b, GLA_HEADS, GLA_DK, GLA_DV), F32)],
        scratch_shapes=[pltpu.VMEM((GLA_DV, GLA_DK), F32)],
        compiler_params=_cparams(("parallel", "parallel", "arbitrary")),
        name="gla_scan",
    )(q, k, v, la, r, norm_g.reshape(1, -1), s0)


def _gla_mixer(x, s0, w_in, w_gate_up, b_gate, norm_g):
    b, t, _ = x.shape
    q, k, v, la, r = _gla_proj(x.reshape(b * t, D_MODEL), w_in, w_gate_up, b_gate)
    tp = -(-t // GLA_SUB) * GLA_SUB
    r3 = lambda a: jnp.pad(a.reshape(b, t, -1), ((0, 0), (0, tp - t), (0, 0)))
    o, sf = _gla(r3(q), r3(k), r3(v), r3(la), r3(r), norm_g, s0)
    return o[:, :t], sf


def _top_values(s, k):
    vals = []
    cur = s
    for _ in range(k):
        mx = jnp.max(cur, axis=0, keepdims=True)
        vals.append(mx)
        cur = jnp.where(cur == mx, NEG_INF, cur)
    return vals


def _top_values_ranked(s, k):
    vals = []
    cur = s
    rank = jnp.full(s.shape, float(k), F32)
    for i in range(k):
        mx = jnp.max(cur, axis=0, keepdims=True)
        vals.append(mx)
        top = cur == mx
        rank = jnp.where(top, float(i), rank)
        cur = jnp.where(top, NEG_INF, cur)
    return vals, rank


def _peer_select_kernel(x_ref, wq_ref, k1_ref, k2_ref, c1_ref, p_ref, r2_ref, e_ref):
    xb = x_ref[...].astype(BF16)
    q = jnp.dot(xb, wq_ref[...], preferred_element_type=F32).astype(BF16)
    tt = q.shape[0]
    k1 = k1_ref[...]
    k2 = k2_ref[...]
    sub8 = lax.broadcasted_iota(I32, (8, tt), 0)
    for h in range(PEER_HEADS):
        s1 = lax.dot_general(k1, q[:, h * PEER_D_KEY:h * PEER_D_KEY + PEER_HALF], NT_DIMS,
                             preferred_element_type=F32)
        s2 = lax.dot_general(k2, q[:, h * PEER_D_KEY + PEER_HALF:(h + 1) * PEER_D_KEY], NT_DIMS,
                             preferred_element_type=F32)
        t1, rank1 = _top_values_ranked(s1, PEER_TOPK)
        t2, rank2 = _top_values_ranked(s2, PEER_TOPK)
        t2a = jnp.concatenate(t2, axis=0)
        cands = [t1[0] + t2a]
        for a in range(1, PEER_TOPK):
            nb = PEER_TOPK // (a + 1)
            cands.append(jnp.where(sub8 < nb, t1[a] + t2a[:8], NEG_INF))
        best = _top_values(jnp.concatenate(cands, axis=0), PEER_TOPK)
        theta = best[PEER_TOPK - 1]
        z = best[0] * 0.0
        for bv in best:
            z = z + jnp.exp(bv - best[0])
        count1 = jnp.zeros(s1.shape, F32)
        for a in range(PEER_TOPK):
            cnt_a = jnp.sum(jnp.where(t1[a] + t2a >= theta, 1.0, 0.0), axis=0, keepdims=True)
            count1 = jnp.where(rank1 == float(a), cnt_a, count1)
        gate1 = jnp.exp(s1 - t1[0]) * (0.5 / z)
        gate2 = jnp.exp(s2 - t2[0])
        for lg in range(tt // LANES):
            ls = slice(lg * LANES, (lg + 1) * LANES)
            c1_ref[lg, h] = count1[:, ls]
            p_ref[lg, h] = gate1[:, ls]
            r2_ref[lg, h] = rank2[:, ls]
            e_ref[lg, h] = gate2[:, ls]


def _peer_select(x2d, w_q, keys1, keys2):
    n = x2d.shape[0]
    tt = min(n, 256)
    wq = w_q.astype(BF16)
    k1 = keys1.astype(BF16)
    k2 = keys2.astype(BF16)
    full = lambda a: pl.BlockSpec(a.shape, lambda i: (0,) * a.ndim)
    hspec = pl.BlockSpec((tt // LANES, PEER_HEADS, PEER_N_KEYS, LANES), lambda i: (i, 0, 0, 0))
    big = lambda dt: jax.ShapeDtypeStruct((n // LANES, PEER_HEADS, PEER_N_KEYS, LANES), dt)
    return pl.pallas_call(
        _peer_select_kernel,
        grid=(n // tt,),
        in_specs=[pl.BlockSpec((tt, D_MODEL), lambda i: (i, 0)), full(wq), full(k1), full(k2)],
        out_specs=[hspec, hspec, hspec, hspec],
        out_shape=[big(F32), big(F32), big(F32), big(F32)],
        compiler_params=_cparams(("parallel",)),
        name="peer_select",
    )(x2d, wq, k1, k2)


def _peer_mix_kernel(h_ref, c1_ref, p_ref, r2_ref, e_ref, u_ref, vt_ref, g_ref, bb_ref, out_ref,
                     xb_ref, ht0_ref, ht1_ref, wt0_ref, wt1_ref, yt_ref, *, n_i1):
    eb = pl.program_id(1)
    n_blk = pl.num_programs(1) - 2
    tt = h_ref.shape[0]

    @pl.when(eb == 0)
    def _():
        xb_ref[...] = h_ref[...].T.astype(BF16)
        yt_ref[...] = jnp.zeros(yt_ref.shape, F32)
        ht1_ref[...] = jnp.zeros(ht1_ref.shape, F32)
        wt0_ref[...] = jnp.zeros(wt0_ref.shape, BF16)

    i1_base = pl.multiple_of(jnp.clip(eb - 1, 0, n_blk - 1) * n_i1, n_i1)

    def gate_rows(cb, pb, ks, lg):
        terms = [jnp.where(r2_ref[lg, h, ks, :] < cb[h], pb[h] * e_ref[lg, h, ks, :], 0.0)
                 for h in range(PEER_HEADS)]
        while len(terms) > 1:
            terms = [terms[i] + terms[i + 1] for i in range(0, len(terms), 2)]
        return terms[0]

    def stages(q):
        ht_w, ht_r = (ht0_ref, ht1_ref) if q == 0 else (ht1_ref, ht0_ref)
        wt_r, wt_w = (wt0_ref, wt1_ref) if q == 0 else (wt1_ref, wt0_ref)
        n_lg = tt // LANES
        eblk = ht_w.shape[0]

        def matmul_piece(k):
            if k % 2 == 0:
                rows = slice((k // 2) * (eblk // n_lg), (k // 2 + 1) * (eblk // n_lg))
                ht_w[rows, :] = jnp.dot(u_ref[rows, :], xb_ref[...], preferred_element_type=F32)
            else:
                rows = slice((k // 2) * (D_MODEL // n_lg), (k // 2 + 1) * (D_MODEL // n_lg))
                yt_ref[rows, :] += jnp.dot(vt_ref[rows, :], wt_r[...], preferred_element_type=F32)

        for lg in range(n_lg):
            ls = slice(lg * LANES, (lg + 1) * LANES)
            c8 = [c1_ref[lg, h, pl.ds(i1_base, n_i1), :] for h in range(PEER_HEADS)]
            p8 = [p_ref[lg, h, pl.ds(i1_base, n_i1), :] for h in range(PEER_HEADS)]
            for j in range(n_i1):
                if j % (n_i1 // 2) == 0:
                    matmul_piece(2 * lg + j // (n_i1 // 2))
                cb = [jnp.broadcast_to(c8[h][j:j + 1], (8, LANES)) for h in range(PEER_HEADS)]
                pb = [jnp.broadcast_to(p8[h][j:j + 1], (8, LANES)) for h in range(PEER_HEADS)]
                for r in range(PEER_N_KEYS // BF16_ROWS):
                    k0 = r * BF16_ROWS
                    acc = jnp.concatenate([gate_rows(cb, pb, slice(k0, k0 + 8), lg),
                                           gate_rows(cb, pb, slice(k0 + 8, k0 + 16), lg)], axis=0)
                    rs = slice(j * PEER_N_KEYS + k0, j * PEER_N_KEYS + k0 + BF16_ROWS)
                    x = ht_r[rs, ls]
                    gelu2 = x * (1.0 + lax.erf(x * (2.0 ** -0.5)))
                    wt_w[rs, ls] = (gelu2 * acc).astype(BF16)

    pl.when(eb % 2 == 0)(functools.partial(stages, 0))
    pl.when(eb % 2 == 1)(functools.partial(stages, 1))

    @pl.when(eb == pl.num_programs(1) - 1)
    def _():
        y = yt_ref[...].T
        out_ref[...] = _layer_norm(DEEPNORM_ALPHA * h_ref[...] + y, g_ref[...], bb_ref[...])


def _peer_ln(h2d, w_q, keys1, keys2, u, v, g, b):
    n_real = h2d.shape[0]
    if n_real % LANES:
        h2d = jnp.pad(h2d, ((0, LANES - n_real % LANES), (0, 0)))
        return _peer_ln(h2d, w_q, keys1, keys2, u, v, g, b)[:n_real]
    n = n_real
    c1, p, r2, e = _peer_select(h2d, w_q, keys1, keys2)
    tt = min(n, 512)
    eblk = 1024
    n_e = u.shape[0]
    ub = u.astype(BF16)
    n_blk = n_e // eblk
    vt = jnp.swapaxes(v.reshape(n_blk, eblk, D_MODEL), 1, 2).astype(BF16)
    hspec = pl.BlockSpec((tt // LANES, PEER_HEADS, PEER_N_KEYS, LANES), lambda i, j: (i, 0, 0, 0))
    vec = pl.BlockSpec((1, D_MODEL), lambda i, j: (0, 0))
    return pl.pallas_call(
        functools.partial(_peer_mix_kernel, n_i1=eblk // PEER_N_KEYS),
        grid=(n // tt, n_blk + 2),
        in_specs=[pl.BlockSpec((tt, D_MODEL), lambda i, j: (i, 0)), hspec, hspec, hspec, hspec,
                  pl.BlockSpec((eblk, D_MODEL), lambda i, j: (jnp.minimum(j, n_blk - 1), 0)),
                  pl.BlockSpec((None, D_MODEL, eblk), lambda i, j: (jnp.clip(j - 2, 0, n_blk - 1), 0, 0)), vec, vec],
        out_specs=pl.BlockSpec((tt, D_MODEL), lambda i, j: (i, 0)),
        out_shape=jax.ShapeDtypeStruct((n, D_MODEL), F32),
        scratch_shapes=[pltpu.VMEM((D_MODEL, tt), BF16), pltpu.VMEM((eblk, tt), F32), pltpu.VMEM((eblk, tt), F32),
                        pltpu.VMEM((eblk, tt), BF16), pltpu.VMEM((eblk, tt), BF16),
                        pltpu.VMEM((D_MODEL, tt), F32)],
        compiler_params=_cparams(("parallel", "arbitrary"), 56),
        name="peer_mix",
    )(h2d, c1, p, r2, e, ub, vt, g.reshape(1, -1), b.reshape(1, -1))


def kernel(x_prompt, x_sample, cache_k, cache_v, cache_kidx, state_gla, page_table,
           attn_w_in, attn_w_out, gla_w_in, gla_w_gate_up, gla_b_gate, gla_norm_g, gla_w_out,
           peer_w_q, peer_keys1, peer_keys2, peer_u, peer_v, ln1_g, ln1_b, ln2_g, ln2_b):
    bp, tp, _ = x_prompt.shape
    bs, ts, _ = x_sample.shape
    n_pool = cache_k.shape[1]
    past = page_table.shape[1] * PAGE_SIZE
    ck = jnp.swapaxes(cache_k.reshape(-1, PAGE_SIZE, ATT_KV_DIM), 1, 2)
    cv = jnp.swapaxes(cache_v.reshape(-1, PAGE_SIZE, ATT_KV_DIM), 1, 2)
    cki = jnp.swapaxes(cache_kidx.reshape(-1, PAGE_SIZE, IDX_DIM), 1, 2)
    hp = x_prompt.reshape(bp * tp, D_MODEL)
    hs = x_sample.reshape(bs * ts, D_MODEL)
    kp_l, vp_l, kip_l, sp_l = [], [], [], []
    ks_l, vs_l, kis_l, ss_l = [], [], [], []
    for i in range(DEPTH):
        j = i // 2
        if i % 2 == 0:
            k, v, ki, kb, kib, qt, qit, wit, vbt = _attn_proj_t(hp, attn_w_in[j])
            r3 = lambda a: a.reshape(bp, tp, -1)
            op = _dsa_prompt(qt, qit, wit, r3(kb), vbt, r3(kib), bp, min(TOPK_MAX, tp // 4))
            kp_l.append(k.reshape(bp, tp, ATT_KV_HEADS, ATT_HEAD_DIM))
            vp_l.append(v.reshape(bp, tp, ATT_KV_HEADS, ATT_HEAD_DIM))
            kip_l.append(ki.reshape(bp, tp, IDX_DIM))
            qs, k, v, kb, vb, qis, ki, kib, wi = _attn_proj(hs, attn_w_in[j])
            r3 = lambda a: a.reshape(bs, ts, -1)
            os_ = _dsa_sample(page_table, r3(qs), r3(qis), r3(wi), r3(kb), r3(vb), r3(kib), cki, ck, cv,
                              j * n_pool, min(TOPK_MAX, (past + ts) // 4))
            ks_l.append(k.reshape(bs, ts, ATT_KV_HEADS, ATT_HEAD_DIM))
            vs_l.append(v.reshape(bs, ts, ATT_KV_HEADS, ATT_HEAD_DIM))
            kis_l.append(ki.reshape(bs, ts, IDX_DIM))
            w_out = attn_w_out[j]
        else:
            s0 = jnp.zeros((bp, GLA_HEADS, GLA_DK, GLA_DV), F32)
            op, sp = _gla_mixer(hp.reshape(bp, tp, D_MODEL), s0, gla_w_in[j], gla_w_gate_up[j], gla_b_gate[j],
                                gla_norm_g[j])
            os_, ss = _gla_mixer(hs.reshape(bs, ts, D_MODEL), state_gla[j], gla_w_in[j], gla_w_gate_up[j],
                                 gla_b_gate[j], gla_norm_g[j])
            sp_l.append(sp)
            ss_l.append(ss)
            w_out = gla_w_out[j]
        hp = _outproj_ln(op.reshape(bp * tp, -1), hp, w_out, ln1_g[i], ln1_b[i])
        hs = _outproj_ln(os_.reshape(bs * ts, -1), hs, w_out, ln1_g[i], ln1_b[i])
        pw = (peer_w_q[i], peer_keys1[i], peer_keys2[i], peer_u[i], peer_v[i], ln2_g[i], ln2_b[i])
        hp = _peer_ln(hp, *pw)
        hs = _peer_ln(hs, *pw)
    return (hp.reshape(bp, tp, D_MODEL), hs.reshape(bs, ts, D_MODEL),
            jnp.stack(kp_l), jnp.stack(vp_l), jnp.stack(kip_l), jnp.stack(sp_l),
            jnp.stack(ks_l), jnp.stack(vs_l), jnp.stack(kis_l), jnp.stack(ss_l))
```

```python
import functools
import math

import jax
import jax.numpy as jnp
from jax import lax
from jax.experimental import pallas as pl
from jax.experimental.pallas import tpu as pltpu

F32 = jnp.float32
BF16 = jnp.bfloat16
I32 = jnp.int32

D_MODEL = 1024
DEPTH = 2
PAGE_SIZE = 128
DEEPNORM_ALPHA = (2.0 * DEPTH) ** 0.25
LN_EPS = 1e-5

ATT_HEADS = 16
ATT_KV_HEADS = 4
ATT_HEAD_DIM = 64
ATT_GROUP = ATT_HEADS // ATT_KV_HEADS
IDX_HEADS = 8
IDX_DIM = 64
TOPK_MAX = 256
Q_BLOCK = 128
ATT_Q_DIM = ATT_HEADS * ATT_HEAD_DIM
ATT_KV_DIM = ATT_KV_HEADS * ATT_HEAD_DIM

GLA_HEADS = 4
GLA_DK = D_MODEL // 2 // GLA_HEADS
GLA_DV = D_MODEL // GLA_HEADS
GLA_GATE_RANK = 16
GLA_TAU = 16.0
GLA_CHUNK = 64
GLA_SUB = 16

PEER_HEADS = 8
PEER_N_KEYS = 128
PEER_D_KEY = 256
PEER_HALF = PEER_D_KEY // 2
PEER_TOPK = 16

LANES = 128
BF16_ROWS = 16
LOG2_E = 1.4426950408889634
INT_MIN = -(2 ** 31)
NEG_INF = float("-inf")

NN_DIMS = (((1,), (0,)), ((), ()))
NT_DIMS = (((1,), (1,)), ((), ()))
TN_DIMS = (((0,), (0,)), ((), ()))


def _cparams(sem, vmem_mib=None, flags=None):
    kw = dict(dimension_semantics=sem)
    if vmem_mib is not None:
        kw["vmem_limit_bytes"] = vmem_mib * 1024 * 1024
    if flags:
        kw["flags"] = flags
    return pltpu.CompilerParams(**kw)


def _sort_key(x):
    b = pltpu.bitcast(x + 0.0, I32)
    return b ^ ((b >> 31) & 0x7FFFFFFF)


def _layer_norm(z, g, b):
    mu = jnp.mean(z, axis=-1, keepdims=True)
    zc = z - mu
    var = jnp.mean(zc * zc, axis=-1, keepdims=True)
    return zc * lax.rsqrt(var + LN_EPS) * g + b


def _row_count(keys_ref, n_chunks, ck, rows, pred):
    def body(c, acc):
        base = pl.multiple_of(c * ck, ck)
        return acc + jnp.where(pred(keys_ref[:, pl.ds(base, ck)], base), 1.0, 0.0)
    acc = lax.fori_loop(0, n_chunks, body, jnp.zeros((rows, ck), F32))
    return jnp.sum(acc, axis=1, keepdims=True)


def _select_threshold(keys_ref, cut_ref, n_chunks, ck, rows, kth, idx_bits):
    kth_f = float(kth)

    def count_ge(t):
        tb = jnp.broadcast_to(t, (rows, ck))
        return _row_count(keys_ref, n_chunks, ck, rows, lambda blk, base: blk >= tb)

    theta = jnp.full((rows, 1), INT_MIN, I32)
    zero = jnp.zeros((rows, 1), I32)
    theta = jnp.where(count_ge(zero) >= kth_f, zero, theta)

    def bit_body(i, t):
        cand = t | lax.shift_left(jnp.int32(1), jnp.int32(30) - i)
        return jnp.where(count_ge(cand) >= kth_f, cand, t)
    theta = lax.fori_loop(0, 31, bit_body, theta)

    n_gt = count_ge(theta + 1)
    n_ge = count_ge(theta)
    need = kth_f - n_gt
    ambiguous = (n_ge - n_gt > need) & (theta > INT_MIN)
    cut_ref[...] = jnp.full(cut_ref.shape, 2 ** idx_bits, I32)

    @pl.when(jnp.max(jnp.where(ambiguous, 1.0, 0.0)) > 0.5)
    def _():
        thb = jnp.broadcast_to(theta, (rows, ck))
        lane = lax.broadcasted_iota(I32, (rows, ck), 1)

        def ties_before(jc):
            jb = jnp.broadcast_to(jc, (rows, ck))
            return _row_count(keys_ref, n_chunks, ck, rows,
                              lambda blk, base: (blk == thb) & (lane + base < jb))

        def jbit(i, jcur):
            cand = jcur | lax.shift_left(jnp.int32(1), jnp.int32(idx_bits - 1) - i)
            return jnp.where(ties_before(cand) <= need, cand, jcur)
        jfin = lax.fori_loop(0, idx_bits, jbit, jnp.zeros((rows, 1), I32))
        jfin = jnp.where(ambiguous, jfin, 2 ** idx_bits)
        cut_ref[...] = jnp.broadcast_to(jfin, cut_ref.shape)
    return theta


def _attn_proj_kernel(x_ref, wq_ref, wkv_ref, wqi_ref, wkw_ref,
                      q_ref, k_ref, v_ref, kb_ref, vb_ref, qi_ref, ki_ref, kib_ref, wi_ref):
    xb = x_ref[...].astype(BF16)
    q = jnp.dot(xb, wq_ref[...], preferred_element_type=F32)
    q_ref[...] = (q * (ATT_HEAD_DIM ** -0.5)).astype(BF16)
    kv = jnp.dot(xb, wkv_ref[...], preferred_element_type=F32)
    k = kv[:, :ATT_KV_DIM]
    v = kv[:, ATT_KV_DIM:]
    k_ref[...] = k
    v_ref[...] = v
    kb_ref[...] = k.astype(BF16)
    vb_ref[...] = v.astype(BF16)
    qi = jnp.dot(xb, wqi_ref[...], preferred_element_type=F32)
    qi_ref[...] = (qi * (IDX_DIM ** -0.5)).astype(BF16)
    kw = jnp.dot(xb, wkw_ref[...], preferred_element_type=F32)
    ki = kw[:, :IDX_DIM]
    ki_ref[...] = ki
    kib_ref[...] = ki.astype(BF16)
    wi_ref[...] = kw[:, IDX_DIM:IDX_DIM + IDX_HEADS] * (IDX_HEADS ** -0.5)


def _attn_weights(w_in):
    s0 = ATT_Q_DIM
    s2 = s0 + 2 * ATT_KV_DIM
    s3 = s2 + IDX_HEADS * IDX_DIM
    wkw = jnp.pad(w_in[:, s3:], ((0, 0), (0, LANES - (w_in.shape[1] - s3))))
    return (w_in[:, :s0].astype(BF16), w_in[:, s0:s2].astype(BF16), w_in[:, s2:s3].astype(BF16),
            wkw.astype(BF16))


def _attn_proj(x2d, w_in):
    m = x2d.shape[0]
    tm = min(m, 512)
    wq, wkv, wqi, wkw = _attn_weights(w_in)
    row = lambda n: pl.BlockSpec((tm, n), lambda i: (i, 0))
    full = lambda a: pl.BlockSpec(a.shape, lambda i: (0, 0))
    outs = [(ATT_Q_DIM, BF16), (ATT_KV_DIM, F32), (ATT_KV_DIM, F32), (ATT_KV_DIM, BF16),
            (ATT_KV_DIM, BF16), (IDX_HEADS * IDX_DIM, BF16), (IDX_DIM, F32), (IDX_DIM, BF16),
            (IDX_HEADS, F32)]
    return pl.pallas_call(
        _attn_proj_kernel,
        grid=(m // tm,),
        in_specs=[row(D_MODEL), full(wq), full(wkv), full(wqi), full(wkw)],
        out_specs=[row(n) for n, _ in outs],
        out_shape=[jax.ShapeDtypeStruct((m, n), dt) for n, dt in outs],
        compiler_params=_cparams(("parallel",)),
        name="attn_proj",
    )(x2d, wq, wkv, wqi, wkw)


def _attn_proj_t_kernel(x_ref, wq_ref, wkv_ref, wqi_ref, wkw_ref,
                        k_ref, v_ref, ki_ref, kb_ref, kib_ref, qt_ref, qit_ref, wit_ref, vbt_ref):
    xb = x_ref[...].astype(BF16)
    q = jnp.dot(xb, wq_ref[...], preferred_element_type=F32)
    qt_ref[...] = (q * (ATT_HEAD_DIM ** -0.5 * LOG2_E)).T.astype(BF16)
    kv = jnp.dot(xb, wkv_ref[...], preferred_element_type=F32)
    k = kv[:, :ATT_KV_DIM]
    v = kv[:, ATT_KV_DIM:]
    k_ref[...] = k
    v_ref[...] = v
    kb_ref[...] = k.astype(BF16)
    vbt_ref[...] = v.T.astype(BF16)
    qi = jnp.dot(xb, wqi_ref[...], preferred_element_type=F32)
    qit_ref[...] = (qi * (IDX_DIM ** -0.5)).T.astype(BF16)
    kw = jnp.dot(xb, wkw_ref[...], preferred_element_type=F32)
    ki = kw[:, :IDX_DIM]
    ki_ref[...] = ki
    kib_ref[...] = ki.astype(BF16)
    wit_ref[...] = kw.T[IDX_DIM:IDX_DIM + IDX_HEADS, :] * (IDX_HEADS ** -0.5)


def _attn_proj_t(x2d, w_in):
    m = x2d.shape[0]
    tm = min(m, 512)
    wq, wkv, wqi, wkw = _attn_weights(w_in)
    row = lambda n: pl.BlockSpec((tm, n), lambda i: (i, 0))
    col = lambda n: pl.BlockSpec((n, tm), lambda i: (0, i))
    full = lambda a: pl.BlockSpec(a.shape, lambda i: (0, 0))
    nat = [(ATT_KV_DIM, F32), (ATT_KV_DIM, F32), (IDX_DIM, F32), (ATT_KV_DIM, BF16), (IDX_DIM, BF16)]
    tr = [(ATT_Q_DIM, BF16), (IDX_HEADS * IDX_DIM, BF16), (IDX_HEADS, F32), (ATT_KV_DIM, BF16)]
    return pl.pallas_call(
        _attn_proj_t_kernel,
        grid=(m // tm,),
        in_specs=[row(D_MODEL), full(wq), full(wkv), full(wqi), full(wkw)],
        out_specs=[row(n) for n, _ in nat] + [col(n) for n, _ in tr],
        out_shape=[jax.ShapeDtypeStruct((m, n), dt) for n, dt in nat] +
                  [jax.ShapeDtypeStruct((n, m), dt) for n, dt in tr],
        compiler_params=_cparams(("parallel",)),
        name="attn_proj_t",
    )(x2d, wq, wkv, wqi, wkw)


def _col_count(keys_ref, n_chunks, ck, pred):
    width = keys_ref.shape[1]
    par = math.gcd(ck, 64)

    def body(c, acc):
        base = pl.multiple_of(c * ck, ck)
        hit = pred(keys_ref[pl.ds(base, ck), :], base)
        return acc + jnp.sum(jnp.where(hit, 1.0, 0.0).reshape(ck // par, par, width), axis=0)
    acc = lax.fori_loop(0, n_chunks, body, jnp.zeros((par, width), F32))
    return jnp.sum(acc, axis=0, keepdims=True)


def _select_threshold_cols(keys_ref, n_chunks, ck, kth, idx_bits):
    width = keys_ref.shape[1]
    kth_f = float(kth)

    def count_ge(t):
        return _col_count(keys_ref, n_chunks, ck, lambda blk, base: blk >= t)

    theta = jnp.full((1, width), INT_MIN, I32)
    zero = jnp.zeros((1, width), I32)
    theta = jnp.where(count_ge(zero) >= kth_f, zero, theta)

    def bit_body(i, t):
        cand = t | lax.shift_left(jnp.int32(1), jnp.int32(30) - i)
        return jnp.where(count_ge(cand) >= kth_f, cand, t)
    theta = lax.fori_loop(0, 31, bit_body, theta)

    n_gt = count_ge(theta + 1)
    n_ge = count_ge(theta)
    need = kth_f - n_gt
    ambiguous = (n_ge - n_gt > need) & (theta > INT_MIN)
    no_cut = jnp.full((1, width), 2 ** idx_bits, I32)

    def tie_search():
        row = lax.broadcasted_iota(I32, (ck, width), 0)

        def ties_before(jc):
            return _col_count(keys_ref, n_chunks, ck, lambda blk, base: (blk == theta) & (row + base < jc))

        def jbit(i, jcur):
            cand = jcur | lax.shift_left(jnp.int32(1), jnp.int32(idx_bits - 1) - i)
            return jnp.where(ties_before(cand) <= need, cand, jcur)
        jfin = lax.fori_loop(0, idx_bits, jbit, jnp.zeros((1, width), I32))
        return jnp.where(ambiguous, jfin, no_cut)

    cut = lax.cond(jnp.max(jnp.where(ambiguous, 1.0, 0.0)) > 0.5, tie_search, lambda: no_cut)
    return theta, cut


def _dsa_prompt_kernel(qt_ref, qit_ref, wit_ref, kb_ref, vbt_ref, kib_ref, o_ref,
                       keys_ref, m_ref, acc_ref, bias_ref, s_ref, p_ref, *, ck, n_sel, idx_bits):
    i = pl.program_id(1)
    qb = Q_BLOCK
    n_chunks = ((i + 1) * qb + ck - 1) // ck
    q_pos = i * qb + lax.broadcasted_iota(I32, (ck, qb), 1)
    row = lax.broadcasted_iota(I32, (ck, qb), 0)

    qit = qit_ref[...]
    qi_all = jnp.concatenate([qit[h * IDX_DIM:(h + 1) * IDX_DIM, :] for h in range(IDX_HEADS)], axis=1)
    wit = wit_ref[...]

    def score_body(c, carry):
        base = pl.multiple_of(c * ck, ck)
        d = jnp.dot(kib_ref[0, pl.ds(base, ck), :], qi_all, preferred_element_type=F32)
        acc = jnp.zeros((ck, qb), F32)
        for h in range(IDX_HEADS):
            acc = acc + jnp.maximum(d[:, h * qb:(h + 1) * qb], 0.0) * wit[h:h + 1, :]
        keys_ref[pl.ds(base, ck), :] = jnp.where(row + base <= q_pos, _sort_key(acc), INT_MIN)
        return carry
    lax.fori_loop(0, n_chunks, score_body, 0)

    theta, cut = _select_threshold_cols(keys_ref, n_chunks, ck, n_sel, idx_bits)

    qt = qt_ref[...]
    qn = [jnp.concatenate([qt[(n * ATT_GROUP + g) * ATT_HEAD_DIM:(n * ATT_GROUP + g + 1) * ATT_HEAD_DIM, :]
                           for g in range(ATT_GROUP)], axis=1) for n in range(ATT_KV_HEADS)]
    m_ref[...] = jnp.full(m_ref.shape, NEG_INF, F32)
    acc_ref[...] = jnp.zeros(acc_ref.shape, F32)
    hd = ATT_HEAD_DIM
    ones = jnp.ones((acc_ref.shape[1] - hd, ck), BF16)
    par = math.gcd(ck, 64)

    def scores(kbase, n):
        kc = kb_ref[0, pl.ds(kbase, ck), n * hd:(n + 1) * hd]
        s_ref[n % 2] = jnp.dot(kc, qn[n], preferred_element_type=F32)
    scores(0, 0)

    def attn_body(c, carry):
        base = pl.multiple_of(c * ck, ck)
        key = keys_ref[pl.ds(base, ck), :]
        sel = ((key > theta) | ((key == theta) & (row + base < cut))) & (key != INT_MIN)
        bias_ref[...] = jnp.where(sel, 0.0, NEG_INF)

        for n in range(ATT_KV_HEADS):
            if n + 1 < ATT_KV_HEADS:
                scores(base, n + 1)
            else:
                scores(pl.multiple_of(jnp.minimum(c + 1, n_chunks - 1) * ck, ck), 0)
            vtc = jnp.concatenate([vbt_ref[n * hd:(n + 1) * hd, pl.ds(base, ck)], ones], axis=0)
            sn_ref = s_ref.at[n % 2]

            def masked(j):
                rs = slice(j * par, (j + 1) * par)
                return sn_ref[rs, :] + jnp.concatenate([bias_ref[rs, :]] * ATT_GROUP, axis=1)
            mx = masked(0)
            for j in range(1, ck // par):
                mx = jnp.maximum(mx, masked(j))
            m_prev = m_ref[n]
            m_new = jnp.maximum(m_prev, jnp.max(mx, axis=0, keepdims=True))
            m_safe = jnp.where(m_new == NEG_INF, 0.0, m_new)
            alpha = jnp.exp2(m_prev - m_safe)
            for j in range(ck // par):
                p_ref[j * par:(j + 1) * par, :] = jnp.exp2((masked(j) - m_safe).astype(BF16))
            acc_ref[n] = alpha * acc_ref[n] + jnp.dot(vtc, p_ref[...], preferred_element_type=F32)
            m_ref[n] = m_new
        return carry
    lax.fori_loop(0, n_chunks, attn_body, 0)

    ot = jnp.concatenate([acc_ref[n, :hd] / acc_ref[n, hd:hd + 1] for n in range(ATT_KV_HEADS)], axis=0)
    for g in range(ATT_GROUP):
        og = ot[:, g * qb:(g + 1) * qb].T
        for n in range(ATT_KV_HEADS):
            h = n * ATT_GROUP + g
            o_ref[:, h * ATT_HEAD_DIM:(h + 1) * ATT_HEAD_DIM] = (
                og[:, n * ATT_HEAD_DIM:(n + 1) * ATT_HEAD_DIM].astype(o_ref.dtype))


def _dsa_prompt(qt, qit, wit, kb, vbt, kib, b, n_sel):
    t = kb.shape[1]
    ck = min(512, t)
    nqb = t // Q_BLOCK
    idx_bits = max(1, math.ceil(math.log2(t))) + 1
    colblk = lambda n: pl.BlockSpec((n, Q_BLOCK), lambda bi, i: (0, bi * nqb + i))
    res = lambda n: pl.BlockSpec((1, t, n), lambda bi, i: (bi, 0, 0))
    gq = ATT_GROUP * Q_BLOCK
    return pl.pallas_call(
        functools.partial(_dsa_prompt_kernel, ck=ck, n_sel=n_sel, idx_bits=idx_bits),
        grid=(b, nqb),
        in_specs=[colblk(ATT_Q_DIM), colblk(IDX_HEADS * IDX_DIM), colblk(IDX_HEADS),
                  res(ATT_KV_DIM), pl.BlockSpec((ATT_KV_DIM, t), lambda bi, i: (0, bi)), res(IDX_DIM)],
        out_specs=pl.BlockSpec((Q_BLOCK, ATT_Q_DIM), lambda bi, i: (bi * nqb + i, 0)),
        out_shape=jax.ShapeDtypeStruct((b * t, ATT_Q_DIM), BF16),
        scratch_shapes=[pltpu.VMEM((t, Q_BLOCK), I32),
                        pltpu.VMEM((ATT_KV_HEADS, 1, gq), F32),
                        pltpu.VMEM((ATT_KV_HEADS, ATT_HEAD_DIM + BF16_ROWS, gq), F32),
                        pltpu.VMEM((ck, Q_BLOCK), F32), pltpu.VMEM((2, ck, gq), F32),
                        pltpu.VMEM((ck, gq), BF16)],
        compiler_params=_cparams(("parallel", "arbitrary"), 56),
        name="dsa_prompt",
    )(qt, qit, wit, kb, vbt, kib)


def _outproj_ln_kernel(o_ref, h_ref, w_ref, g_ref, b_ref, out_ref):
    y = jnp.dot(o_ref[...], w_ref[...], preferred_element_type=F32)
    out_ref[...] = _layer_norm(DEEPNORM_ALPHA * h_ref[...] + y, g_ref[...], b_ref[...])


def _outproj_ln(o2d, h2d, w_out, g, b):
    m, kdim = o2d.shape
    tm = min(m, 512)
    row = lambda n: pl.BlockSpec((tm, n), lambda i: (i, 0))
    return pl.pallas_call(
        _outproj_ln_kernel,
        grid=(m // tm,),
        in_specs=[row(kdim), row(D_MODEL), pl.BlockSpec((kdim, D_MODEL), lambda i: (0, 0)),
                  pl.BlockSpec((1, D_MODEL), lambda i: (0, 0)), pl.BlockSpec((1, D_MODEL), lambda i: (0, 0))],
        out_specs=row(D_MODEL),
        out_shape=jax.ShapeDtypeStruct((m, D_MODEL), F32),
        compiler_params=_cparams(("parallel",)),
        name="outproj_ln",
    )(o2d, h2d, w_out.astype(BF16), g.reshape(1, -1), b.reshape(1, -1))


def _dsa_sample_kernel(pt_ref, qbd_ref, qi_ref, wi_ref, kn_ref, vn_ref, kin_ref,
                       cki_hbm, ck_hbm, cv_hbm, o_ref,
                       ki_buf, k_buf, v_buf, s_ref, keys_ref, cut_ref, sem,
                       *, n_pages, t_new, n_sel, idx_bits, layer_off, lchunk):
    b = pl.program_id(0)
    past = n_pages * PAGE_SIZE
    ltot = past + LANES
    rows = keys_ref.shape[0]

    def page_copy(hbm, buf, p, s):
        page = pt_ref[b, p] + layer_off
        return pltpu.make_async_copy(hbm.at[page],
                                     buf.at[:, pl.ds(pl.multiple_of(p * PAGE_SIZE, PAGE_SIZE), PAGE_SIZE)],
                                     sem.at[s])

    def start_all(p, c):
        page_copy(cki_hbm, ki_buf, p, 0).start()
        page_copy(ck_hbm, k_buf, p, 1).start()
        page_copy(cv_hbm, v_buf, p, 2).start()
        return c
    lax.fori_loop(0, n_pages, start_all, 0)

    def wait_all(hbm, buf, s):
        def body(p, c):
            page_copy(hbm, buf, p, s).wait()
            return c
        lax.fori_loop(0, n_pages, body, 0)

    qi = qi_ref[0]
    wcol = wi_ref[0]
    wait_all(cki_hbm, ki_buf, 0)

    def idx_scores(kic, dims=NN_DIMS):
        d = lax.dot_general(qi, kic, dims, preferred_element_type=F32)
        d = jnp.maximum(d, 0.0) * wcol
        return jnp.sum(d.reshape(t_new, IDX_HEADS, d.shape[1]), axis=1)

    keys_ref[...] = jnp.full(keys_ref.shape, INT_MIN, I32)
    for c in range(past // lchunk):
        sc = idx_scores(ki_buf[:, c * lchunk:(c + 1) * lchunk].astype(BF16))
        keys_ref[0:t_new, c * lchunk:(c + 1) * lchunk] = _sort_key(sc)
    scn = idx_scores(kin_ref[0], NT_DIMS)
    tpos = lax.broadcasted_iota(I32, (t_new, LANES), 0)
    lpos = lax.broadcasted_iota(I32, (t_new, LANES), 1)
    keys_ref[0:t_new, past:ltot] = jnp.where((lpos <= tpos) & (lpos < t_new), _sort_key(scn), INT_MIN)

    sck = max(w for w in range(LANES, 8 * LANES + 1, LANES) if ltot % w == 0)
    theta = _select_threshold(keys_ref, cut_ref, ltot // sck, sck, rows, n_sel, idx_bits)
    key = keys_ref[...]
    lane = lax.broadcasted_iota(I32, (rows, ltot), 1)
    sel = ((key > theta) | ((key == theta) & (lane < cut_ref[:, :1]))) & (key != INT_MIN)
    bias = jnp.where(sel, 0.0, NEG_INF)[0:t_new]
    bias = jnp.broadcast_to(bias[:, None, :], (t_new, ATT_HEADS, ltot)).reshape(t_new * ATT_HEADS, ltot)

    qbd = qbd_ref[0]
    wait_all(ck_hbm, k_buf, 1)
    for c in range(past // lchunk):
        kc = k_buf[:, c * lchunk:(c + 1) * lchunk].astype(BF16)
        s_ref[:, c * lchunk:(c + 1) * lchunk] = jnp.dot(qbd, kc, preferred_element_type=F32)
    s_ref[:, past:ltot] = lax.dot_general(qbd, kn_ref[0], NT_DIMS, preferred_element_type=F32)
    s = s_ref[...] + bias
    m = jnp.max(s, axis=1, keepdims=True)
    p = jnp.exp(s - m)
    linv = 1.0 / jnp.sum(p, axis=1, keepdims=True)
    s_ref[...] = p
    wait_all(cv_hbm, v_buf, 2)
    o = jnp.dot(s_ref[:, past:ltot].astype(BF16), vn_ref[0], preferred_element_type=F32)
    for c in range(past // lchunk):
        vc = v_buf[:, c * lchunk:(c + 1) * lchunk].astype(BF16)
        o = o + lax.dot_general(s_ref[:, c * lchunk:(c + 1) * lchunk].astype(BF16), vc, NT_DIMS,
                                preferred_element_type=F32)
    o = o * linv
    nrow = (lax.broadcasted_iota(I32, (t_new * ATT_HEADS, ATT_HEAD_DIM), 0) % ATT_HEADS) // ATT_GROUP
    out = jnp.zeros((t_new * ATT_HEADS, ATT_HEAD_DIM), F32)
    for n in range(ATT_KV_HEADS):
        out = out + jnp.where(nrow == n, o[:, n * ATT_HEAD_DIM:(n + 1) * ATT_HEAD_DIM], 0.0)
    o_ref[0] = out.astype(o_ref.dtype)


def _dsa_sample(page_table, qs, qis, wi, kb, vb, kib, cache_ki, cache_k, cache_v, layer_off, n_sel):
    b, t, _ = qs.shape
    n_pages = page_table.shape[1]
    past = n_pages * PAGE_SIZE
    ltot = past + LANES
    rows = 8
    q5 = qs.reshape(b, t, ATT_KV_HEADS, ATT_GROUP, 1, ATT_HEAD_DIM)
    eye = jnp.eye(ATT_KV_HEADS, dtype=qs.dtype).reshape(1, 1, ATT_KV_HEADS, 1, ATT_KV_HEADS, 1)
    qbd = (q5 * eye).reshape(b, t * ATT_HEADS, ATT_KV_DIM)
    qi = qis.reshape(b, t * IDX_HEADS, IDX_DIM)
    wcol = wi.reshape(b, t * IDX_HEADS, 1)
    padr = lambda a: jnp.pad(a, ((0, 0), (0, LANES - t), (0, 0)))
    kn, vn, kin = padr(kb), padr(vb), padr(kib)
    idx_bits = max(1, math.ceil(math.log2(ltot))) + 1
    lchunk = min(1024, past)
    bl = lambda a: pl.BlockSpec((1,) + a.shape[1:], lambda i, pt: (i, 0, 0))
    anyspec = pl.BlockSpec(memory_space=pl.ANY)
    grid_spec = pltpu.PrefetchScalarGridSpec(
        num_scalar_prefetch=1,
        grid=(b,),
        in_specs=[bl(qbd), bl(qi), bl(wcol), bl(kn), bl(vn), bl(kin), anyspec, anyspec, anyspec],
        out_specs=pl.BlockSpec((1, t * ATT_HEADS, ATT_HEAD_DIM), lambda i, pt: (i, 0, 0)),
        scratch_shapes=[pltpu.VMEM((IDX_DIM, past), F32), pltpu.VMEM((ATT_KV_DIM, past), F32),
                        pltpu.VMEM((ATT_KV_DIM, past), F32), pltpu.VMEM((t * ATT_HEADS, ltot), F32),
                        pltpu.VMEM((rows, ltot), I32), pltpu.VMEM((rows, LANES), I32),
                        pltpu.SemaphoreType.DMA((3,))],
    )
    o = pl.pallas_call(
        functools.partial(_dsa_sample_kernel, n_pages=n_pages, t_new=t, n_sel=n_sel, idx_bits=idx_bits,
                          layer_off=layer_off, lchunk=lchunk),
        grid_spec=grid_spec,
        out_shape=jax.ShapeDtypeStruct((b, t * ATT_HEADS, ATT_HEAD_DIM), BF16),
        compiler_params=_cparams(("arbitrary",), 56),
        name="dsa_sample",
    )(page_table, qbd, qi, wcol, kn, vn, kin, cache_ki, cache_k, cache_v)
    return o.reshape(b, t, ATT_Q_DIM)


def _gla_proj_kernel(x_ref, wq_ref, wk_ref, wv_ref, wg_ref, wr_ref, wup_ref, bg_ref,
                     q_ref, k_ref, v_ref, la_ref, r_ref):
    xb = x_ref[...].astype(BF16)
    q_ref[...] = jnp.dot(xb, wq_ref[...], preferred_element_type=F32) * (GLA_DK ** -0.5)
    k_ref[...] = jnp.dot(xb, wk_ref[...], preferred_element_type=F32)
    v_ref[...] = jnp.dot(xb, wv_ref[...], preferred_element_type=F32)
    r_ref[...] = jnp.dot(xb, wr_ref[...], preferred_element_type=F32)
    gd = jnp.dot(xb, wg_ref[...], preferred_element_type=F32)
    glogit = jnp.dot(gd.astype(BF16), wup_ref[...], preferred_element_type=F32) + bg_ref[...]
    la_ref[...] = jax.nn.log_sigmoid(glogit) / GLA_TAU


def _gla_proj(x2d, w_in, w_gate_up, b_gate):
    m = x2d.shape[0]
    tm = min(m, 512)
    dq = GLA_HEADS * GLA_DK
    dv = GLA_HEADS * GLA_DV
    wq = w_in[:, :dq].astype(BF16)
    wk = w_in[:, dq:2 * dq].astype(BF16)
    wv = w_in[:, 2 * dq:2 * dq + dv].astype(BF16)
    wg = jnp.pad(w_in[:, 2 * dq + dv:2 * dq + dv + GLA_GATE_RANK], ((0, 0), (0, LANES - GLA_GATE_RANK))).astype(BF16)
    wr = w_in[:, 2 * dq + dv + GLA_GATE_RANK:].astype(BF16)
    wup = jnp.pad(w_gate_up, ((0, LANES - GLA_GATE_RANK), (0, 0))).astype(BF16)
    row = lambda n: pl.BlockSpec((tm, n), lambda i: (i, 0))
    full = lambda a: pl.BlockSpec(a.shape, lambda i: (0, 0))
    bg = b_gate.reshape(1, -1)
    outs = [dq, dq, dv, dq, dv]
    return pl.pallas_call(
        _gla_proj_kernel,
        grid=(m // tm,),
        in_specs=[row(D_MODEL), full(wq), full(wk), full(wv), full(wg), full(wr), full(wup), full(bg)],
        out_specs=[row(n) for n in outs],
        out_shape=[jax.ShapeDtypeStruct((m, n), F32) for n in outs],
        compiler_params=_cparams(("parallel",)),
        name="gla_proj",
    )(x2d, wq, wk, wv, wg, wr, wup, bg)


def _gla_chunk(q, k, v, g, st, c):
    sub = GLA_SUB
    n_sub = c // sub
    ri = lax.broadcasted_iota(I32, (c, c), 0)
    ci = lax.broadcasted_iota(I32, (c, c), 1)
    tri = jnp.where(ci <= ri, 1.0, 0.0).astype(F32)
    bc = jnp.dot(tri, g, preferred_element_type=F32, precision=lax.Precision.HIGHEST)
    o = lax.dot_general((q * jnp.exp(bc)).astype(BF16), st.astype(BF16), NT_DIMS, preferred_element_type=F32)
    if n_sub > 1:
        mrow = jnp.concatenate([jnp.zeros((sub, GLA_DK), F32)] +
                               [jnp.broadcast_to(bc[i * sub - 1:i * sub], (sub, GLA_DK)) for i in range(1, n_sub)], axis=0)
        qh = (q * jnp.exp(bc - mrow)).astype(BF16)
        parts = [jnp.zeros((sub, GLA_DV), F32)]
        for i in range(1, n_sub):
            kh = (k[:i * sub] * jnp.exp(bc[i * sub - 1:i * sub] - bc[:i * sub])).astype(BF16)
            a = lax.dot_general(qh[i * sub:(i + 1) * sub], kh, NT_DIMS, preferred_element_type=F32)
            parts.append(jnp.dot(a.astype(BF16), v[:i * sub].astype(BF16), preferred_element_type=F32))
        o = o + jnp.concatenate(parts, axis=0)
    rsub = lax.broadcasted_iota(I32, (c, 1), 0) % sub
    for dlt in range(sub):
        ks = k if dlt == 0 else pltpu.roll(k, dlt, 0)
        bs = bc if dlt == 0 else pltpu.roll(bc, dlt, 0)
        vs = v if dlt == 0 else pltpu.roll(v, dlt, 0)
        w = jnp.sum(q * ks * jnp.exp(jnp.minimum(bc - bs, 0.0)), axis=1, keepdims=True)
        o = o + jnp.where(rsub >= dlt, w, 0.0) * vs
    bl = bc[c - 1:c]
    kt = (k * jnp.exp(bl - bc)).astype(BF16)
    st_new = st * jnp.exp(bl) + lax.dot_general(v.astype(BF16), kt, TN_DIMS, preferred_element_type=F32)
    return o, st_new


def _gla_kernel(q_ref, k_ref, v_ref, g_ref, r_ref, ng_ref, s0_ref, o_ref, sf_ref, st_ref, *, c, n_c):
    tb = pl.program_id(2)

    @pl.when(tb == 0)
    def _():
        st_ref[...] = s0_ref[0, 0].T

    st = st_ref[...]
    for ci in range(n_c):
        sl = slice(ci * c, (ci + 1) * c)
        o, st = _gla_chunk(q_ref[0, sl, :], k_ref[0, sl, :], v_ref[0, sl, :], g_ref[0, sl, :], st, c)
        o = o * lax.rsqrt(jnp.mean(o * o, axis=-1, keepdims=True) + LN_EPS) * ng_ref[...]
        o_ref[0, sl, :] = (o * jax.nn.silu(r_ref[0, sl, :])).astype(o_ref.dtype)
    st_ref[...] = st

    @pl.when(tb == pl.num_programs(2) - 1)
    def _():
        sf_ref[0, 0] = st.T


def _gla(q, k, v, la, r, norm_g, s0):
    b, t, _ = q.shape
    c = math.gcd(t, GLA_CHUNK)
    tblk = math.gcd(t, 4 * GLA_CHUNK)
    n_c = tblk // c
    kspec = pl.BlockSpec((1, tblk, GLA_DK), lambda bi, h, i: (bi, i, h))
    vspec = pl.BlockSpec((1, tblk, GLA_DV), lambda bi, h, i: (bi, i, h))
    sspec = pl.BlockSpec((1, 1, GLA_DK, GLA_DV), lambda bi, h, i: (bi, h, 0, 0))
    return pl.pallas_call(
        functools.partial(_gla_kernel, c=c, n_c=n_c),
        grid=(b, GLA_HEADS, t // tblk),
        in_specs=[kspec, kspec, vspec, kspec, vspec, pl.BlockSpec((1, GLA_DV), lambda bi, h, i: (0, 0)), sspec],
        out_specs=[vspec, sspec],
        out_shape=[jax.ShapeDtypeStruct((b, t, GLA_HEADS * GLA_DV), BF16),
                   jax.ShapeDtypeStruct((b, GLA_HEADS, GLA_DK, GLA_DV), F32)],
        scratch_shapes=[pltpu.VMEM((GLA_DV, GLA_DK), F32)],
        compiler_params=_cparams(("parallel", "parallel", "arbitrary")),
        name="gla_scan",
    )(q, k, v, la, r, norm_g.reshape(1, -1), s0)


def _gla_mixer(x, s0, w_in, w_gate_up, b_gate, norm_g):
    b, t, _ = x.shape
    q, k, v, la, r = _gla_proj(x.reshape(b * t, D_MODEL), w_in, w_gate_up, b_gate)
    tp = -(-t // GLA_SUB) * GLA_SUB
    r3 = lambda a: jnp.pad(a.reshape(b, t, -1), ((0, 0), (0, tp - t), (0, 0)))
    o, sf = _gla(r3(q), r3(k), r3(v), r3(la), r3(r), norm_g, s0)
    return o[:, :t], sf


def _top_values(s, k):
    vals = []
    cur = s
    for _ in range(k):
        mx = jnp.max(cur, axis=0, keepdims=True)
        vals.append(mx)
        cur = jnp.where(cur == mx, NEG_INF, cur)
    return vals


def _top_values_ranked(s, k):
    vals = []
    cur = s
    rank = jnp.full(s.shape, float(k), F32)
    for i in range(k):
        mx = jnp.max(cur, axis=0, keepdims=True)
        vals.append(mx)
        top = cur == mx
        rank = jnp.where(top, float(i), rank)
        cur = jnp.where(top, NEG_INF, cur)
    return vals, rank


def _peer_select_kernel(x_ref, wq_ref, k1_ref, k2_ref, c1_ref, p_ref, r2_ref, e_ref):
    xb = x_ref[...].astype(BF16)
    q = jnp.dot(xb, wq_ref[...], preferred_element_type=F32).astype(BF16)
    tt = q.shape[0]
    k1 = k1_ref[...]
    k2 = k2_ref[...]
    sub8 = lax.broadcasted_iota(I32, (8, tt), 0)
    for h in range(PEER_HEADS):
        s1 = lax.dot_general(k1, q[:, h * PEER_D_KEY:h * PEER_D_KEY + PEER_HALF], NT_DIMS,
                             preferred_element_type=F32)
        s2 = lax.dot_general(k2, q[:, h * PEER_D_KEY + PEER_HALF:(h + 1) * PEER_D_KEY], NT_DIMS,
                             preferred_element_type=F32)
        t1, rank1 = _top_values_ranked(s1, PEER_TOPK)
        t2, rank2 = _top_values_ranked(s2, PEER_TOPK)
        t2a = jnp.concatenate(t2, axis=0)
        cands = [t1[0] + t2a]
        for a in range(1, PEER_TOPK):
            nb = PEER_TOPK // (a + 1)
            cands.append(jnp.where(sub8 < nb, t1[a] + t2a[:8], NEG_INF))
        best = _top_values(jnp.concatenate(cands, axis=0), PEER_TOPK)
        theta = best[PEER_TOPK - 1]
        z = best[0] * 0.0
        for bv in best:
            z = z + jnp.exp(bv - best[0])
        count1 = jnp.zeros(s1.shape, F32)
        for a in range(PEER_TOPK):
            cnt_a = jnp.sum(jnp.where(t1[a] + t2a >= theta, 1.0, 0.0), axis=0, keepdims=True)
            count1 = jnp.where(rank1 == float(a), cnt_a, count1)
        gate1 = jnp.exp(s1 - t1[0]) * (0.5 / z)
        gate2 = jnp.exp(s2 - t2[0])
        for lg in range(tt // LANES):
            ls = slice(lg * LANES, (lg + 1) * LANES)
            c1_ref[lg, h] = count1[:, ls]
            p_ref[lg, h] = gate1[:, ls]
            r2_ref[lg, h] = rank2[:, ls]
            e_ref[lg, h] = gate2[:, ls]


def _peer_select(x2d, w_q, keys1, keys2):
    n = x2d.shape[0]
    tt = min(n, 256)
    wq = w_q.astype(BF16)
    k1 = keys1.astype(BF16)
    k2 = keys2.astype(BF16)
    full = lambda a: pl.BlockSpec(a.shape, lambda i: (0,) * a.ndim)
    hspec = pl.BlockSpec((tt // LANES, PEER_HEADS, PEER_N_KEYS, LANES), lambda i: (i, 0, 0, 0))
    big = lambda dt: jax.ShapeDtypeStruct((n // LANES, PEER_HEADS, PEER_N_KEYS, LANES), dt)
    return pl.pallas_call(
        _peer_select_kernel,
        grid=(n // tt,),
        in_specs=[pl.BlockSpec((tt, D_MODEL), lambda i: (i, 0)), full(wq), full(k1), full(k2)],
        out_specs=[hspec, hspec, hspec, hspec],
        out_shape=[big(F32), big(F32), big(F32), big(F32)],
        compiler_params=_cparams(("parallel",)),
        name="peer_select",
    )(x2d, wq, k1, k2)


def _peer_mix_kernel(h_ref, c1_ref, p_ref, r2_ref, e_ref, u_ref, vt_ref, g_ref, bb_ref, out_ref,
                     xb_ref, ht_ref, wt_ref, yt_ref, *, n_i1):
    eb = pl.program_id(1)
    tt = h_ref.shape[0]

    @pl.when(eb == 0)
    def _():
        xb_ref[...] = h_ref[...].T.astype(BF16)
        yt_ref[...] = jnp.zeros(yt_ref.shape, F32)

    ht_ref[...] = jnp.dot(u_ref[...], xb_ref[...], preferred_element_type=F32)

    i1_base = pl.multiple_of(eb * n_i1, n_i1)

    def gate_rows(cb, pb, ks, lg):
        terms = [jnp.where(r2_ref[lg, h, ks, :] < cb[h], pb[h] * e_ref[lg, h, ks, :], 0.0)
                 for h in range(PEER_HEADS)]
        while len(terms) > 1:
            terms = [terms[i] + terms[i + 1] for i in range(0, len(terms), 2)]
        return terms[0]

    for lg in range(tt // LANES):
        ls = slice(lg * LANES, (lg + 1) * LANES)
        c8 = [c1_ref[lg, h, pl.ds(i1_base, n_i1), :] for h in range(PEER_HEADS)]
        p8 = [p_ref[lg, h, pl.ds(i1_base, n_i1), :] for h in range(PEER_HEADS)]
        for j in range(n_i1):
            cb = [jnp.broadcast_to(c8[h][j:j + 1], (8, LANES)) for h in range(PEER_HEADS)]
            pb = [jnp.broadcast_to(p8[h][j:j + 1], (8, LANES)) for h in range(PEER_HEADS)]
            for r in range(PEER_N_KEYS // BF16_ROWS):
                k0 = r * BF16_ROWS
                acc = jnp.concatenate([gate_rows(cb, pb, slice(k0, k0 + 8), lg),
                                       gate_rows(cb, pb, slice(k0 + 8, k0 + 16), lg)], axis=0)
                rs = slice(j * PEER_N_KEYS + k0, j * PEER_N_KEYS + k0 + BF16_ROWS)
                x = ht_ref[rs, ls]
                gelu2 = x * (1.0 + lax.erf(x * (2.0 ** -0.5)))
                wt_ref[rs, ls] = (gelu2 * acc).astype(BF16)

    yt_ref[...] += jnp.dot(vt_ref[...], wt_ref[...], preferred_element_type=F32)

    @pl.when(eb == pl.num_programs(1) - 1)
    def _():
        y = yt_ref[...].T
        out_ref[...] = _layer_norm(DEEPNORM_ALPHA * h_ref[...] + y, g_ref[...], bb_ref[...])


def _peer_ln(h2d, w_q, keys1, keys2, u, v, g, b):
    n_real = h2d.shape[0]
    if n_real % LANES:
        h2d = jnp.pad(h2d, ((0, LANES - n_real % LANES), (0, 0)))
        return _peer_ln(h2d, w_q, keys1, keys2, u, v, g, b)[:n_real]
    n = n_real
    c1, p, r2, e = _peer_select(h2d, w_q, keys1, keys2)
    tt = min(n, 512)
    eblk = 1024
    n_e = u.shape[0]
    ub = u.astype(BF16)
    n_blk = n_e // eblk
    vt = jnp.swapaxes(v.reshape(n_blk, eblk, D_MODEL), 1, 2).astype(BF16)
    hspec = pl.BlockSpec((tt // LANES, PEER_HEADS, PEER_N_KEYS, LANES), lambda i, j: (i, 0, 0, 0))
    vec = pl.BlockSpec((1, D_MODEL), lambda i, j: (0, 0))
    return pl.pallas_call(
        functools.partial(_peer_mix_kernel, n_i1=eblk // PEER_N_KEYS),
        grid=(n // tt, n_blk),
        in_specs=[pl.BlockSpec((tt, D_MODEL), lambda i, j: (i, 0)), hspec, hspec, hspec, hspec,
                  pl.BlockSpec((eblk, D_MODEL), lambda i, j: (j, 0)),
                  pl.BlockSpec((None, D_MODEL, eblk), lambda i, j: (j, 0, 0)), vec, vec],
        out_specs=pl.BlockSpec((tt, D_MODEL), lambda i, j: (i, 0)),
        out_shape=jax.ShapeDtypeStruct((n, D_MODEL), F32),
        scratch_shapes=[pltpu.VMEM((D_MODEL, tt), BF16), pltpu.VMEM((eblk, tt), F32),
                        pltpu.VMEM((eblk, tt), BF16), pltpu.VMEM((D_MODEL, tt), F32)],
        compiler_params=_cparams(("parallel", "arbitrary"), 56),
        name="peer_mix",
    )(h2d, c1, p, r2, e, ub, vt, g.reshape(1, -1), b.reshape(1, -1))


def kernel(x_prompt, x_sample, cache_k, cache_v, cache_kidx, state_gla, page_table,
           attn_w_in, attn_w_out, gla_w_in, gla_w_gate_up, gla_b_gate, gla_norm_g, gla_w_out,
           peer_w_q, peer_keys1, peer_keys2, peer_u, peer_v, ln1_g, ln1_b, ln2_g, ln2_b):
    bp, tp, _ = x_prompt.shape
    bs, ts, _ = x_sample.shape
    n_pool = cache_k.shape[1]
    past = page_table.shape[1] * PAGE_SIZE
    ck = jnp.swapaxes(cache_k.reshape(-1, PAGE_SIZE, ATT_KV_DIM), 1, 2)
    cv = jnp.swapaxes(cache_v.reshape(-1, PAGE_SIZE, ATT_KV_DIM), 1, 2)
    cki = jnp.swapaxes(cache_kidx.reshape(-1, PAGE_SIZE, IDX_DIM), 1, 2)
    hp = x_prompt.reshape(bp * tp, D_MODEL)
    hs = x_sample.reshape(bs * ts, D_MODEL)
    kp_l, vp_l, kip_l, sp_l = [], [], [], []
    ks_l, vs_l, kis_l, ss_l = [], [], [], []
    for i in range(DEPTH):
        j = i // 2
        if i % 2 == 0:
            k, v, ki, kb, kib, qt, qit, wit, vbt = _attn_proj_t(hp, attn_w_in[j])
            r3 = lambda a: a.reshape(bp, tp, -1)
            op = _dsa_prompt(qt, qit, wit, r3(kb), vbt, r3(kib), bp, min(TOPK_MAX, tp // 4))
            kp_l.append(k.reshape(bp, tp, ATT_KV_HEADS, ATT_HEAD_DIM))
            vp_l.append(v.reshape(bp, tp, ATT_KV_HEADS, ATT_HEAD_DIM))
            kip_l.append(ki.reshape(bp, tp, IDX_DIM))
            qs, k, v, kb, vb, qis, ki, kib, wi = _attn_proj(hs, attn_w_in[j])
            r3 = lambda a: a.reshape(bs, ts, -1)
            os_ = _dsa_sample(page_table, r3(qs), r3(qis), r3(wi), r3(kb), r3(vb), r3(kib), cki, ck, cv,
                              j * n_pool, min(TOPK_MAX, (past + ts) // 4))
            ks_l.append(k.reshape(bs, ts, ATT_KV_HEADS, ATT_HEAD_DIM))
            vs_l.append(v.reshape(bs, ts, ATT_KV_HEADS, ATT_HEAD_DIM))
            kis_l.append(ki.reshape(bs, ts, IDX_DIM))
            w_out = attn_w_out[j]
        else:
            s0 = jnp.zeros((bp, GLA_HEADS, GLA_DK, GLA_DV), F32)
            op, sp = _gla_mixer(hp.reshape(bp, tp, D_MODEL), s0, gla_w_in[j], gla_w_gate_up[j], gla_b_gate[j],
                                gla_norm_g[j])
            os_, ss = _gla_mixer(hs.reshape(bs, ts, D_MODEL), state_gla[j], gla_w_in[j], gla_w_gate_up[j],
                                 gla_b_gate[j], gla_norm_g[j])
            sp_l.append(sp)
            ss_l.append(ss)
            w_out = gla_w_out[j]
        hp = _outproj_ln(op.reshape(bp * tp, -1), hp, w_out, ln1_g[i], ln1_b[i])
        hs = _outproj_ln(os_.reshape(bs * ts, -1), hs, w_out, ln1_g[i], ln1_b[i])
        pw = (peer_w_q[i], peer_keys1[i], peer_keys2[i], peer_u[i], peer_v[i], ln2_g[i], ln2_b[i])
        hp = _peer_ln(hp, *pw)
        hs = _peer_ln(hs, *pw)
    return (hp.reshape(bp, tp, D_MODEL), hs.reshape(bs, ts, D_MODEL),
            jnp.stack(kp_l), jnp.stack(vp_l), jnp.stack(kip_l), jnp.stack(sp_l),
            jnp.stack(ks_l), jnp.stack(vs_l), jnp.stack(kis_l), jnp.stack(ss_l))
```

```python
import functools
import math

import jax
import jax.numpy as jnp
from jax import lax
from jax.experimental import pallas as pl
from jax.experimental.pallas import tpu as pltpu

F32 = jnp.float32
BF16 = jnp.bfloat16
I32 = jnp.int32

D_MODEL = 1024
DEPTH = 2
PAGE_SIZE = 128
DEEPNORM_ALPHA = (2.0 * DEPTH) ** 0.25
LN_EPS = 1e-5

ATT_HEADS = 16
ATT_KV_HEADS = 4
ATT_HEAD_DIM = 64
ATT_GROUP = ATT_HEADS // ATT_KV_HEADS
IDX_HEADS = 8
IDX_DIM = 64
TOPK_MAX = 256
Q_BLOCK = 128
ATT_Q_DIM = ATT_HEADS * ATT_HEAD_DIM
ATT_KV_DIM = ATT_KV_HEADS * ATT_HEAD_DIM

GLA_HEADS = 4
GLA_DK = D_MODEL // 2 // GLA_HEADS
GLA_DV = D_MODEL // GLA_HEADS
GLA_GATE_RANK = 16
GLA_TAU = 16.0
GLA_CHUNK = 64
GLA_SUB = 16

PEER_HEADS = 8
PEER_N_KEYS = 128
PEER_D_KEY = 256
PEER_HALF = PEER_D_KEY // 2
PEER_TOPK = 16

LANES = 128
BF16_ROWS = 16
LOG2_E = 1.4426950408889634
INT_MIN = -(2 ** 31)
NEG_INF = float("-inf")

NN_DIMS = (((1,), (0,)), ((), ()))
NT_DIMS = (((1,), (1,)), ((), ()))
TN_DIMS = (((0,), (0,)), ((), ()))


def _cparams(sem, vmem_mib=None, flags=None):
    kw = dict(dimension_semantics=sem)
    if vmem_mib is not None:
        kw["vmem_limit_bytes"] = vmem_mib * 1024 * 1024
    if flags:
        kw["flags"] = flags
    return pltpu.CompilerParams(**kw)


def _sort_key(x):
    b = pltpu.bitcast(x + 0.0, I32)
    return b ^ ((b >> 31) & 0x7FFFFFFF)


def _layer_norm(z, g, b):
    mu = jnp.mean(z, axis=-1, keepdims=True)
    zc = z - mu
    var = jnp.mean(zc * zc, axis=-1, keepdims=True)
    return zc * lax.rsqrt(var + LN_EPS) * g + b


def _row_count(keys_ref, n_chunks, ck, rows, pred):
    def body(c, acc):
        base = pl.multiple_of(c * ck, ck)
        return acc + jnp.where(pred(keys_ref[:, pl.ds(base, ck)], base), 1.0, 0.0)
    acc = lax.fori_loop(0, n_chunks, body, jnp.zeros((rows, ck), F32))
    return jnp.sum(acc, axis=1, keepdims=True)


def _select_threshold(keys_ref, cut_ref, n_chunks, ck, rows, kth, idx_bits):
    kth_f = float(kth)

    def count_ge(t):
        tb = jnp.broadcast_to(t, (rows, ck))
        return _row_count(keys_ref, n_chunks, ck, rows, lambda blk, base: blk >= tb)

    theta = jnp.full((rows, 1), INT_MIN, I32)
    zero = jnp.zeros((rows, 1), I32)
    theta = jnp.where(count_ge(zero) >= kth_f, zero, theta)

    def bit_body(i, t):
        cand = t | lax.shift_left(jnp.int32(1), jnp.int32(30) - i)
        return jnp.where(count_ge(cand) >= kth_f, cand, t)
    theta = lax.fori_loop(0, 31, bit_body, theta)

    n_gt = count_ge(theta + 1)
    n_ge = count_ge(theta)
    need = kth_f - n_gt
    ambiguous = (n_ge - n_gt > need) & (theta > INT_MIN)
    cut_ref[...] = jnp.full(cut_ref.shape, 2 ** idx_bits, I32)

    @pl.when(jnp.max(jnp.where(ambiguous, 1.0, 0.0)) > 0.5)
    def _():
        thb = jnp.broadcast_to(theta, (rows, ck))
        lane = lax.broadcasted_iota(I32, (rows, ck), 1)

        def ties_before(jc):
            jb = jnp.broadcast_to(jc, (rows, ck))
            return _row_count(keys_ref, n_chunks, ck, rows,
                              lambda blk, base: (blk == thb) & (lane + base < jb))

        def jbit(i, jcur):
            cand = jcur | lax.shift_left(jnp.int32(1), jnp.int32(idx_bits - 1) - i)
            return jnp.where(ties_before(cand) <= need, cand, jcur)
        jfin = lax.fori_loop(0, idx_bits, jbit, jnp.zeros((rows, 1), I32))
        jfin = jnp.where(ambiguous, jfin, 2 ** idx_bits)
        cut_ref[...] = jnp.broadcast_to(jfin, cut_ref.shape)
    return theta


def _attn_proj_kernel(x_ref, wq_ref, wkv_ref, wqi_ref, wkw_ref,
                      q_ref, k_ref, v_ref, kb_ref, vb_ref, qi_ref, ki_ref, kib_ref, wi_ref):
    xb = x_ref[...].astype(BF16)
    q = jnp.dot(xb, wq_ref[...], preferred_element_type=F32)
    q_ref[...] = (q * (ATT_HEAD_DIM ** -0.5)).astype(BF16)
    kv = jnp.dot(xb, wkv_ref[...], preferred_element_type=F32)
    k = kv[:, :ATT_KV_DIM]
    v = kv[:, ATT_KV_DIM:]
    k_ref[...] = k
    v_ref[...] = v
    kb_ref[...] = k.astype(BF16)
    vb_ref[...] = v.astype(BF16)
    qi = jnp.dot(xb, wqi_ref[...], preferred_element_type=F32)
    qi_ref[...] = (qi * (IDX_DIM ** -0.5)).astype(BF16)
    kw = jnp.dot(xb, wkw_ref[...], preferred_element_type=F32)
    ki = kw[:, :IDX_DIM]
    ki_ref[...] = ki
    kib_ref[...] = ki.astype(BF16)
    wi_ref[...] = kw[:, IDX_DIM:IDX_DIM + IDX_HEADS] * (IDX_HEADS ** -0.5)


def _attn_weights(w_in):
    s0 = ATT_Q_DIM
    s2 = s0 + 2 * ATT_KV_DIM
    s3 = s2 + IDX_HEADS * IDX_DIM
    wkw = jnp.pad(w_in[:, s3:], ((0, 0), (0, LANES - (w_in.shape[1] - s3))))
    return (w_in[:, :s0].astype(BF16), w_in[:, s0:s2].astype(BF16), w_in[:, s2:s3].astype(BF16),
            wkw.astype(BF16))


def _attn_proj(x2d, w_in):
    m = x2d.shape[0]
    tm = min(m, 512)
    wq, wkv, wqi, wkw = _attn_weights(w_in)
    row = lambda n: pl.BlockSpec((tm, n), lambda i: (i, 0))
    full = lambda a: pl.BlockSpec(a.shape, lambda i: (0, 0))
    outs = [(ATT_Q_DIM, BF16), (ATT_KV_DIM, F32), (ATT_KV_DIM, F32), (ATT_KV_DIM, BF16),
            (ATT_KV_DIM, BF16), (IDX_HEADS * IDX_DIM, BF16), (IDX_DIM, F32), (IDX_DIM, BF16),
            (IDX_HEADS, F32)]
    return pl.pallas_call(
        _attn_proj_kernel,
        grid=(m // tm,),
        in_specs=[row(D_MODEL), full(wq), full(wkv), full(wqi), full(wkw)],
        out_specs=[row(n) for n, _ in outs],
        out_shape=[jax.ShapeDtypeStruct((m, n), dt) for n, dt in outs],
        compiler_params=_cparams(("parallel",)),
        name="attn_proj",
    )(x2d, wq, wkv, wqi, wkw)


def _attn_proj_t_kernel(x_ref, wq_ref, wkv_ref, wqi_ref, wkw_ref,
                        k_ref, v_ref, ki_ref, kb_ref, kib_ref, qt_ref, qit_ref, wit_ref, vbt_ref):
    xb = x_ref[...].astype(BF16)
    q = jnp.dot(xb, wq_ref[...], preferred_element_type=F32)
    qt_ref[...] = (q * (ATT_HEAD_DIM ** -0.5 * LOG2_E)).T.astype(BF16)
    kv = jnp.dot(xb, wkv_ref[...], preferred_element_type=F32)
    k = kv[:, :ATT_KV_DIM]
    v = kv[:, ATT_KV_DIM:]
    k_ref[...] = k
    v_ref[...] = v
    kb_ref[...] = k.astype(BF16)
    vbt_ref[...] = v.T.astype(BF16)
    qi = jnp.dot(xb, wqi_ref[...], preferred_element_type=F32)
    qit_ref[...] = (qi * (IDX_DIM ** -0.5)).T.astype(BF16)
    kw = jnp.dot(xb, wkw_ref[...], preferred_element_type=F32)
    ki = kw[:, :IDX_DIM]
    ki_ref[...] = ki
    kib_ref[...] = ki.astype(BF16)
    wit_ref[...] = kw.T[IDX_DIM:IDX_DIM + IDX_HEADS, :] * (IDX_HEADS ** -0.5)


def _attn_proj_t(x2d, w_in):
    m = x2d.shape[0]
    tm = min(m, 512)
    wq, wkv, wqi, wkw = _attn_weights(w_in)
    row = lambda n: pl.BlockSpec((tm, n), lambda i: (i, 0))
    col = lambda n: pl.BlockSpec((n, tm), lambda i: (0, i))
    full = lambda a: pl.BlockSpec(a.shape, lambda i: (0, 0))
    nat = [(ATT_KV_DIM, F32), (ATT_KV_DIM, F32), (IDX_DIM, F32), (ATT_KV_DIM, BF16), (IDX_DIM, BF16)]
    tr = [(ATT_Q_DIM, BF16), (IDX_HEADS * IDX_DIM, BF16), (IDX_HEADS, F32), (ATT_KV_DIM, BF16)]
    return pl.pallas_call(
        _attn_proj_t_kernel,
        grid=(m // tm,),
        in_specs=[row(D_MODEL), full(wq), full(wkv), full(wqi), full(wkw)],
        out_specs=[row(n) for n, _ in nat] + [col(n) for n, _ in tr],
        out_shape=[jax.ShapeDtypeStruct((m, n), dt) for n, dt in nat] +
                  [jax.ShapeDtypeStruct((n, m), dt) for n, dt in tr],
        compiler_params=_cparams(("parallel",)),
        name="attn_proj_t",
    )(x2d, wq, wkv, wqi, wkw)


def _col_count(keys_ref, n_chunks, ck, pred):
    width = keys_ref.shape[1]
    par = math.gcd(ck, 64)

    def body(c, acc):
        base = pl.multiple_of(c * ck, ck)
        hit = pred(keys_ref[pl.ds(base, ck), :], base)
        return acc + jnp.sum(jnp.where(hit, 1.0, 0.0).reshape(ck // par, par, width), axis=0)
    acc = lax.fori_loop(0, n_chunks, body, jnp.zeros((par, width), F32))
    return jnp.sum(acc, axis=0, keepdims=True)


def _select_threshold_cols(keys_ref, n_chunks, ck, kth, idx_bits):
    width = keys_ref.shape[1]
    kth_f = float(kth)

    def count_ge(t):
        return _col_count(keys_ref, n_chunks, ck, lambda blk, base: blk >= t)

    theta = jnp.full((1, width), INT_MIN, I32)
    zero = jnp.zeros((1, width), I32)
    theta = jnp.where(count_ge(zero) >= kth_f, zero, theta)

    def bit_body(i, t):
        cand = t | lax.shift_left(jnp.int32(1), jnp.int32(30) - i)
        return jnp.where(count_ge(cand) >= kth_f, cand, t)
    theta = lax.fori_loop(0, 31, bit_body, theta)

    n_gt = count_ge(theta + 1)
    n_ge = count_ge(theta)
    need = kth_f - n_gt
    ambiguous = (n_ge - n_gt > need) & (theta > INT_MIN)
    no_cut = jnp.full((1, width), 2 ** idx_bits, I32)

    def tie_search():
        row = lax.broadcasted_iota(I32, (ck, width), 0)

        def ties_before(jc):
            return _col_count(keys_ref, n_chunks, ck, lambda blk, base: (blk == theta) & (row + base < jc))

        def jbit(i, jcur):
            cand = jcur | lax.shift_left(jnp.int32(1), jnp.int32(idx_bits - 1) - i)
            return jnp.where(ties_before(cand) <= need, cand, jcur)
        jfin = lax.fori_loop(0, idx_bits, jbit, jnp.zeros((1, width), I32))
        return jnp.where(ambiguous, jfin, no_cut)

    cut = lax.cond(jnp.max(jnp.where(ambiguous, 1.0, 0.0)) > 0.5, tie_search, lambda: no_cut)
    return theta, cut


def _dsa_prompt_kernel(qt_ref, qit_ref, wit_ref, kb_ref, vbt_ref, kib_ref, o_ref,
                       keys_ref, m_ref, acc_ref, bias_ref, s_ref, p_ref, *, ck, n_sel, idx_bits):
    i = pl.program_id(1)
    qb = Q_BLOCK
    n_chunks = ((i + 1) * qb + ck - 1) // ck
    q_pos = i * qb + lax.broadcasted_iota(I32, (ck, qb), 1)
    row = lax.broadcasted_iota(I32, (ck, qb), 0)

    qit = qit_ref[...]
    qi_all = jnp.concatenate([qit[h * IDX_DIM:(h + 1) * IDX_DIM, :] for h in range(IDX_HEADS)], axis=1)
    wit = wit_ref[...]

    def score_body(c, carry):
        base = pl.multiple_of(c * ck, ck)
        d = jnp.dot(kib_ref[0, pl.ds(base, ck), :], qi_all, preferred_element_type=F32)
        acc = jnp.zeros((ck, qb), F32)
        for h in range(IDX_HEADS):
            acc = acc + jnp.maximum(d[:, h * qb:(h + 1) * qb], 0.0) * wit[h:h + 1, :]
        keys_ref[pl.ds(base, ck), :] = jnp.where(row + base <= q_pos, _sort_key(acc), INT_MIN)
        return carry
    lax.fori_loop(0, n_chunks, score_body, 0)

    theta, cut = _select_threshold_cols(keys_ref, n_chunks, ck, n_sel, idx_bits)

    qt = qt_ref[...]
    qn = [jnp.concatenate([qt[(n * ATT_GROUP + g) * ATT_HEAD_DIM:(n * ATT_GROUP + g + 1) * ATT_HEAD_DIM, :]
                           for g in range(ATT_GROUP)], axis=1) for n in range(ATT_KV_HEADS)]
    m_ref[...] = jnp.full(m_ref.shape, NEG_INF, F32)
    acc_ref[...] = jnp.zeros(acc_ref.shape, F32)
    hd = ATT_HEAD_DIM
    ones = jnp.ones((acc_ref.shape[1] - hd, ck), BF16)
    par = math.gcd(ck, 64)

    def scores(kbase, n):
        kc = kb_ref[0, pl.ds(kbase, ck), n * hd:(n + 1) * hd]
        s_ref[n % 2] = jnp.dot(kc, qn[n], preferred_element_type=F32)
    scores(0, 0)

    def attn_body(c, carry):
        base = pl.multiple_of(c * ck, ck)
        key = keys_ref[pl.ds(base, ck), :]
        sel = ((key > theta) | ((key == theta) & (row + base < cut))) & (key != INT_MIN)
        bias_ref[...] = jnp.where(sel, 0.0, NEG_INF)

        for n in range(ATT_KV_HEADS):
            if n + 1 < ATT_KV_HEADS:
                scores(base, n + 1)
            else:
                scores(pl.multiple_of(jnp.minimum(c + 1, n_chunks - 1) * ck, ck), 0)
            vtc = jnp.concatenate([vbt_ref[n * hd:(n + 1) * hd, pl.ds(base, ck)], ones], axis=0)
            sn_ref = s_ref.at[n % 2]

            def masked(j):
                rs = slice(j * par, (j + 1) * par)
                return sn_ref[rs, :] + jnp.concatenate([bias_ref[rs, :]] * ATT_GROUP, axis=1)
            mx = masked(0)
            for j in range(1, ck // par):
                mx = jnp.maximum(mx, masked(j))
            m_prev = m_ref[n]
            m_new = jnp.maximum(m_prev, jnp.max(mx, axis=0, keepdims=True))
            m_safe = jnp.where(m_new == NEG_INF, 0.0, m_new)
            alpha = jnp.exp2(m_prev - m_safe)
            for j in range(ck // par):
                p_ref[j * par:(j + 1) * par, :] = jnp.exp2((masked(j) - m_safe).astype(BF16))
            acc_ref[n] = alpha * acc_ref[n] + jnp.dot(vtc, p_ref[...], preferred_element_type=F32)
            m_ref[n] = m_new
        return carry
    lax.fori_loop(0, n_chunks, attn_body, 0)

    ot = jnp.concatenate([acc_ref[n, :hd] / acc_ref[n, hd:hd + 1] for n in range(ATT_KV_HEADS)], axis=0)
    for g in range(ATT_GROUP):
        og = ot[:, g * qb:(g + 1) * qb].T
        for n in range(ATT_KV_HEADS):
            h = n * ATT_GROUP + g
            o_ref[:, h * ATT_HEAD_DIM:(h + 1) * ATT_HEAD_DIM] = (
                og[:, n * ATT_HEAD_DIM:(n + 1) * ATT_HEAD_DIM].astype(o_ref.dtype))


def _dsa_prompt(qt, qit, wit, kb, vbt, kib, b, n_sel):
    t = kb.shape[1]
    ck = min(512, t)
    nqb = t // Q_BLOCK
    idx_bits = max(1, math.ceil(math.log2(t))) + 1
    colblk = lambda n: pl.BlockSpec((n, Q_BLOCK), lambda bi, i: (0, bi * nqb + i))
    res = lambda n: pl.BlockSpec((1, t, n), lambda bi, i: (bi, 0, 0))
    gq = ATT_GROUP * Q_BLOCK
    return pl.pallas_call(
        functools.partial(_dsa_prompt_kernel, ck=ck, n_sel=n_sel, idx_bits=idx_bits),
        grid=(b, nqb),
        in_specs=[colblk(ATT_Q_DIM), colblk(IDX_HEADS * IDX_DIM), colblk(IDX_HEADS),
                  res(ATT_KV_DIM), pl.BlockSpec((ATT_KV_DIM, t), lambda bi, i: (0, bi)), res(IDX_DIM)],
        out_specs=pl.BlockSpec((Q_BLOCK, ATT_Q_DIM), lambda bi, i: (bi * nqb + i, 0)),
        out_shape=jax.ShapeDtypeStruct((b * t, ATT_Q_DIM), BF16),
        scratch_shapes=[pltpu.VMEM((t, Q_BLOCK), I32),
                        pltpu.VMEM((ATT_KV_HEADS, 1, gq), F32),
                        pltpu.VMEM((ATT_KV_HEADS, ATT_HEAD_DIM + BF16_ROWS, gq), F32),
                        pltpu.VMEM((ck, Q_BLOCK), F32), pltpu.VMEM((2, ck, gq), F32),
                        pltpu.VMEM((ck, gq), BF16)],
        compiler_params=_cparams(("parallel", "arbitrary"), 56),
        name="dsa_prompt",
    )(qt, qit, wit, kb, vbt, kib)


def _outproj_ln_kernel(o_ref, h_ref, w_ref, g_ref, b_ref, out_ref):
    y = jnp.dot(o_ref[...], w_ref[...], preferred_element_type=F32)
    out_ref[...] = _layer_norm(DEEPNORM_ALPHA * h_ref[...] + y, g_ref[...], b_ref[...])


def _outproj_ln(o2d, h2d, w_out, g, b):
    m, kdim = o2d.shape
    tm = min(m, 512)
    row = lambda n: pl.BlockSpec((tm, n), lambda i: (i, 0))
    return pl.pallas_call(
        _outproj_ln_kernel,
        grid=(m // tm,),
        in_specs=[row(kdim), row(D_MODEL), pl.BlockSpec((kdim, D_MODEL), lambda i: (0, 0)),
                  pl.BlockSpec((1, D_MODEL), lambda i: (0, 0)), pl.BlockSpec((1, D_MODEL), lambda i: (0, 0))],
        out_specs=row(D_MODEL),
        out_shape=jax.ShapeDtypeStruct((m, D_MODEL), F32),
        compiler_params=_cparams(("parallel",)),
        name="outproj_ln",
    )(o2d, h2d, w_out.astype(BF16), g.reshape(1, -1), b.reshape(1, -1))


def _dsa_sample_kernel(pt_ref, qbd_ref, qi_ref, wi_ref, kn_ref, vn_ref, kin_ref,
                       cki_hbm, ck_hbm, cv_hbm, o_ref,
                       ki_buf, k_buf, v_buf, s_ref, keys_ref, cut_ref, sem,
                       *, n_pages, t_new, n_sel, idx_bits, layer_off, lchunk):
    b = pl.program_id(0)
    past = n_pages * PAGE_SIZE
    ltot = past + LANES
    rows = keys_ref.shape[0]

    def page_copy(hbm, buf, p, s):
        page = pt_ref[b, p] + layer_off
        return pltpu.make_async_copy(hbm.at[page],
                                     buf.at[:, pl.ds(pl.multiple_of(p * PAGE_SIZE, PAGE_SIZE), PAGE_SIZE)],
                                     sem.at[s])

    def start_all(p, c):
        page_copy(cki_hbm, ki_buf, p, 0).start()
        page_copy(ck_hbm, k_buf, p, 1).start()
        page_copy(cv_hbm, v_buf, p, 2).start()
        return c
    lax.fori_loop(0, n_pages, start_all, 0)

    def wait_all(hbm, buf, s):
        def body(p, c):
            page_copy(hbm, buf, p, s).wait()
            return c
        lax.fori_loop(0, n_pages, body, 0)

    qi = qi_ref[0]
    wcol = wi_ref[0]
    wait_all(cki_hbm, ki_buf, 0)

    def idx_scores(kic, dims=NN_DIMS):
        d = lax.dot_general(qi, kic, dims, preferred_element_type=F32)
        d = jnp.maximum(d, 0.0) * wcol
        return jnp.sum(d.reshape(t_new, IDX_HEADS, d.shape[1]), axis=1)

    keys_ref[...] = jnp.full(keys_ref.shape, INT_MIN, I32)
    for c in range(past // lchunk):
        sc = idx_scores(ki_buf[:, c * lchunk:(c + 1) * lchunk].astype(BF16))
        keys_ref[0:t_new, c * lchunk:(c + 1) * lchunk] = _sort_key(sc)
    scn = idx_scores(kin_ref[0], NT_DIMS)
    tpos = lax.broadcasted_iota(I32, (t_new, LANES), 0)
    lpos = lax.broadcasted_iota(I32, (t_new, LANES), 1)
    keys_ref[0:t_new, past:ltot] = jnp.where((lpos <= tpos) & (lpos < t_new), _sort_key(scn), INT_MIN)

    sck = max(w for w in range(LANES, 8 * LANES + 1, LANES) if ltot % w == 0)
    theta = _select_threshold(keys_ref, cut_ref, ltot // sck, sck, rows, n_sel, idx_bits)
    key = keys_ref[...]
    lane = lax.broadcasted_iota(I32, (rows, ltot), 1)
    sel = ((key > theta) | ((key == theta) & (lane < cut_ref[:, :1]))) & (key != INT_MIN)
    bias = jnp.where(sel, 0.0, NEG_INF)[0:t_new]
    bias = jnp.broadcast_to(bias[:, None, :], (t_new, ATT_HEADS, ltot)).reshape(t_new * ATT_HEADS, ltot)

    qbd = qbd_ref[0]
    wait_all(ck_hbm, k_buf, 1)
    for c in range(past // lchunk):
        kc = k_buf[:, c * lchunk:(c + 1) * lchunk].astype(BF16)
        s_ref[:, c * lchunk:(c + 1) * lchunk] = jnp.dot(qbd, kc, preferred_element_type=F32)
    s_ref[:, past:ltot] = lax.dot_general(qbd, kn_ref[0], NT_DIMS, preferred_element_type=F32)
    s = s_ref[...] + bias
    m = jnp.max(s, axis=1, keepdims=True)
    p = jnp.exp(s - m)
    linv = 1.0 / jnp.sum(p, axis=1, keepdims=True)
    s_ref[...] = p
    wait_all(cv_hbm, v_buf, 2)
    o = jnp.dot(s_ref[:, past:ltot].astype(BF16), vn_ref[0], preferred_element_type=F32)
    for c in range(past // lchunk):
        vc = v_buf[:, c * lchunk:(c + 1) * lchunk].astype(BF16)
        o = o + lax.dot_general(s_ref[:, c * lchunk:(c + 1) * lchunk].astype(BF16), vc, NT_DIMS,
                                preferred_element_type=F32)
    o = o * linv
    nrow = (lax.broadcasted_iota(I32, (t_new * ATT_HEADS, ATT_HEAD_DIM), 0) % ATT_HEADS) // ATT_GROUP
    out = jnp.zeros((t_new * ATT_HEADS, ATT_HEAD_DIM), F32)
    for n in range(ATT_KV_HEADS):
        out = out + jnp.where(nrow == n, o[:, n * ATT_HEAD_DIM:(n + 1) * ATT_HEAD_DIM], 0.0)
    o_ref[0] = out.astype(o_ref.dtype)


def _dsa_sample(page_table, qs, qis, wi, kb, vb, kib, cache_ki, cache_k, cache_v, layer_off, n_sel):
    b, t, _ = qs.shape
    n_pages = page_table.shape[1]
    past = n_pages * PAGE_SIZE
    ltot = past + LANES
    rows = 8
    q5 = qs.reshape(b, t, ATT_KV_HEADS, ATT_GROUP, 1, ATT_HEAD_DIM)
    eye = jnp.eye(ATT_KV_HEADS, dtype=qs.dtype).reshape(1, 1, ATT_KV_HEADS, 1, ATT_KV_HEADS, 1)
    qbd = (q5 * eye).reshape(b, t * ATT_HEADS, ATT_KV_DIM)
    qi = qis.reshape(b, t * IDX_HEADS, IDX_DIM)
    wcol = wi.reshape(b, t * IDX_HEADS, 1)
    padr = lambda a: jnp.pad(a, ((0, 0), (0, LANES - t), (0, 0)))
    kn, vn, kin = padr(kb), padr(vb), padr(kib)
    idx_bits = max(1, math.ceil(math.log2(ltot))) + 1
    lchunk = min(1024, past)
    bl = lambda a: pl.BlockSpec((1,) + a.shape[1:], lambda i, pt: (i, 0, 0))
    anyspec = pl.BlockSpec(memory_space=pl.ANY)
    grid_spec = pltpu.PrefetchScalarGridSpec(
        num_scalar_prefetch=1,
        grid=(b,),
        in_specs=[bl(qbd), bl(qi), bl(wcol), bl(kn), bl(vn), bl(kin), anyspec, anyspec, anyspec],
        out_specs=pl.BlockSpec((1, t * ATT_HEADS, ATT_HEAD_DIM), lambda i, pt: (i, 0, 0)),
        scratch_shapes=[pltpu.VMEM((IDX_DIM, past), F32), pltpu.VMEM((ATT_KV_DIM, past), F32),
                        pltpu.VMEM((ATT_KV_DIM, past), F32), pltpu.VMEM((t * ATT_HEADS, ltot), F32),
                        pltpu.VMEM((rows, ltot), I32), pltpu.VMEM((rows, LANES), I32),
                        pltpu.SemaphoreType.DMA((3,))],
    )
    o = pl.pallas_call(
        functools.partial(_dsa_sample_kernel, n_pages=n_pages, t_new=t, n_sel=n_sel, idx_bits=idx_bits,
                          layer_off=layer_off, lchunk=lchunk),
        grid_spec=grid_spec,
        out_shape=jax.ShapeDtypeStruct((b, t * ATT_HEADS, ATT_HEAD_DIM), BF16),
        compiler_params=_cparams(("arbitrary",), 56),
        name="dsa_sample",
    )(page_table, qbd, qi, wcol, kn, vn, kin, cache_ki, cache_k, cache_v)
    return o.reshape(b, t, ATT_Q_DIM)


def _gla_proj_kernel(x_ref, wq_ref, wk_ref, wv_ref, wg_ref, wr_ref, wup_ref, bg_ref,
                     q_ref, k_ref, v_ref, la_ref, r_ref):
    xb = x_ref[...].astype(BF16)
    q_ref[...] = jnp.dot(xb, wq_ref[...], preferred_element_type=F32) * (GLA_DK ** -0.5)
    k_ref[...] = jnp.dot(xb, wk_ref[...], preferred_element_type=F32)
    v_ref[...] = jnp.dot(xb, wv_ref[...], preferred_element_type=F32)
    r_ref[...] = jnp.dot(xb, wr_ref[...], preferred_element_type=F32)
    gd = jnp.dot(xb, wg_ref[...], preferred_element_type=F32)
    glogit = jnp.dot(gd.astype(BF16), wup_ref[...], preferred_element_type=F32) + bg_ref[...]
    la_ref[...] = jax.nn.log_sigmoid(glogit) / GLA_TAU


def _gla_proj(x2d, w_in, w_gate_up, b_gate):
    m = x2d.shape[0]
    tm = min(m, 512)
    dq = GLA_HEADS * GLA_DK
    dv = GLA_HEADS * GLA_DV
    wq = w_in[:, :dq].astype(BF16)
    wk = w_in[:, dq:2 * dq].astype(BF16)
    wv = w_in[:, 2 * dq:2 * dq + dv].astype(BF16)
    wg = jnp.pad(w_in[:, 2 * dq + dv:2 * dq + dv + GLA_GATE_RANK], ((0, 0), (0, LANES - GLA_GATE_RANK))).astype(BF16)
    wr = w_in[:, 2 * dq + dv + GLA_GATE_RANK:].astype(BF16)
    wup = jnp.pad(w_gate_up, ((0, LANES - GLA_GATE_RANK), (0, 0))).astype(BF16)
    row = lambda n: pl.BlockSpec((tm, n), lambda i: (i, 0))
    full = lambda a: pl.BlockSpec(a.shape, lambda i: (0, 0))
    bg = b_gate.reshape(1, -1)
    outs = [dq, dq, dv, dq, dv]
    return pl.pallas_call(
        _gla_proj_kernel,
        grid=(m // tm,),
        in_specs=[row(D_MODEL), full(wq), full(wk), full(wv), full(wg), full(wr), full(wup), full(bg)],
        out_specs=[row(n) for n in outs],
        out_shape=[jax.ShapeDtypeStruct((m, n), F32) for n in outs],
        compiler_params=_cparams(("parallel",)),
        name="gla_proj",
    )(x2d, wq, wk, wv, wg, wr, wup, bg)


def _gla_chunk(q, k, v, g, st, c):
    sub = GLA_SUB
    n_sub = c // sub
    ri = lax.broadcasted_iota(I32, (c, c), 0)
    ci = lax.broadcasted_iota(I32, (c, c), 1)
    tri = jnp.where(ci <= ri, 1.0, 0.0).astype(F32)
    bc = jnp.dot(tri, g, preferred_element_type=F32, precision=lax.Precision.HIGHEST)
    o = lax.dot_general((q * jnp.exp(bc)).astype(BF16), st.astype(BF16), NT_DIMS, preferred_element_type=F32)
    if n_sub > 1:
        mrow = jnp.concatenate([jnp.zeros((sub, GLA_DK), F32)] +
                               [jnp.broadcast_to(bc[i * sub - 1:i * sub], (sub, GLA_DK)) for i in range(1, n_sub)], axis=0)
        qh = (q * jnp.exp(bc - mrow)).astype(BF16)
        parts = [jnp.zeros((sub, GLA_DV), F32)]
        for i in range(1, n_sub):
            kh = (k[:i * sub] * jnp.exp(bc[i * sub - 1:i * sub] - bc[:i * sub])).astype(BF16)
            a = lax.dot_general(qh[i * sub:(i + 1) * sub], kh, NT_DIMS, preferred_element_type=F32)
            parts.append(jnp.dot(a.astype(BF16), v[:i * sub].astype(BF16), preferred_element_type=F32))
        o = o + jnp.concatenate(parts, axis=0)
    rsub = lax.broadcasted_iota(I32, (c, 1), 0) % sub
    for dlt in range(sub):
        ks = k if dlt == 0 else pltpu.roll(k, dlt, 0)
        bs = bc if dlt == 0 else pltpu.roll(bc, dlt, 0)
        vs = v if dlt == 0 else pltpu.roll(v, dlt, 0)
        w = jnp.sum(q * ks * jnp.exp(jnp.minimum(bc - bs, 0.0)), axis=1, keepdims=True)
        o = o + jnp.where(rsub >= dlt, w, 0.0) * vs
    bl = bc[c - 1:c]
    kt = (k * jnp.exp(bl - bc)).astype(BF16)
    st_new = st * jnp.exp(bl) + lax.dot_general(v.astype(BF16), kt, TN_DIMS, preferred_element_type=F32)
    return o, st_new


def _gla_kernel(q_ref, k_ref, v_ref, g_ref, r_ref, ng_ref, s0_ref, o_ref, sf_ref, st_ref, *, c, n_c):
    tb = pl.program_id(2)

    @pl.when(tb == 0)
    def _():
        st_ref[...] = s0_ref[0, 0].T

    st = st_ref[...]
    for ci in range(n_c):
        sl = slice(ci * c, (ci + 1) * c)
        o, st = _gla_chunk(q_ref[0, sl, :], k_ref[0, sl, :], v_ref[0, sl, :], g_ref[0, sl, :], st, c)
        o = o * lax.rsqrt(jnp.mean(o * o, axis=-1, keepdims=True) + LN_EPS) * ng_ref[...]
        o_ref[0, sl, :] = (o * jax.nn.silu(r_ref[0, sl, :])).astype(o_ref.dtype)
    st_ref[...] = st

    @pl.when(tb == pl.num_programs(2) - 1)
    def _():
        sf_ref[0, 0] = st.T


def _gla(q, k, v, la, r, norm_g, s0):
    b, t, _ = q.shape
    c = math.gcd(t, GLA_CHUNK)
    tblk = math.gcd(t, 4 * GLA_CHUNK)
    n_c = tblk // c
    kspec = pl.BlockSpec((1, tblk, GLA_DK), lambda bi, h, i: (bi, i, h))
    vspec = pl.BlockSpec((1, tblk, GLA_DV), lambda bi, h, i: (bi, i, h))
    sspec = pl.BlockSpec((1, 1, GLA_DK, GLA_DV), lambda bi, h, i: (bi, h, 0, 0))
    return pl.pallas_call(
        functools.partial(_gla_kernel, c=c, n_c=n_c),
        grid=(b, GLA_HEADS, t // tblk),
        in_specs=[kspec, kspec, vspec, kspec, vspec, pl.BlockSpec((1, GLA_DV), lambda bi, h, i: (0, 0)), sspec],
        out_specs=[vspec, sspec],
        out_shape=[jax.ShapeDtypeStruct((b, t, GLA_HEADS * GLA_DV), BF16),
                   jax.ShapeDtypeStruct((b, GLA_HEADS, GLA_DK, GLA_DV), F32)],
        scratch_shapes=[pltpu.VMEM((GLA_DV, GLA_DK), F32)],
        compiler_params=_cparams(("parallel", "parallel", "arbitrary")),
        name="gla_scan",
    )(q, k, v, la, r, norm_g.reshape(1, -1), s0)


def _gla_mixer(x, s0, w_in, w_gate_up, b_gate, norm_g):
    b, t, _ = x.shape
    q, k, v, la, r = _gla_proj(x.reshape(b * t, D_MODEL), w_in, w_gate_up, b_gate)
    tp = -(-t // GLA_SUB) * GLA_SUB
    r3 = lambda a: jnp.pad(a.reshape(b, t, -1), ((0, 0), (0, tp - t), (0, 0)))
    o, sf = _gla(r3(q), r3(k), r3(v), r3(la), r3(r), norm_g, s0)
    return o[:, :t], sf


def _top_values(s, k):
    vals = []
    cur = s
    for _ in range(k):
        mx = jnp.max(cur, axis=0, keepdims=True)
        vals.append(mx)
        cur = jnp.where(cur == mx, NEG_INF, cur)
    return vals


def _top_values_ranked(s, k):
    vals = []
    cur = s
    rank = jnp.full(s.shape, float(k), F32)
    for i in range(k):
        mx = jnp.max(cur, axis=0, keepdims=True)
        vals.append(mx)
        top = cur == mx
        rank = jnp.where(top, float(i), rank)
        cur = jnp.where(top, NEG_INF, cur)
    return vals, rank


def _peer_select_kernel(x_ref, wq_ref, k1_ref, k2_ref, c1_ref, p_ref, r2_ref, e_ref):
    xb = x_ref[...].astype(BF16)
    q = jnp.dot(xb, wq_ref[...], preferred_element_type=F32).astype(BF16)
    tt = q.shape[0]
    k1 = k1_ref[...]
    k2 = k2_ref[...]
    sub8 = lax.broadcasted_iota(I32, (8, tt), 0)
    for h in range(PEER_HEADS):
        s1 = lax.dot_general(k1, q[:, h * PEER_D_KEY:h * PEER_D_KEY + PEER_HALF], NT_DIMS,
                             preferred_element_type=F32)
        s2 = lax.dot_general(k2, q[:, h * PEER_D_KEY + PEER_HALF:(h + 1) * PEER_D_KEY], NT_DIMS,
                             preferred_element_type=F32)
        t1, rank1 = _top_values_ranked(s1, PEER_TOPK)
        t2, rank2 = _top_values_ranked(s2, PEER_TOPK)
        t2a = jnp.concatenate(t2, axis=0)
        cands = [t1[0] + t2a]
        for a in range(1, PEER_TOPK):
            nb = PEER_TOPK // (a + 1)
            cands.append(jnp.where(sub8 < nb, t1[a] + t2a[:8], NEG_INF))
        best = _top_values(jnp.concatenate(cands, axis=0), PEER_TOPK)
        theta = best[PEER_TOPK - 1]
        z = best[0] * 0.0
        for bv in best:
            z = z + jnp.exp(bv - best[0])
        count1 = jnp.zeros(s1.shape, F32)
        for a in range(PEER_TOPK):
            cnt_a = jnp.sum(jnp.where(t1[a] + t2a >= theta, 1.0, 0.0), axis=0, keepdims=True)
            count1 = jnp.where(rank1 == float(a), cnt_a, count1)
        c1_ref[h] = count1
        p_ref[h] = jnp.exp(s1 - t1[0]) * (0.5 / z)
        r2_ref[h] = rank2
        e_ref[h] = jnp.exp(s2 - t2[0])


def _peer_select(x2d, w_q, keys1, keys2):
    n = x2d.shape[0]
    tt = min(n, 256)
    wq = w_q.astype(BF16)
    k1 = keys1.astype(BF16)
    k2 = keys2.astype(BF16)
    full = lambda a: pl.BlockSpec(a.shape, lambda i: (0,) * a.ndim)
    hspec = pl.BlockSpec((PEER_HEADS, PEER_N_KEYS, tt), lambda i: (0, 0, i))
    big = lambda dt: jax.ShapeDtypeStruct((PEER_HEADS, PEER_N_KEYS, n), dt)
    return pl.pallas_call(
        _peer_select_kernel,
        grid=(n // tt,),
        in_specs=[pl.BlockSpec((tt, D_MODEL), lambda i: (i, 0)), full(wq), full(k1), full(k2)],
        out_specs=[hspec, hspec, hspec, hspec],
        out_shape=[big(F32), big(F32), big(F32), big(F32)],
        compiler_params=_cparams(("parallel",)),
        name="peer_select",
    )(x2d, wq, k1, k2)


def _peer_mix_kernel(h_ref, c1_ref, p_ref, r2_ref, e_ref, u_ref, vt_ref, g_ref, bb_ref, out_ref,
                     xb_ref, ht_ref, wt_ref, yt_ref, *, n_i1):
    eb = pl.program_id(1)
    tt = h_ref.shape[0]

    @pl.when(eb == 0)
    def _():
        xb_ref[...] = h_ref[...].T.astype(BF16)
        yt_ref[...] = jnp.zeros(yt_ref.shape, F32)

    ht_ref[...] = jnp.dot(u_ref[...], xb_ref[...], preferred_element_type=F32)

    i1_base = pl.multiple_of(eb * n_i1, n_i1)

    for lg in range(tt // LANES):
        ls = slice(lg * LANES, (lg + 1) * LANES)
        c8 = [c1_ref[h, pl.ds(i1_base, n_i1), ls] for h in range(PEER_HEADS)]
        p8 = [p_ref[h, pl.ds(i1_base, n_i1), ls] for h in range(PEER_HEADS)]
        for j in range(n_i1):
            cb = [jnp.broadcast_to(c8[h][j:j + 1], (BF16_ROWS, LANES)) for h in range(PEER_HEADS)]
            pb = [jnp.broadcast_to(p8[h][j:j + 1], (BF16_ROWS, LANES)) for h in range(PEER_HEADS)]
            for r in range(PEER_N_KEYS // BF16_ROWS):
                ks = slice(r * BF16_ROWS, (r + 1) * BF16_ROWS)
                acc = jnp.zeros((BF16_ROWS, LANES), F32)
                for h in range(PEER_HEADS):
                    acc = acc + jnp.where(r2_ref[h, ks, ls] < cb[h], pb[h] * e_ref[h, ks, ls], 0.0)
                rs = slice(j * PEER_N_KEYS + r * BF16_ROWS, j * PEER_N_KEYS + (r + 1) * BF16_ROWS)
                x = ht_ref[rs, ls]
                gelu2 = x * (1.0 + lax.erf(x * (2.0 ** -0.5)))
                wt_ref[rs, ls] = (gelu2 * acc).astype(BF16)

    yt_ref[...] += jnp.dot(vt_ref[...], wt_ref[...], preferred_element_type=F32)

    @pl.when(eb == pl.num_programs(1) - 1)
    def _():
        y = yt_ref[...].T
        out_ref[...] = _layer_norm(DEEPNORM_ALPHA * h_ref[...] + y, g_ref[...], bb_ref[...])


def _peer_ln(h2d, w_q, keys1, keys2, u, v, g, b):
    n_real = h2d.shape[0]
    if n_real % LANES:
        h2d = jnp.pad(h2d, ((0, LANES - n_real % LANES), (0, 0)))
        return _peer_ln(h2d, w_q, keys1, keys2, u, v, g, b)[:n_real]
    n = n_real
    c1, p, r2, e = _peer_select(h2d, w_q, keys1, keys2)
    tt = min(n, 512)
    eblk = 1024
    n_e = u.shape[0]
    ub = u.astype(BF16)
    n_blk = n_e // eblk
    vt = jnp.swapaxes(v.reshape(n_blk, eblk, D_MODEL), 1, 2).astype(BF16)
    hspec = pl.BlockSpec((PEER_HEADS, PEER_N_KEYS, tt), lambda i, j: (0, 0, i))
    vec = pl.BlockSpec((1, D_MODEL), lambda i, j: (0, 0))
    return pl.pallas_call(
        functools.partial(_peer_mix_kernel, n_i1=eblk // PEER_N_KEYS),
        grid=(n // tt, n_blk),
        in_specs=[pl.BlockSpec((tt, D_MODEL), lambda i, j: (i, 0)), hspec, hspec, hspec, hspec,
                  pl.BlockSpec((eblk, D_MODEL), lambda i, j: (j, 0)),
                  pl.BlockSpec((None, D_MODEL, eblk), lambda i, j: (j, 0, 0)), vec, vec],
        out_specs=pl.BlockSpec((tt, D_MODEL), lambda i, j: (i, 0)),
        out_shape=jax.ShapeDtypeStruct((n, D_MODEL), F32),
        scratch_shapes=[pltpu.VMEM((D_MODEL, tt), BF16), pltpu.VMEM((eblk, tt), F32),
                        pltpu.VMEM((eblk, tt), BF16), pltpu.VMEM((D_MODEL, tt), F32)],
        compiler_params=_cparams(("parallel", "arbitrary"), 56),
        name="peer_mix",
    )(h2d, c1, p, r2, e, ub, vt, g.reshape(1, -1), b.reshape(1, -1))


def kernel(x_prompt, x_sample, cache_k, cache_v, cache_kidx, state_gla, page_table,
           attn_w_in, attn_w_out, gla_w_in, gla_w_gate_up, gla_b_gate, gla_norm_g, gla_w_out,
           peer_w_q, peer_keys1, peer_keys2, peer_u, peer_v, ln1_g, ln1_b, ln2_g, ln2_b):
    bp, tp, _ = x_prompt.shape
    bs, ts, _ = x_sample.shape
    n_pool = cache_k.shape[1]
    past = page_table.shape[1] * PAGE_SIZE
    ck = jnp.swapaxes(cache_k.reshape(-1, PAGE_SIZE, ATT_KV_DIM), 1, 2)
    cv = jnp.swapaxes(cache_v.reshape(-1, PAGE_SIZE, ATT_KV_DIM), 1, 2)
    cki = jnp.swapaxes(cache_kidx.reshape(-1, PAGE_SIZE, IDX_DIM), 1, 2)
    hp = x_prompt.reshape(bp * tp, D_MODEL)
    hs = x_sample.reshape(bs * ts, D_MODEL)
    kp_l, vp_l, kip_l, sp_l = [], [], [], []
    ks_l, vs_l, kis_l, ss_l = [], [], [], []
    for i in range(DEPTH):
        j = i // 2
        if i % 2 == 0:
            k, v, ki, kb, kib, qt, qit, wit, vbt = _attn_proj_t(hp, attn_w_in[j])
            r3 = lambda a: a.reshape(bp, tp, -1)
            op = _dsa_prompt(qt, qit, wit, r3(kb), vbt, r3(kib), bp, min(TOPK_MAX, tp // 4))
            kp_l.append(k.reshape(bp, tp, ATT_KV_HEADS, ATT_HEAD_DIM))
            vp_l.append(v.reshape(bp, tp, ATT_KV_HEADS, ATT_HEAD_DIM))
            kip_l.append(ki.reshape(bp, tp, IDX_DIM))
            qs, k, v, kb, vb, qis, ki, kib, wi = _attn_proj(hs, attn_w_in[j])
            r3 = lambda a: a.reshape(bs, ts, -1)
            os_ = _dsa_sample(page_table, r3(qs), r3(qis), r3(wi), r3(kb), r3(vb), r3(kib), cki, ck, cv,
                              j * n_pool, min(TOPK_MAX, (past + ts) // 4))
            ks_l.append(k.reshape(bs, ts, ATT_KV_HEADS, ATT_HEAD_DIM))
            vs_l.append(v.reshape(bs, ts, ATT_KV_HEADS, ATT_HEAD_DIM))
            kis_l.append(ki.reshape(bs, ts, IDX_DIM))
            w_out = attn_w_out[j]
        else:
            s0 = jnp.zeros((bp, GLA_HEADS, GLA_DK, GLA_DV), F32)
            op, sp = _gla_mixer(hp.reshape(bp, tp, D_MODEL), s0, gla_w_in[j], gla_w_gate_up[j], gla_b_gate[j],
                                gla_norm_g[j])
            os_, ss = _gla_mixer(hs.reshape(bs, ts, D_MODEL), state_gla[j], gla_w_in[j], gla_w_gate_up[j],
                                 gla_b_gate[j], gla_norm_g[j])
            sp_l.append(sp)
            ss_l.append(ss)
            w_out = gla_w_out[j]
        hp = _outproj_ln(op.reshape(bp * tp, -1), hp, w_out, ln1_g[i], ln1_b[i])
        hs = _outproj_ln(os_.reshape(bs * ts, -1), hs, w_out, ln1_g[i], ln1_b[i])
        pw = (peer_w_q[i], peer_keys1[i], peer_keys2[i], peer_u[i], peer_v[i], ln2_g[i], ln2_b[i])
        hp = _peer_ln(hp, *pw)
        hs = _peer_ln(hs, *pw)
    return (hp.reshape(bp, tp, D_MODEL), hs.reshape(bs, ts, D_MODEL),
            jnp.stack(kp_l), jnp.stack(vp_l), jnp.stack(kip_l), jnp.stack(sp_l),
            jnp.stack(ks_l), jnp.stack(vs_l), jnp.stack(kis_l), jnp.stack(ss_l))
```

```python
import functools
import math

import jax
import jax.numpy as jnp
from jax import lax
from jax.experimental import pallas as pl
from jax.experimental.pallas import tpu as pltpu

F32 = jnp.float32
BF16 = jnp.bfloat16
I32 = jnp.int32

D_MODEL = 1024
DEPTH = 2
PAGE_SIZE = 128
DEEPNORM_ALPHA = (2.0 * DEPTH) ** 0.25
LN_EPS = 1e-5

ATT_HEADS = 16
ATT_KV_HEADS = 4
ATT_HEAD_DIM = 64
ATT_GROUP = ATT_HEADS // ATT_KV_HEADS
IDX_HEADS = 8
IDX_DIM = 64
TOPK_MAX = 256
Q_BLOCK = 128
ATT_Q_DIM = ATT_HEADS * ATT_HEAD_DIM
ATT_KV_DIM = ATT_KV_HEADS * ATT_HEAD_DIM

GLA_HEADS = 4
GLA_DK = D_MODEL // 2 // GLA_HEADS
GLA_DV = D_MODEL // GLA_HEADS
GLA_GATE_RANK = 16
GLA_TAU = 16.0
GLA_CHUNK = 64
GLA_SUB = 16

PEER_HEADS = 8
PEER_N_KEYS = 128
PEER_D_KEY = 256
PEER_HALF = PEER_D_KEY // 2
PEER_TOPK = 16

LANES = 128
BF16_ROWS = 16
LOG2_E = 1.4426950408889634
INT_MIN = -(2 ** 31)
NEG_INF = float("-inf")
MASKED = -1e30

NN_DIMS = (((1,), (0,)), ((), ()))
NT_DIMS = (((1,), (1,)), ((), ()))
TN_DIMS = (((0,), (0,)), ((), ()))


def _cparams(sem, vmem_mib=None, flags=None):
    kw = dict(dimension_semantics=sem)
    if vmem_mib is not None:
        kw["vmem_limit_bytes"] = vmem_mib * 1024 * 1024
    if flags:
        kw["flags"] = flags
    return pltpu.CompilerParams(**kw)


def _sort_key(x):
    b = pltpu.bitcast(x + 0.0, I32)
    return b ^ ((b >> 31) & 0x7FFFFFFF)


def _layer_norm(z, g, b):
    mu = jnp.mean(z, axis=-1, keepdims=True)
    zc = z - mu
    var = jnp.mean(zc * zc, axis=-1, keepdims=True)
    return zc * lax.rsqrt(var + LN_EPS) * g + b


def _row_count(keys_ref, n_chunks, ck, rows, pred):
    def body(c, acc):
        base = pl.multiple_of(c * ck, ck)
        return acc + jnp.where(pred(keys_ref[:, pl.ds(base, ck)], base), 1.0, 0.0)
    acc = lax.fori_loop(0, n_chunks, body, jnp.zeros((rows, ck), F32))
    return jnp.sum(acc, axis=1, keepdims=True)


def _select_threshold(keys_ref, cut_ref, n_chunks, ck, rows, kth, idx_bits):
    kth_f = float(kth)

    def count_ge(t):
        tb = jnp.broadcast_to(t, (rows, ck))
        return _row_count(keys_ref, n_chunks, ck, rows, lambda blk, base: blk >= tb)

    theta = jnp.full((rows, 1), INT_MIN, I32)
    zero = jnp.zeros((rows, 1), I32)
    theta = jnp.where(count_ge(zero) >= kth_f, zero, theta)

    def bit_body(i, t):
        cand = t | lax.shift_left(jnp.int32(1), jnp.int32(30) - i)
        return jnp.where(count_ge(cand) >= kth_f, cand, t)
    theta = lax.fori_loop(0, 31, bit_body, theta)

    n_gt = count_ge(theta + 1)
    n_ge = count_ge(theta)
    need = kth_f - n_gt
    ambiguous = (n_ge - n_gt > need) & (theta > INT_MIN)
    cut_ref[...] = jnp.full(cut_ref.shape, 2 ** idx_bits, I32)

    @pl.when(jnp.max(jnp.where(ambiguous, 1.0, 0.0)) > 0.5)
    def _():
        thb = jnp.broadcast_to(theta, (rows, ck))
        lane = lax.broadcasted_iota(I32, (rows, ck), 1)

        def ties_before(jc):
            jb = jnp.broadcast_to(jc, (rows, ck))
            return _row_count(keys_ref, n_chunks, ck, rows,
                              lambda blk, base: (blk == thb) & (lane + base < jb))

        def jbit(i, jcur):
            cand = jcur | lax.shift_left(jnp.int32(1), jnp.int32(idx_bits - 1) - i)
            return jnp.where(ties_before(cand) <= need, cand, jcur)
        jfin = lax.fori_loop(0, idx_bits, jbit, jnp.zeros((rows, 1), I32))
        jfin = jnp.where(ambiguous, jfin, 2 ** idx_bits)
        cut_ref[...] = jnp.broadcast_to(jfin, cut_ref.shape)
    return theta


def _attn_proj_kernel(x_ref, wq_ref, wkv_ref, wqi_ref, wkw_ref,
                      q_ref, k_ref, v_ref, kb_ref, vb_ref, qi_ref, ki_ref, kib_ref, wi_ref):
    xb = x_ref[...].astype(BF16)
    q = jnp.dot(xb, wq_ref[...], preferred_element_type=F32)
    q_ref[...] = (q * (ATT_HEAD_DIM ** -0.5)).astype(BF16)
    kv = jnp.dot(xb, wkv_ref[...], preferred_element_type=F32)
    k = kv[:, :ATT_KV_DIM]
    v = kv[:, ATT_KV_DIM:]
    k_ref[...] = k
    v_ref[...] = v
    kb_ref[...] = k.astype(BF16)
    vb_ref[...] = v.astype(BF16)
    qi = jnp.dot(xb, wqi_ref[...], preferred_element_type=F32)
    qi_ref[...] = (qi * (IDX_DIM ** -0.5)).astype(BF16)
    kw = jnp.dot(xb, wkw_ref[...], preferred_element_type=F32)
    ki = kw[:, :IDX_DIM]
    ki_ref[...] = ki
    kib_ref[...] = ki.astype(BF16)
    wi_ref[...] = kw[:, IDX_DIM:IDX_DIM + IDX_HEADS] * (IDX_HEADS ** -0.5)


def _attn_weights(w_in):
    s0 = ATT_Q_DIM
    s2 = s0 + 2 * ATT_KV_DIM
    s3 = s2 + IDX_HEADS * IDX_DIM
    wkw = jnp.pad(w_in[:, s3:], ((0, 0), (0, LANES - (w_in.shape[1] - s3))))
    return (w_in[:, :s0].astype(BF16), w_in[:, s0:s2].astype(BF16), w_in[:, s2:s3].astype(BF16),
            wkw.astype(BF16))


def _attn_proj(x2d, w_in):
    m = x2d.shape[0]
    tm = min(m, 512)
    wq, wkv, wqi, wkw = _attn_weights(w_in)
    row = lambda n: pl.BlockSpec((tm, n), lambda i: (i, 0))
    full = lambda a: pl.BlockSpec(a.shape, lambda i: (0, 0))
    outs = [(ATT_Q_DIM, BF16), (ATT_KV_DIM, F32), (ATT_KV_DIM, F32), (ATT_KV_DIM, BF16),
            (ATT_KV_DIM, BF16), (IDX_HEADS * IDX_DIM, BF16), (IDX_DIM, F32), (IDX_DIM, BF16),
            (IDX_HEADS, F32)]
    return pl.pallas_call(
        _attn_proj_kernel,
        grid=(m // tm,),
        in_specs=[row(D_MODEL), full(wq), full(wkv), full(wqi), full(wkw)],
        out_specs=[row(n) for n, _ in outs],
        out_shape=[jax.ShapeDtypeStruct((m, n), dt) for n, dt in outs],
        compiler_params=_cparams(("parallel",)),
        name="attn_proj",
    )(x2d, wq, wkv, wqi, wkw)


def _attn_proj_t_kernel(x_ref, wq_ref, wkv_ref, wqi_ref, wkw_ref,
                        k_ref, v_ref, ki_ref, kb_ref, kib_ref, qt_ref, qit_ref, wit_ref, vbt_ref):
    xb = x_ref[...].astype(BF16)
    q = jnp.dot(xb, wq_ref[...], preferred_element_type=F32)
    qt_ref[...] = (q * (ATT_HEAD_DIM ** -0.5 * LOG2_E)).T.astype(BF16)
    kv = jnp.dot(xb, wkv_ref[...], preferred_element_type=F32)
    k = kv[:, :ATT_KV_DIM]
    v = kv[:, ATT_KV_DIM:]
    k_ref[...] = k
    v_ref[...] = v
    kb_ref[...] = k.astype(BF16)
    vbt_ref[...] = v.T.astype(BF16)
    qi = jnp.dot(xb, wqi_ref[...], preferred_element_type=F32)
    qit_ref[...] = (qi * (IDX_DIM ** -0.5)).T.astype(BF16)
    kw = jnp.dot(xb, wkw_ref[...], preferred_element_type=F32)
    ki = kw[:, :IDX_DIM]
    ki_ref[...] = ki
    kib_ref[...] = ki.astype(BF16)
    wit_ref[...] = kw.T[IDX_DIM:IDX_DIM + IDX_HEADS, :] * (IDX_HEADS ** -0.5)


def _attn_proj_t(x2d, w_in):
    m = x2d.shape[0]
    tm = min(m, 512)
    wq, wkv, wqi, wkw = _attn_weights(w_in)
    row = lambda n: pl.BlockSpec((tm, n), lambda i: (i, 0))
    col = lambda n: pl.BlockSpec((n, tm), lambda i: (0, i))
    full = lambda a: pl.BlockSpec(a.shape, lambda i: (0, 0))
    nat = [(ATT_KV_DIM, F32), (ATT_KV_DIM, F32), (IDX_DIM, F32), (ATT_KV_DIM, BF16), (IDX_DIM, BF16)]
    tr = [(ATT_Q_DIM, BF16), (IDX_HEADS * IDX_DIM, BF16), (IDX_HEADS, F32), (ATT_KV_DIM, BF16)]
    return pl.pallas_call(
        _attn_proj_t_kernel,
        grid=(m // tm,),
        in_specs=[row(D_MODEL), full(wq), full(wkv), full(wqi), full(wkw)],
        out_specs=[row(n) for n, _ in nat] + [col(n) for n, _ in tr],
        out_shape=[jax.ShapeDtypeStruct((m, n), dt) for n, dt in nat] +
                  [jax.ShapeDtypeStruct((n, m), dt) for n, dt in tr],
        compiler_params=_cparams(("parallel",)),
        name="attn_proj_t",
    )(x2d, wq, wkv, wqi, wkw)


def _col_count(keys_ref, n_chunks, ck, pred):
    width = keys_ref.shape[1]
    par = math.gcd(ck, 64)

    def body(c, acc):
        base = pl.multiple_of(c * ck, ck)
        hit = pred(keys_ref[pl.ds(base, ck), :], base)
        return acc + jnp.sum(jnp.where(hit, 1.0, 0.0).reshape(ck // par, par, width), axis=0)
    acc = lax.fori_loop(0, n_chunks, body, jnp.zeros((par, width), F32))
    return jnp.sum(acc, axis=0, keepdims=True)


def _select_threshold_cols(keys_ref, n_chunks, ck, kth, idx_bits):
    width = keys_ref.shape[1]
    kth_f = float(kth)

    def count_ge(t):
        return _col_count(keys_ref, n_chunks, ck, lambda blk, base: blk >= t)

    theta = jnp.full((1, width), INT_MIN, I32)
    zero = jnp.zeros((1, width), I32)
    theta = jnp.where(count_ge(zero) >= kth_f, zero, theta)

    def bit_body(i, t):
        cand = t | lax.shift_left(jnp.int32(1), jnp.int32(30) - i)
        return jnp.where(count_ge(cand) >= kth_f, cand, t)
    theta = lax.fori_loop(0, 31, bit_body, theta)

    n_gt = count_ge(theta + 1)
    n_ge = count_ge(theta)
    need = kth_f - n_gt
    ambiguous = (n_ge - n_gt > need) & (theta > INT_MIN)
    no_cut = jnp.full((1, width), 2 ** idx_bits, I32)

    def tie_search():
        row = lax.broadcasted_iota(I32, (ck, width), 0)

        def ties_before(jc):
            return _col_count(keys_ref, n_chunks, ck, lambda blk, base: (blk == theta) & (row + base < jc))

        def jbit(i, jcur):
            cand = jcur | lax.shift_left(jnp.int32(1), jnp.int32(idx_bits - 1) - i)
            return jnp.where(ties_before(cand) <= need, cand, jcur)
        jfin = lax.fori_loop(0, idx_bits, jbit, jnp.zeros((1, width), I32))
        return jnp.where(ambiguous, jfin, no_cut)

    cut = lax.cond(jnp.max(jnp.where(ambiguous, 1.0, 0.0)) > 0.5, tie_search, lambda: no_cut)
    return theta, cut


def _dsa_prompt_kernel(qt_ref, qit_ref, wit_ref, kb_ref, vbt_ref, kib_ref, o_ref,
                       keys_ref, m_ref, acc_ref, bias_ref, s_ref, p_ref, *, ck, n_sel, idx_bits):
    i = pl.program_id(1)
    qb = Q_BLOCK
    n_chunks = ((i + 1) * qb + ck - 1) // ck
    q_pos = i * qb + lax.broadcasted_iota(I32, (ck, qb), 1)
    row = lax.broadcasted_iota(I32, (ck, qb), 0)

    qit = qit_ref[...]
    qi_all = jnp.concatenate([qit[h * IDX_DIM:(h + 1) * IDX_DIM, :] for h in range(IDX_HEADS)], axis=1)
    wit = wit_ref[...]

    def score_body(c, carry):
        base = pl.multiple_of(c * ck, ck)
        d = jnp.dot(kib_ref[0, pl.ds(base, ck), :], qi_all, preferred_element_type=F32)
        acc = jnp.zeros((ck, qb), F32)
        for h in range(IDX_HEADS):
            acc = acc + jnp.maximum(d[:, h * qb:(h + 1) * qb], 0.0) * wit[h:h + 1, :]
        keys_ref[pl.ds(base, ck), :] = jnp.where(row + base <= q_pos, _sort_key(acc), INT_MIN)
        return carry
    lax.fori_loop(0, n_chunks, score_body, 0)

    theta, cut = _select_threshold_cols(keys_ref, n_chunks, ck, n_sel, idx_bits)

    qt = qt_ref[...]
    qn = [jnp.concatenate([qt[(n * ATT_GROUP + g) * ATT_HEAD_DIM:(n * ATT_GROUP + g + 1) * ATT_HEAD_DIM, :]
                           for g in range(ATT_GROUP)], axis=1) for n in range(ATT_KV_HEADS)]
    m_ref[...] = jnp.full(m_ref.shape, NEG_INF, F32)
    acc_ref[...] = jnp.zeros(acc_ref.shape, F32)
    hd = ATT_HEAD_DIM
    ones = jnp.ones((acc_ref.shape[1] - hd, ck), BF16)
    par = math.gcd(ck, 64)

    n_sub = ck // par
    pair = 2

    eye = (lax.broadcasted_iota(I32, (qb, ATT_GROUP * qb), 1) % qb ==
           lax.broadcasted_iota(I32, (qb, ATT_GROUP * qb), 0))
    eye = jnp.where(eye, 1.0, 0.0).astype(BF16)
    qa = [jnp.concatenate([eye, qn[n]], axis=0) for n in range(ATT_KV_HEADS)]

    def mask_rows(kbase, slot):
        key = keys_ref[pl.ds(kbase, ck), :]
        sel = ((key > theta) | ((key == theta) & (row + kbase < cut))) & (key != INT_MIN)
        bias_ref[slot] = jnp.where(sel, 0.0, MASKED).astype(BF16)

    def scores(kbase, slot, n):
        kc = jnp.concatenate([bias_ref[slot], kb_ref[0, pl.ds(kbase, ck), n * hd:(n + 1) * hd]], axis=1)
        s_ref[n] = jnp.dot(kc, qa[n], preferred_element_type=F32)
    mask_rows(0, 0)
    for n in range(pair):
        scores(0, 0, n)

    def attn_body(c, carry):
        base = pl.multiple_of(c * ck, ck)
        slot = c % 2

        for n0 in range(0, ATT_KV_HEADS, pair):
            heads = range(n0, n0 + pair)
            if n0 + pair < ATT_KV_HEADS:
                for n in heads:
                    scores(base, slot, n + pair)
            else:
                nbase = pl.multiple_of(jnp.minimum(c + 1, n_chunks - 1) * ck, ck)
                mask_rows(nbase, 1 - slot)
                for n in heads:
                    scores(nbase, 1 - slot, n - n0)

            def masked(n, j):
                return s_ref[n, j * par:(j + 1) * par, :]
            mx = {n: masked(n, 0) for n in heads}
            for j in range(1, n_sub):
                for n in heads:
                    mx[n] = jnp.maximum(mx[n], masked(n, j))
            m_safe, alpha, m_new = {}, {}, {}
            for n in heads:
                m_prev = m_ref[n]
                m_new[n] = jnp.maximum(m_prev, jnp.max(mx[n], axis=0, keepdims=True))
                m_safe[n] = jnp.where(m_new[n] == NEG_INF, 0.0, m_new[n])
                alpha[n] = jnp.exp2(m_prev - m_safe[n])
            for j in range(n_sub):
                for n in heads:
                    p_ref[n - n0, j * par:(j + 1) * par, :] = jnp.exp2((masked(n, j) - m_safe[n]).astype(BF16))
            for n in heads:
                vtc = jnp.concatenate([vbt_ref[n * hd:(n + 1) * hd, pl.ds(base, ck)], ones], axis=0)
                acc_ref[n] = alpha[n] * acc_ref[n] + jnp.dot(vtc, p_ref[n - n0], preferred_element_type=F32)
                m_ref[n] = m_new[n]
        return carry
    lax.fori_loop(0, n_chunks, attn_body, 0)

    ot = jnp.concatenate([acc_ref[n, :hd] / acc_ref[n, hd:hd + 1] for n in range(ATT_KV_HEADS)], axis=0)
    for g in range(ATT_GROUP):
        og = ot[:, g * qb:(g + 1) * qb].T
        for n in range(ATT_KV_HEADS):
            h = n * ATT_GROUP + g
            o_ref[:, h * ATT_HEAD_DIM:(h + 1) * ATT_HEAD_DIM] = (
                og[:, n * ATT_HEAD_DIM:(n + 1) * ATT_HEAD_DIM].astype(o_ref.dtype))


def _dsa_prompt(qt, qit, wit, kb, vbt, kib, b, n_sel):
    t = kb.shape[1]
    ck = min(512, t)
    nqb = t // Q_BLOCK
    idx_bits = max(1, math.ceil(math.log2(t))) + 1
    colblk = lambda n: pl.BlockSpec((n, Q_BLOCK), lambda bi, i: (0, bi * nqb + i))
    res = lambda n: pl.BlockSpec((1, t, n), lambda bi, i: (bi, 0, 0))
    gq = ATT_GROUP * Q_BLOCK
    return pl.pallas_call(
        functools.partial(_dsa_prompt_kernel, ck=ck, n_sel=n_sel, idx_bits=idx_bits),
        grid=(b, nqb),
        in_specs=[colblk(ATT_Q_DIM), colblk(IDX_HEADS * IDX_DIM), colblk(IDX_HEADS),
                  res(ATT_KV_DIM), pl.BlockSpec((ATT_KV_DIM, t), lambda bi, i: (0, bi)), res(IDX_DIM)],
        out_specs=pl.BlockSpec((Q_BLOCK, ATT_Q_DIM), lambda bi, i: (bi * nqb + i, 0)),
        out_shape=jax.ShapeDtypeStruct((b * t, ATT_Q_DIM), BF16),
        scratch_shapes=[pltpu.VMEM((t, Q_BLOCK), I32),
                        pltpu.VMEM((ATT_KV_HEADS, 1, gq), F32),
                        pltpu.VMEM((ATT_KV_HEADS, ATT_HEAD_DIM + BF16_ROWS, gq), F32),
                        pltpu.VMEM((2, ck, Q_BLOCK), BF16), pltpu.VMEM((ATT_KV_HEADS, ck, gq), F32),
                        pltpu.VMEM((2, ck, gq), BF16)],
        compiler_params=_cparams(("parallel", "arbitrary"), 56),
        name="dsa_prompt",
    )(qt, qit, wit, kb, vbt, kib)


def _outproj_ln_kernel(o_ref, h_ref, w_ref, g_ref, b_ref, out_ref):
    y = jnp.dot(o_ref[...], w_ref[...], preferred_element_type=F32)
    out_ref[...] = _layer_norm(DEEPNORM_ALPHA * h_ref[...] + y, g_ref[...], b_ref[...])


def _outproj_ln(o2d, h2d, w_out, g, b):
    m, kdim = o2d.shape
    tm = min(m, 512)
    row = lambda n: pl.BlockSpec((tm, n), lambda i: (i, 0))
    return pl.pallas_call(
        _outproj_ln_kernel,
        grid=(m // tm,),
        in_specs=[row(kdim), row(D_MODEL), pl.BlockSpec((kdim, D_MODEL), lambda i: (0, 0)),
                  pl.BlockSpec((1, D_MODEL), lambda i: (0, 0)), pl.BlockSpec((1, D_MODEL), lambda i: (0, 0))],
        out_specs=row(D_MODEL),
        out_shape=jax.ShapeDtypeStruct((m, D_MODEL), F32),
        compiler_params=_cparams(("parallel",)),
        name="outproj_ln",
    )(o2d, h2d, w_out.astype(BF16), g.reshape(1, -1), b.reshape(1, -1))


def _dsa_sample_kernel(pt_ref, qbd_ref, qi_ref, wi_ref, kn_ref, vn_ref, kin_ref,
                       cki_hbm, ck_hbm, cv_hbm, o_ref,
                       ki_buf, k_buf, v_buf, s_ref, keys_ref, cut_ref, sem,
                       *, n_pages, t_new, n_sel, idx_bits, layer_off, lchunk):
    b = pl.program_id(0)
    past = n_pages * PAGE_SIZE
    ltot = past + LANES
    rows = keys_ref.shape[0]

    def page_copy(hbm, buf, p, s):
        page = pt_ref[b, p] + layer_off
        return pltpu.make_async_copy(hbm.at[page],
                                     buf.at[:, pl.ds(pl.multiple_of(p * PAGE_SIZE, PAGE_SIZE), PAGE_SIZE)],
                                     sem.at[s])

    def start_all(p, c):
        page_copy(cki_hbm, ki_buf, p, 0).start()
        page_copy(ck_hbm, k_buf, p, 1).start()
        page_copy(cv_hbm, v_buf, p, 2).start()
        return c
    lax.fori_loop(0, n_pages, start_all, 0)

    def wait_all(hbm, buf, s):
        def body(p, c):
            page_copy(hbm, buf, p, s).wait()
            return c
        lax.fori_loop(0, n_pages, body, 0)

    qi = qi_ref[0]
    wcol = wi_ref[0]
    wait_all(cki_hbm, ki_buf, 0)

    def idx_scores(kic, dims=NN_DIMS):
        d = lax.dot_general(qi, kic, dims, preferred_element_type=F32)
        d = jnp.maximum(d, 0.0) * wcol
        return jnp.sum(d.reshape(t_new, IDX_HEADS, d.shape[1]), axis=1)

    keys_ref[...] = jnp.full(keys_ref.shape, INT_MIN, I32)
    for c in range(past // lchunk):
        sc = idx_scores(ki_buf[:, c * lchunk:(c + 1) * lchunk].astype(BF16))
        keys_ref[0:t_new, c * lchunk:(c + 1) * lchunk] = _sort_key(sc)
    scn = idx_scores(kin_ref[0], NT_DIMS)
    tpos = lax.broadcasted_iota(I32, (t_new, LANES), 0)
    lpos = lax.broadcasted_iota(I32, (t_new, LANES), 1)
    keys_ref[0:t_new, past:ltot] = jnp.where((lpos <= tpos) & (lpos < t_new), _sort_key(scn), INT_MIN)

    sck = max(w for w in range(LANES, 8 * LANES + 1, LANES) if ltot % w == 0)
    theta = _select_threshold(keys_ref, cut_ref, ltot // sck, sck, rows, n_sel, idx_bits)
    key = keys_ref[...]
    lane = lax.broadcasted_iota(I32, (rows, ltot), 1)
    sel = ((key > theta) | ((key == theta) & (lane < cut_ref[:, :1]))) & (key != INT_MIN)
    bias = jnp.where(sel, 0.0, NEG_INF)[0:t_new]
    bias = jnp.broadcast_to(bias[:, None, :], (t_new, ATT_HEADS, ltot)).reshape(t_new * ATT_HEADS, ltot)

    qbd = qbd_ref[0]
    wait_all(ck_hbm, k_buf, 1)
    for c in range(past // lchunk):
        kc = k_buf[:, c * lchunk:(c + 1) * lchunk].astype(BF16)
        s_ref[:, c * lchunk:(c + 1) * lchunk] = jnp.dot(qbd, kc, preferred_element_type=F32)
    s_ref[:, past:ltot] = lax.dot_general(qbd, kn_ref[0], NT_DIMS, preferred_element_type=F32)
    s = s_ref[...] + bias
    m = jnp.max(s, axis=1, keepdims=True)
    p = jnp.exp(s - m)
    linv = 1.0 / jnp.sum(p, axis=1, keepdims=True)
    s_ref[...] = p
    wait_all(cv_hbm, v_buf, 2)
    o = jnp.dot(s_ref[:, past:ltot].astype(BF16), vn_ref[0], preferred_element_type=F32)
    for c in range(past // lchunk):
        vc = v_buf[:, c * lchunk:(c + 1) * lchunk].astype(BF16)
        o = o + lax.dot_general(s_ref[:, c * lchunk:(c + 1) * lchunk].astype(BF16), vc, NT_DIMS,
                                preferred_element_type=F32)
    o = o * linv
    nrow = (lax.broadcasted_iota(I32, (t_new * ATT_HEADS, ATT_HEAD_DIM), 0) % ATT_HEADS) // ATT_GROUP
    out = jnp.zeros((t_new * ATT_HEADS, ATT_HEAD_DIM), F32)
    for n in range(ATT_KV_HEADS):
        out = out + jnp.where(nrow == n, o[:, n * ATT_HEAD_DIM:(n + 1) * ATT_HEAD_DIM], 0.0)
    o_ref[0] = out.astype(o_ref.dtype)


def _dsa_sample(page_table, qs, qis, wi, kb, vb, kib, cache_ki, cache_k, cache_v, layer_off, n_sel):
    b, t, _ = qs.shape
    n_pages = page_table.shape[1]
    past = n_pages * PAGE_SIZE
    ltot = past + LANES
    rows = 8
    q5 = qs.reshape(b, t, ATT_KV_HEADS, ATT_GROUP, 1, ATT_HEAD_DIM)
    eye = jnp.eye(ATT_KV_HEADS, dtype=qs.dtype).reshape(1, 1, ATT_KV_HEADS, 1, ATT_KV_HEADS, 1)
    qbd = (q5 * eye).reshape(b, t * ATT_HEADS, ATT_KV_DIM)
    qi = qis.reshape(b, t * IDX_HEADS, IDX_DIM)
    wcol = wi.reshape(b, t * IDX_HEADS, 1)
    padr = lambda a: jnp.pad(a, ((0, 0), (0, LANES - t), (0, 0)))
    kn, vn, kin = padr(kb), padr(vb), padr(kib)
    idx_bits = max(1, math.ceil(math.log2(ltot))) + 1
    lchunk = min(1024, past)
    bl = lambda a: pl.BlockSpec((1,) + a.shape[1:], lambda i, pt: (i, 0, 0))
    anyspec = pl.BlockSpec(memory_space=pl.ANY)
    grid_spec = pltpu.PrefetchScalarGridSpec(
        num_scalar_prefetch=1,
        grid=(b,),
        in_specs=[bl(qbd), bl(qi), bl(wcol), bl(kn), bl(vn), bl(kin), anyspec, anyspec, anyspec],
        out_specs=pl.BlockSpec((1, t * ATT_HEADS, ATT_HEAD_DIM), lambda i, pt: (i, 0, 0)),
        scratch_shapes=[pltpu.VMEM((IDX_DIM, past), F32), pltpu.VMEM((ATT_KV_DIM, past), F32),
                        pltpu.VMEM((ATT_KV_DIM, past), F32), pltpu.VMEM((t * ATT_HEADS, ltot), F32),
                        pltpu.VMEM((rows, ltot), I32), pltpu.VMEM((rows, LANES), I32),
                        pltpu.SemaphoreType.DMA((3,))],
    )
    o = pl.pallas_call(
        functools.partial(_dsa_sample_kernel, n_pages=n_pages, t_new=t, n_sel=n_sel, idx_bits=idx_bits,
                          layer_off=layer_off, lchunk=lchunk),
        grid_spec=grid_spec,
        out_shape=jax.ShapeDtypeStruct((b, t * ATT_HEADS, ATT_HEAD_DIM), BF16),
        compiler_params=_cparams(("arbitrary",), 56),
        name="dsa_sample",
    )(page_table, qbd, qi, wcol, kn, vn, kin, cache_ki, cache_k, cache_v)
    return o.reshape(b, t, ATT_Q_DIM)


def _gla_proj_kernel(x_ref, wq_ref, wk_ref, wv_ref, wg_ref, wr_ref, wup_ref, bg_ref,
                     q_ref, k_ref, v_ref, la_ref, r_ref):
    xb = x_ref[...].astype(BF16)
    q_ref[...] = jnp.dot(xb, wq_ref[...], preferred_element_type=F32) * (GLA_DK ** -0.5)
    k_ref[...] = jnp.dot(xb, wk_ref[...], preferred_element_type=F32)
    v_ref[...] = jnp.dot(xb, wv_ref[...], preferred_element_type=F32)
    r_ref[...] = jnp.dot(xb, wr_ref[...], preferred_element_type=F32)
    gd = jnp.dot(xb, wg_ref[...], preferred_element_type=F32)
    glogit = jnp.dot(gd.astype(BF16), wup_ref[...], preferred_element_type=F32) + bg_ref[...]
    la_ref[...] = jax.nn.log_sigmoid(glogit) / GLA_TAU


def _gla_proj(x2d, w_in, w_gate_up, b_gate):
    m = x2d.shape[0]
    tm = min(m, 512)
    dq = GLA_HEADS * GLA_DK
    dv = GLA_HEADS * GLA_DV
    wq = w_in[:, :dq].astype(BF16)
    wk = w_in[:, dq:2 * dq].astype(BF16)
    wv = w_in[:, 2 * dq:2 * dq + dv].astype(BF16)
    wg = jnp.pad(w_in[:, 2 * dq + dv:2 * dq + dv + GLA_GATE_RANK], ((0, 0), (0, LANES - GLA_GATE_RANK))).astype(BF16)
    wr = w_in[:, 2 * dq + dv + GLA_GATE_RANK:].astype(BF16)
    wup = jnp.pad(w_gate_up, ((0, LANES - GLA_GATE_RANK), (0, 0))).astype(BF16)
    row = lambda n: pl.BlockSpec((tm, n), lambda i: (i, 0))
    full = lambda a: pl.BlockSpec(a.shape, lambda i: (0, 0))
    bg = b_gate.reshape(1, -1)
    outs = [dq, dq, dv, dq, dv]
    return pl.pallas_call(
        _gla_proj_kernel,
        grid=(m // tm,),
        in_specs=[row(D_MODEL), full(wq), full(wk), full(wv), full(wg), full(wr), full(wup), full(bg)],
        out_specs=[row(n) for n in outs],
        out_shape=[jax.ShapeDtypeStruct((m, n), F32) for n in outs],
        compiler_params=_cparams(("parallel",)),
        name="gla_proj",
    )(x2d, wq, wk, wv, wg, wr, wup, bg)


def _gla_chunk(q, k, v, g, st, c):
    sub = GLA_SUB
    n_sub = c // sub
    ri = lax.broadcasted_iota(I32, (c, c), 0)
    ci = lax.broadcasted_iota(I32, (c, c), 1)
    tri = jnp.where(ci <= ri, 1.0, 0.0).astype(F32)
    bc = jnp.dot(tri, g, preferred_element_type=F32, precision=lax.Precision.HIGHEST)
    o = lax.dot_general((q * jnp.exp(bc)).astype(BF16), st.astype(BF16), NT_DIMS, preferred_element_type=F32)
    if n_sub > 1:
        mrow = jnp.concatenate([jnp.zeros((sub, GLA_DK), F32)] +
                               [jnp.broadcast_to(bc[i * sub - 1:i * sub], (sub, GLA_DK)) for i in range(1, n_sub)], axis=0)
        qh = (q * jnp.exp(bc - mrow)).astype(BF16)
        parts = [jnp.zeros((sub, GLA_DV), F32)]
        for i in range(1, n_sub):
            kh = (k[:i * sub] * jnp.exp(bc[i * sub - 1:i * sub] - bc[:i * sub])).astype(BF16)
            a = lax.dot_general(qh[i * sub:(i + 1) * sub], kh, NT_DIMS, preferred_element_type=F32)
            parts.append(jnp.dot(a.astype(BF16), v[:i * sub].astype(BF16), preferred_element_type=F32))
        o = o + jnp.concatenate(parts, axis=0)
    rsub = lax.broadcasted_iota(I32, (c, 1), 0) % sub
    for dlt in range(sub):
        ks = k if dlt == 0 else pltpu.roll(k, dlt, 0)
        bs = bc if dlt == 0 else pltpu.roll(bc, dlt, 0)
        vs = v if dlt == 0 else pltpu.roll(v, dlt, 0)
        w = jnp.sum(q * ks * jnp.exp(jnp.minimum(bc - bs, 0.0)), axis=1, keepdims=True)
        o = o + jnp.where(rsub >= dlt, w, 0.0) * vs
    bl = bc[c - 1:c]
    kt = (k * jnp.exp(bl - bc)).astype(BF16)
    st_new = st * jnp.exp(bl) + lax.dot_general(v.astype(BF16), kt, TN_DIMS, preferred_element_type=F32)
    return o, st_new


def _gla_kernel(q_ref, k_ref, v_ref, g_ref, r_ref, ng_ref, s0_ref, o_ref, sf_ref, st_ref, *, c, n_c):
    tb = pl.program_id(2)

    @pl.when(tb == 0)
    def _():
        st_ref[...] = s0_ref[0, 0].T

    st = st_ref[...]
    for ci in range(n_c):
        sl = slice(ci * c, (ci + 1) * c)
        o, st = _gla_chunk(q_ref[0, sl, :], k_ref[0, sl, :], v_ref[0, sl, :], g_ref[0, sl, :], st, c)
        o = o * lax.rsqrt(jnp.mean(o * o, axis=-1, keepdims=True) + LN_EPS) * ng_ref[...]
        o_ref[0, sl, :] = (o * jax.nn.silu(r_ref[0, sl, :])).astype(o_ref.dtype)
    st_ref[...] = st

    @pl.when(tb == pl.num_programs(2) - 1)
    def _():
        sf_ref[0, 0] = st.T


def _gla(q, k, v, la, r, norm_g, s0):
    b, t, _ = q.shape
    c = math.gcd(t, GLA_CHUNK)
    tblk = math.gcd(t, 4 * GLA_CHUNK)
    n_c = tblk // c
    kspec = pl.BlockSpec((1, tblk, GLA_DK), lambda bi, h, i: (bi, i, h))
    vspec = pl.BlockSpec((1, tblk, GLA_DV), lambda bi, h, i: (bi, i, h))
    sspec = pl.BlockSpec((1, 1, GLA_DK, GLA_DV), lambda bi, h, i: (bi, h, 0, 0))
    return pl.pallas_call(
        functools.partial(_gla_kernel, c=c, n_c=n_c),
        grid=(b, GLA_HEADS, t // tblk),
        in_specs=[kspec, kspec, vspec, kspec, vspec, pl.BlockSpec((1, GLA_DV), lambda bi, h, i: (0, 0)), sspec],
        out_specs=[vspec, sspec],
        out_shape=[jax.ShapeDtypeStruct((b, t, GLA_HEADS * GLA_DV), BF16),
                   jax.ShapeDtypeStruct((b, GLA_HEADS, GLA_DK, GLA_DV), F32)],
        scratch_shapes=[pltpu.VMEM((GLA_DV, GLA_DK), F32)],
        compiler_params=_cparams(("parallel", "parallel", "arbitrary")),
        name="gla_scan",
    )(q, k, v, la, r, norm_g.reshape(1, -1), s0)


def _gla_mixer(x, s0, w_in, w_gate_up, b_gate, norm_g):
    b, t, _ = x.shape
    q, k, v, la, r = _gla_proj(x.reshape(b * t, D_MODEL), w_in, w_gate_up, b_gate)
    tp = -(-t // GLA_SUB) * GLA_SUB
    r3 = lambda a: jnp.pad(a.reshape(b, t, -1), ((0, 0), (0, tp - t), (0, 0)))
    o, sf = _gla(r3(q), r3(k), r3(v), r3(la), r3(r), norm_g, s0)
    return o[:, :t], sf


def _top_values(s, k):
    vals = []
    cur = s
    for _ in range(k):
        mx = jnp.max(cur, axis=0, keepdims=True)
        vals.append(mx)
        cur = jnp.where(cur == mx, NEG_INF, cur)
    return vals


def _top_values_ranked(s, k):
    vals = []
    cur = s
    rank = jnp.full(s.shape, float(k), F32)
    for i in range(k):
        mx = jnp.max(cur, axis=0, keepdims=True)
        vals.append(mx)
        top = cur == mx
        rank = jnp.where(top, float(i), rank)
        cur = jnp.where(top, NEG_INF, cur)
    return vals, rank


def _peer_select_kernel(x_ref, wq_ref, k1_ref, k2_ref, c1_ref, p_ref, r2_ref, e_ref):
    xb = x_ref[...].astype(BF16)
    q = jnp.dot(xb, wq_ref[...], preferred_element_type=F32).astype(BF16)
    tt = q.shape[0]
    k1 = k1_ref[...]
    k2 = k2_ref[...]
    sub8 = lax.broadcasted_iota(I32, (8, tt), 0)
    for h in range(PEER_HEADS):
        s1 = lax.dot_general(k1, q[:, h * PEER_D_KEY:h * PEER_D_KEY + PEER_HALF], NT_DIMS,
                             preferred_element_type=F32)
        s2 = lax.dot_general(k2, q[:, h * PEER_D_KEY + PEER_HALF:(h + 1) * PEER_D_KEY], NT_DIMS,
                             preferred_element_type=F32)
        t1, rank1 = _top_values_ranked(s1, PEER_TOPK)
        t2, rank2 = _top_values_ranked(s2, PEER_TOPK)
        t2a = jnp.concatenate(t2, axis=0)
        cands = [t1[0] + t2a]
        for a in range(1, PEER_TOPK):
            nb = PEER_TOPK // (a + 1)
            cands.append(jnp.where(sub8 < nb, t1[a] + t2a[:8], NEG_INF))
        best = _top_values(jnp.concatenate(cands, axis=0), PEER_TOPK)
        theta = best[PEER_TOPK - 1]
        z = best[0] * 0.0
        for bv in best:
            z = z + jnp.exp(bv - best[0])
        count1 = jnp.zeros(s1.shape, F32)
        for a in range(PEER_TOPK):
            cnt_a = jnp.sum(jnp.where(t1[a] + t2a >= theta, 1.0, 0.0), axis=0, keepdims=True)
            count1 = jnp.where(rank1 == float(a), cnt_a, count1)
        c1_ref[h] = count1
        p_ref[h] = jnp.exp(s1 - t1[0]) * (0.5 / z)
        r2_ref[h] = rank2
        e_ref[h] = jnp.exp(s2 - t2[0])


def _peer_select(x2d, w_q, keys1, keys2):
    n = x2d.shape[0]
    tt = min(n, 256)
    wq = w_q.astype(BF16)
    k1 = keys1.astype(BF16)
    k2 = keys2.astype(BF16)
    full = lambda a: pl.BlockSpec(a.shape, lambda i: (0,) * a.ndim)
    hspec = pl.BlockSpec((PEER_HEADS, PEER_N_KEYS, tt), lambda i: (0, 0, i))
    big = lambda dt: jax.ShapeDtypeStruct((PEER_HEADS, PEER_N_KEYS, n), dt)
    return pl.pallas_call(
        _peer_select_kernel,
        grid=(n // tt,),
        in_specs=[pl.BlockSpec((tt, D_MODEL), lambda i: (i, 0)), full(wq), full(k1), full(k2)],
        out_specs=[hspec, hspec, hspec, hspec],
        out_shape=[big(F32), big(F32), big(F32), big(F32)],
        compiler_params=_cparams(("parallel",)),
        name="peer_select",
    )(x2d, wq, k1, k2)


def _peer_mix_kernel(h_ref, c1_ref, p_ref, r2_ref, e_ref, u_ref, vt_ref, g_ref, bb_ref, out_ref,
                     xb_ref, ht_ref, wt_ref, yt_ref, *, n_i1):
    eb = pl.program_id(1)
    tt = h_ref.shape[0]

    @pl.when(eb == 0)
    def _():
        xb_ref[...] = h_ref[...].T.astype(BF16)
        yt_ref[...] = jnp.zeros(yt_ref.shape, F32)

    ht_ref[...] = jnp.dot(u_ref[...], xb_ref[...], preferred_element_type=F32)

    i1_base = pl.multiple_of(eb * n_i1, n_i1)

    for lg in range(tt // LANES):
        ls = slice(lg * LANES, (lg + 1) * LANES)
        c8 = [c1_ref[h, pl.ds(i1_base, n_i1), ls] for h in range(PEER_HEADS)]
        p8 = [p_ref[h, pl.ds(i1_base, n_i1), ls] for h in range(PEER_HEADS)]
        for j in range(n_i1):
            cb = [jnp.broadcast_to(c8[h][j:j + 1], (BF16_ROWS, LANES)) for h in range(PEER_HEADS)]
            pb = [jnp.broadcast_to(p8[h][j:j + 1], (BF16_ROWS, LANES)) for h in range(PEER_HEADS)]
            for r in range(PEER_N_KEYS // BF16_ROWS):
                ks = slice(r * BF16_ROWS, (r + 1) * BF16_ROWS)
                acc = jnp.zeros((BF16_ROWS, LANES), F32)
                for h in range(PEER_HEADS):
                    acc = acc + jnp.where(r2_ref[h, ks, ls] < cb[h], pb[h] * e_ref[h, ks, ls], 0.0)
                rs = slice(j * PEER_N_KEYS + r * BF16_ROWS, j * PEER_N_KEYS + (r + 1) * BF16_ROWS)
                x = ht_ref[rs, ls]
                gelu2 = x * (1.0 + lax.erf(x * (2.0 ** -0.5)))
                wt_ref[rs, ls] = (gelu2 * acc).astype(BF16)

    yt_ref[...] += jnp.dot(vt_ref[...], wt_ref[...], preferred_element_type=F32)

    @pl.when(eb == pl.num_programs(1) - 1)
    def _():
        y = yt_ref[...].T
        out_ref[...] = _layer_norm(DEEPNORM_ALPHA * h_ref[...] + y, g_ref[...], bb_ref[...])


def _peer_ln(h2d, w_q, keys1, keys2, u, v, g, b):
    n_real = h2d.shape[0]
    if n_real % LANES:
        h2d = jnp.pad(h2d, ((0, LANES - n_real % LANES), (0, 0)))
        return _peer_ln(h2d, w_q, keys1, keys2, u, v, g, b)[:n_real]
    n = n_real
    c1, p, r2, e = _peer_select(h2d, w_q, keys1, keys2)
    tt = min(n, 512)
    eblk = 1024
    n_e = u.shape[0]
    ub = u.astype(BF16)
    n_blk = n_e // eblk
    vt = jnp.swapaxes(v.reshape(n_blk, eblk, D_MODEL), 1, 2).astype(BF16)
    hspec = pl.BlockSpec((PEER_HEADS, PEER_N_KEYS, tt), lambda i, j: (0, 0, i))
    vec = pl.BlockSpec((1, D_MODEL), lambda i, j: (0, 0))
    return pl.pallas_call(
        functools.partial(_peer_mix_kernel, n_i1=eblk // PEER_N_KEYS),
        grid=(n // tt, n_blk),
        in_specs=[pl.BlockSpec((tt, D_MODEL), lambda i, j: (i, 0)), hspec, hspec, hspec, hspec,
                  pl.BlockSpec((eblk, D_MODEL), lambda i, j: (j, 0)),
                  pl.BlockSpec((None, D_MODEL, eblk), lambda i, j: (j, 0, 0)), vec, vec],
        out_specs=pl.BlockSpec((tt, D_MODEL), lambda i, j: (i, 0)),
        out_shape=jax.ShapeDtypeStruct((n, D_MODEL), F32),
        scratch_shapes=[pltpu.VMEM((D_MODEL, tt), BF16), pltpu.VMEM((eblk, tt), F32),
                        pltpu.VMEM((eblk, tt), BF16), pltpu.VMEM((D_MODEL, tt), F32)],
        compiler_params=_cparams(("parallel", "arbitrary"), 56),
        name="peer_mix",
    )(h2d, c1, p, r2, e, ub, vt, g.reshape(1, -1), b.reshape(1, -1))


def kernel(x_prompt, x_sample, cache_k, cache_v, cache_kidx, state_gla, page_table,
           attn_w_in, attn_w_out, gla_w_in, gla_w_gate_up, gla_b_gate, gla_norm_g, gla_w_out,
           peer_w_q, peer_keys1, peer_keys2, peer_u, peer_v, ln1_g, ln1_b, ln2_g, ln2_b):
    bp, tp, _ = x_prompt.shape
    bs, ts, _ = x_sample.shape
    n_pool = cache_k.shape[1]
    past = page_table.shape[1] * PAGE_SIZE
    ck = jnp.swapaxes(cache_k.reshape(-1, PAGE_SIZE, ATT_KV_DIM), 1, 2)
    cv = jnp.swapaxes(cache_v.reshape(-1, PAGE_SIZE, ATT_KV_DIM), 1, 2)
    cki = jnp.swapaxes(cache_kidx.reshape(-1, PAGE_SIZE, IDX_DIM), 1, 2)
    hp = x_prompt.reshape(bp * tp, D_MODEL)
    hs = x_sample.reshape(bs * ts, D_MODEL)
    kp_l, vp_l, kip_l, sp_l = [], [], [], []
    ks_l, vs_l, kis_l, ss_l = [], [], [], []
    for i in range(DEPTH):
        j = i // 2
        if i % 2 == 0:
            k, v, ki, kb, kib, qt, qit, wit, vbt = _attn_proj_t(hp, attn_w_in[j])
            r3 = lambda a: a.reshape(bp, tp, -1)
            op = _dsa_prompt(qt, qit, wit, r3(kb), vbt, r3(kib), bp, min(TOPK_MAX, tp // 4))
            kp_l.append(k.reshape(bp, tp, ATT_KV_HEADS, ATT_HEAD_DIM))
            vp_l.append(v.reshape(bp, tp, ATT_KV_HEADS, ATT_HEAD_DIM))
            kip_l.append(ki.reshape(bp, tp, IDX_DIM))
            qs, k, v, kb, vb, qis, ki, kib, wi = _attn_proj(hs, attn_w_in[j])
            r3 = lambda a: a.reshape(bs, ts, -1)
            os_ = _dsa_sample(page_table, r3(qs), r3(qis), r3(wi), r3(kb), r3(vb), r3(kib), cki, ck, cv,
                              j * n_pool, min(TOPK_MAX, (past + ts) // 4))
            ks_l.append(k.reshape(bs, ts, ATT_KV_HEADS, ATT_HEAD_DIM))
            vs_l.append(v.reshape(bs, ts, ATT_KV_HEADS, ATT_HEAD_DIM))
            kis_l.append(ki.reshape(bs, ts, IDX_DIM))
            w_out = attn_w_out[j]
        else:
            s0 = jnp.zeros((bp, GLA_HEADS, GLA_DK, GLA_DV), F32)
            op, sp = _gla_mixer(hp.reshape(bp, tp, D_MODEL), s0, gla_w_in[j], gla_w_gate_up[j], gla_b_gate[j],
                                gla_norm_g[j])
            os_, ss = _gla_mixer(hs.reshape(bs, ts, D_MODEL), state_gla[j], gla_w_in[j], gla_w_gate_up[j],
                                 gla_b_gate[j], gla_norm_g[j])
            sp_l.append(sp)
            ss_l.append(ss)
            w_out = gla_w_out[j]
        hp = _outproj_ln(op.reshape(bp * tp, -1), hp, w_out, ln1_g[i], ln1_b[i])
        hs = _outproj_ln(os_.reshape(bs * ts, -1), hs, w_out, ln1_g[i], ln1_b[i])
        pw = (peer_w_q[i], peer_keys1[i], peer_keys2[i], peer_u[i], peer_v[i], ln2_g[i], ln2_b[i])
        hp = _peer_ln(hp, *pw)
        hs = _peer_ln(hs, *pw)
    return (hp.reshape(bp, tp, D_MODEL), hs.reshape(bs, ts, D_MODEL),
            jnp.stack(kp_l), jnp.stack(vp_l), jnp.stack(kip_l), jnp.stack(sp_l),
            jnp.stack(ks_l), jnp.stack(vs_l), jnp.stack(kis_l), jnp.stack(ss_l))
```

```python
import functools
import math

import jax
import jax.numpy as jnp
from jax import lax
from jax.experimental import pallas as pl
from jax.experimental.pallas import tpu as pltpu

F32 = jnp.float32
BF16 = jnp.bfloat16
I32 = jnp.int32

D_MODEL = 1024
DEPTH = 2
PAGE_SIZE = 128
DEEPNORM_ALPHA = (2.0 * DEPTH) ** 0.25
LN_EPS = 1e-5

ATT_HEADS = 16
ATT_KV_HEADS = 4
ATT_HEAD_DIM = 64
ATT_GROUP = ATT_HEADS // ATT_KV_HEADS
IDX_HEADS = 8
IDX_DIM = 64
TOPK_MAX = 256
Q_BLOCK = 128
ATT_Q_DIM = ATT_HEADS * ATT_HEAD_DIM
ATT_KV_DIM = ATT_KV_HEADS * ATT_HEAD_DIM

GLA_HEADS = 4
GLA_DK = D_MODEL // 2 // GLA_HEADS
GLA_DV = D_MODEL // GLA_HEADS
GLA_GATE_RANK = 16
GLA_TAU = 16.0
GLA_CHUNK = 64
GLA_SUB = 16

PEER_HEADS = 8
PEER_N_KEYS = 128
PEER_D_KEY = 256
PEER_HALF = PEER_D_KEY // 2
PEER_TOPK = 16
PEER_EXPERT_BLOCK = 8 * PEER_N_KEYS

LANES = 128
BF16_ROWS = 16
LOG2_E = 1.4426950408889634
INT_MIN = -(2 ** 31)
NEG_INF = float("-inf")
MASKED = -1e30

NN_DIMS = (((1,), (0,)), ((), ()))
NT_DIMS = (((1,), (1,)), ((), ()))
TN_DIMS = (((0,), (0,)), ((), ()))


def _cparams(sem, vmem_mib=None, flags=None):
    kw = dict(dimension_semantics=sem)
    if vmem_mib is not None:
        kw["vmem_limit_bytes"] = vmem_mib * 1024 * 1024
    if flags:
        kw["flags"] = flags
    return pltpu.CompilerParams(**kw)


def _sort_key(x):
    b = pltpu.bitcast(x + 0.0, I32)
    return b ^ ((b >> 31) & 0x7FFFFFFF)


def _layer_norm(z, g, b):
    mu = jnp.mean(z, axis=-1, keepdims=True)
    zc = z - mu
    var = jnp.mean(zc * zc, axis=-1, keepdims=True)
    return zc * lax.rsqrt(var + LN_EPS) * g + b


def _row_count(keys_ref, n_chunks, ck, rows, pred):
    def body(c, acc):
        base = pl.multiple_of(c * ck, ck)
        return acc + jnp.where(pred(keys_ref[:, pl.ds(base, ck)], base), 1.0, 0.0)
    acc = lax.fori_loop(0, n_chunks, body, jnp.zeros((rows, ck), F32))
    return jnp.sum(acc, axis=1, keepdims=True)


def _select_threshold(keys_ref, cut_ref, n_chunks, ck, rows, kth, idx_bits):
    kth_f = float(kth)

    def count_ge(t):
        tb = jnp.broadcast_to(t, (rows, ck))
        return _row_count(keys_ref, n_chunks, ck, rows, lambda blk, base: blk >= tb)

    theta = jnp.full((rows, 1), INT_MIN, I32)
    zero = jnp.zeros((rows, 1), I32)
    theta = jnp.where(count_ge(zero) >= kth_f, zero, theta)

    def bit_body(i, t):
        cand = t | lax.shift_left(jnp.int32(1), jnp.int32(30) - i)
        return jnp.where(count_ge(cand) >= kth_f, cand, t)
    theta = lax.fori_loop(0, 31, bit_body, theta)

    n_gt = count_ge(theta + 1)
    n_ge = count_ge(theta)
    need = kth_f - n_gt
    ambiguous = (n_ge - n_gt > need) & (theta > INT_MIN)
    cut_ref[...] = jnp.full(cut_ref.shape, 2 ** idx_bits, I32)

    @pl.when(jnp.max(jnp.where(ambiguous, 1.0, 0.0)) > 0.5)
    def _():
        thb = jnp.broadcast_to(theta, (rows, ck))
        lane = lax.broadcasted_iota(I32, (rows, ck), 1)

        def ties_before(jc):
            jb = jnp.broadcast_to(jc, (rows, ck))
            return _row_count(keys_ref, n_chunks, ck, rows,
                              lambda blk, base: (blk == thb) & (lane + base < jb))

        def jbit(i, jcur):
            cand = jcur | lax.shift_left(jnp.int32(1), jnp.int32(idx_bits - 1) - i)
            return jnp.where(ties_before(cand) <= need, cand, jcur)
        jfin = lax.fori_loop(0, idx_bits, jbit, jnp.zeros((rows, 1), I32))
        jfin = jnp.where(ambiguous, jfin, 2 ** idx_bits)
        cut_ref[...] = jnp.broadcast_to(jfin, cut_ref.shape)
    return theta


def _attn_proj_kernel(x_ref, wq_ref, wkv_ref, wqi_ref, wkw_ref,
                      q_ref, k_ref, v_ref, kb_ref, vb_ref, qi_ref, ki_ref, kib_ref, wi_ref):
    xb = x_ref[...].astype(BF16)
    q = jnp.dot(xb, wq_ref[...], preferred_element_type=F32)
    q_ref[...] = (q * (ATT_HEAD_DIM ** -0.5)).astype(BF16)
    kv = jnp.dot(xb, wkv_ref[...], preferred_element_type=F32)
    k = kv[:, :ATT_KV_DIM]
    v = kv[:, ATT_KV_DIM:]
    k_ref[...] = k
    v_ref[...] = v
    kb_ref[...] = k.astype(BF16)
    vb_ref[...] = v.astype(BF16)
    qi = jnp.dot(xb, wqi_ref[...], preferred_element_type=F32)
    qi_ref[...] = (qi * (IDX_DIM ** -0.5)).astype(BF16)
    kw = jnp.dot(xb, wkw_ref[...], preferred_element_type=F32)
    ki = kw[:, :IDX_DIM]
    ki_ref[...] = ki
    kib_ref[...] = ki.astype(BF16)
    wi_ref[...] = kw[:, IDX_DIM:IDX_DIM + IDX_HEADS] * (IDX_HEADS ** -0.5)


def _attn_weights(w_in):
    s0 = ATT_Q_DIM
    s2 = s0 + 2 * ATT_KV_DIM
    s3 = s2 + IDX_HEADS * IDX_DIM
    wkw = jnp.pad(w_in[:, s3:], ((0, 0), (0, LANES - (w_in.shape[1] - s3))))
    return (w_in[:, :s0].astype(BF16), w_in[:, s0:s2].astype(BF16), w_in[:, s2:s3].astype(BF16),
            wkw.astype(BF16))


def _attn_proj(x2d, w_in):
    m = x2d.shape[0]
    tm = min(m, 512)
    wq, wkv, wqi, wkw = _attn_weights(w_in)
    row = lambda n: pl.BlockSpec((tm, n), lambda i: (i, 0))
    full = lambda a: pl.BlockSpec(a.shape, lambda i: (0, 0))
    outs = [(ATT_Q_DIM, BF16), (ATT_KV_DIM, F32), (ATT_KV_DIM, F32), (ATT_KV_DIM, BF16),
            (ATT_KV_DIM, BF16), (IDX_HEADS * IDX_DIM, BF16), (IDX_DIM, F32), (IDX_DIM, BF16),
            (IDX_HEADS, F32)]
    return pl.pallas_call(
        _attn_proj_kernel,
        grid=(m // tm,),
        in_specs=[row(D_MODEL), full(wq), full(wkv), full(wqi), full(wkw)],
        out_specs=[row(n) for n, _ in outs],
        out_shape=[jax.ShapeDtypeStruct((m, n), dt) for n, dt in outs],
        compiler_params=_cparams(("parallel",)),
        name="attn_proj",
    )(x2d, wq, wkv, wqi, wkw)


def _attn_proj_t_kernel(x_ref, wq_ref, wkv_ref, wqi_ref, wkw_ref,
                        k_ref, v_ref, ki_ref, kb_ref, kib_ref, qt_ref, qit_ref, wit_ref, vbt_ref):
    xb = x_ref[...].astype(BF16)
    q = jnp.dot(xb, wq_ref[...], preferred_element_type=F32)
    qt_ref[...] = (q * (ATT_HEAD_DIM ** -0.5 * LOG2_E)).T.astype(BF16)
    kv = jnp.dot(xb, wkv_ref[...], preferred_element_type=F32)
    k = kv[:, :ATT_KV_DIM]
    v = kv[:, ATT_KV_DIM:]
    k_ref[...] = k
    v_ref[...] = v
    kb_ref[...] = k.astype(BF16)
    vbt_ref[...] = v.T.astype(BF16)
    qi = jnp.dot(xb, wqi_ref[...], preferred_element_type=F32)
    qit_ref[...] = (qi * (IDX_DIM ** -0.5)).T.astype(BF16)
    kw = jnp.dot(xb, wkw_ref[...], preferred_element_type=F32)
    ki = kw[:, :IDX_DIM]
    ki_ref[...] = ki
    kib_ref[...] = ki.astype(BF16)
    wit_ref[...] = kw.T[IDX_DIM:IDX_DIM + IDX_HEADS, :] * (IDX_HEADS ** -0.5)


def _attn_proj_t(x2d, w_in):
    m = x2d.shape[0]
    tm = min(m, 512)
    wq, wkv, wqi, wkw = _attn_weights(w_in)
    row = lambda n: pl.BlockSpec((tm, n), lambda i: (i, 0))
    col = lambda n: pl.BlockSpec((n, tm), lambda i: (0, i))
    full = lambda a: pl.BlockSpec(a.shape, lambda i: (0, 0))
    nat = [(ATT_KV_DIM, F32), (ATT_KV_DIM, F32), (IDX_DIM, F32), (ATT_KV_DIM, BF16), (IDX_DIM, BF16)]
    tr = [(ATT_Q_DIM, BF16), (IDX_HEADS * IDX_DIM, BF16), (IDX_HEADS, F32), (ATT_KV_DIM, BF16)]
    return pl.pallas_call(
        _attn_proj_t_kernel,
        grid=(m // tm,),
        in_specs=[row(D_MODEL), full(wq), full(wkv), full(wqi), full(wkw)],
        out_specs=[row(n) for n, _ in nat] + [col(n) for n, _ in tr],
        out_shape=[jax.ShapeDtypeStruct((m, n), dt) for n, dt in nat] +
                  [jax.ShapeDtypeStruct((n, m), dt) for n, dt in tr],
        compiler_params=_cparams(("parallel",)),
        name="attn_proj_t",
    )(x2d, wq, wkv, wqi, wkw)


def _col_count(keys_ref, n_chunks, ck, pred):
    width = keys_ref.shape[1]
    par = math.gcd(ck, 64)

    def body(c, acc):
        base = pl.multiple_of(c * ck, ck)
        hit = pred(keys_ref[pl.ds(base, ck), :], base)
        return acc + jnp.sum(jnp.where(hit, 1.0, 0.0).reshape(ck // par, par, width), axis=0)
    acc = lax.fori_loop(0, n_chunks, body, jnp.zeros((par, width), F32))
    return jnp.sum(acc, axis=0, keepdims=True)


def _select_threshold_cols(keys_ref, n_chunks, ck, kth, idx_bits):
    width = keys_ref.shape[1]
    kth_f = float(kth)

    def count_ge(t):
        return _col_count(keys_ref, n_chunks, ck, lambda blk, base: blk >= t)

    theta = jnp.full((1, width), INT_MIN, I32)
    zero = jnp.zeros((1, width), I32)
    theta = jnp.where(count_ge(zero) >= kth_f, zero, theta)

    def bit_body(i, t):
        cand = t | lax.shift_left(jnp.int32(1), jnp.int32(30) - i)
        return jnp.where(count_ge(cand) >= kth_f, cand, t)
    theta = lax.fori_loop(0, 31, bit_body, theta)

    n_gt = count_ge(theta + 1)
    n_ge = count_ge(theta)
    need = kth_f - n_gt
    ambiguous = (n_ge - n_gt > need) & (theta > INT_MIN)
    no_cut = jnp.full((1, width), 2 ** idx_bits, I32)

    def tie_search():
        row = lax.broadcasted_iota(I32, (ck, width), 0)

        def ties_before(jc):
            return _col_count(keys_ref, n_chunks, ck, lambda blk, base: (blk == theta) & (row + base < jc))

        def jbit(i, jcur):
            cand = jcur | lax.shift_left(jnp.int32(1), jnp.int32(idx_bits - 1) - i)
            return jnp.where(ties_before(cand) <= need, cand, jcur)
        jfin = lax.fori_loop(0, idx_bits, jbit, jnp.zeros((1, width), I32))
        return jnp.where(ambiguous, jfin, no_cut)

    cut = lax.cond(jnp.max(jnp.where(ambiguous, 1.0, 0.0)) > 0.5, tie_search, lambda: no_cut)
    return theta, cut


def _dsa_prompt_kernel(qt_ref, qit_ref, wit_ref, kb_ref, vbt_ref, kib_ref, o_ref,
                       keys_ref, m_ref, acc_ref, bias_ref, s_ref, p_ref, *, ck, n_sel, idx_bits):
    i = pl.program_id(1)
    qb = Q_BLOCK
    n_chunks = ((i + 1) * qb + ck - 1) // ck
    q_pos = i * qb + lax.broadcasted_iota(I32, (ck, qb), 1)
    row = lax.broadcasted_iota(I32, (ck, qb), 0)

    qit = qit_ref[...]
    qi_all = jnp.concatenate([qit[h * IDX_DIM:(h + 1) * IDX_DIM, :] for h in range(IDX_HEADS)], axis=1)
    wit = wit_ref[...]

    def score_body(c, carry):
        base = pl.multiple_of(c * ck, ck)
        d = jnp.dot(kib_ref[0, pl.ds(base, ck), :], qi_all, preferred_element_type=F32)
        acc = jnp.zeros((ck, qb), F32)
        for h in range(IDX_HEADS):
            acc = acc + jnp.maximum(d[:, h * qb:(h + 1) * qb], 0.0) * wit[h:h + 1, :]
        keys_ref[pl.ds(base, ck), :] = jnp.where(row + base <= q_pos, _sort_key(acc), INT_MIN)
        return carry
    lax.fori_loop(0, n_chunks, score_body, 0)

    theta, cut = _select_threshold_cols(keys_ref, n_chunks, ck, n_sel, idx_bits)

    qt = qt_ref[...]
    qn = [jnp.concatenate([qt[(n * ATT_GROUP + g) * ATT_HEAD_DIM:(n * ATT_GROUP + g + 1) * ATT_HEAD_DIM, :]
                           for g in range(ATT_GROUP)], axis=1) for n in range(ATT_KV_HEADS)]
    m_ref[...] = jnp.full(m_ref.shape, NEG_INF, F32)
    acc_ref[...] = jnp.zeros(acc_ref.shape, F32)
    hd = ATT_HEAD_DIM
    ones = jnp.ones((acc_ref.shape[1] - hd, ck), BF16)
    par = math.gcd(ck, 64)

    n_sub = ck // par
    pair = 2

    eye = (lax.broadcasted_iota(I32, (qb, ATT_GROUP * qb), 1) % qb ==
           lax.broadcasted_iota(I32, (qb, ATT_GROUP * qb), 0))
    eye = jnp.where(eye, 1.0, 0.0).astype(BF16)
    qa = [jnp.concatenate([eye, qn[n]], axis=0) for n in range(ATT_KV_HEADS)]

    def mask_rows(kbase, slot):
        key = keys_ref[pl.ds(kbase, ck), :]
        sel = ((key > theta) | ((key == theta) & (row + kbase < cut))) & (key != INT_MIN)
        bias_ref[slot] = jnp.where(sel, 0.0, MASKED).astype(BF16)

    def scores(kbase, slot, n):
        kc = jnp.concatenate([bias_ref[slot], kb_ref[0, pl.ds(kbase, ck), n * hd:(n + 1) * hd]], axis=1)
        s_ref[n] = jnp.dot(kc, qa[n], preferred_element_type=F32)
    mask_rows(0, 0)
    for n in range(pair):
        scores(0, 0, n)

    def attn_body(c, carry):
        base = pl.multiple_of(c * ck, ck)
        slot = c % 2

        for n0 in range(0, ATT_KV_HEADS, pair):
            heads = range(n0, n0 + pair)
            if n0 + pair < ATT_KV_HEADS:
                for n in heads:
                    scores(base, slot, n + pair)
            else:
                nbase = pl.multiple_of(jnp.minimum(c + 1, n_chunks - 1) * ck, ck)
                mask_rows(nbase, 1 - slot)
                for n in heads:
                    scores(nbase, 1 - slot, n - n0)

            def masked(n, j):
                return s_ref[n, j * par:(j + 1) * par, :]
            mx = {n: masked(n, 0) for n in heads}
            for j in range(1, n_sub):
                for n in heads:
                    mx[n] = jnp.maximum(mx[n], masked(n, j))
            m_safe, alpha, m_new = {}, {}, {}
            for n in heads:
                m_prev = m_ref[n]
                m_new[n] = jnp.maximum(m_prev, jnp.max(mx[n], axis=0, keepdims=True))
                m_safe[n] = jnp.where(m_new[n] == NEG_INF, 0.0, m_new[n])
                alpha[n] = jnp.exp2(m_prev - m_safe[n])
            for j in range(n_sub):
                for n in heads:
                    p_ref[n - n0, j * par:(j + 1) * par, :] = jnp.exp2((masked(n, j) - m_safe[n]).astype(BF16))
            for n in heads:
                vtc = jnp.concatenate([vbt_ref[n * hd:(n + 1) * hd, pl.ds(base, ck)], ones], axis=0)
                acc_ref[n] = alpha[n] * acc_ref[n] + jnp.dot(vtc, p_ref[n - n0], preferred_element_type=F32)
                m_ref[n] = m_new[n]
        return carry
    lax.fori_loop(0, n_chunks, attn_body, 0)

    ot = jnp.concatenate([acc_ref[n, :hd] / acc_ref[n, hd:hd + 1] for n in range(ATT_KV_HEADS)], axis=0)
    for g in range(ATT_GROUP):
        og = ot[:, g * qb:(g + 1) * qb].T
        for n in range(ATT_KV_HEADS):
            h = n * ATT_GROUP + g
            o_ref[:, h * ATT_HEAD_DIM:(h + 1) * ATT_HEAD_DIM] = (
                og[:, n * ATT_HEAD_DIM:(n + 1) * ATT_HEAD_DIM].astype(o_ref.dtype))


def _dsa_prompt(qt, qit, wit, kb, vbt, kib, b, n_sel):
    t = kb.shape[1]
    ck = min(512, t)
    nqb = t // Q_BLOCK
    idx_bits = max(1, math.ceil(math.log2(t))) + 1
    colblk = lambda n: pl.BlockSpec((n, Q_BLOCK), lambda bi, i: (0, bi * nqb + i))
    res = lambda n: pl.BlockSpec((1, t, n), lambda bi, i: (bi, 0, 0))
    gq = ATT_GROUP * Q_BLOCK
    return pl.pallas_call(
        functools.partial(_dsa_prompt_kernel, ck=ck, n_sel=n_sel, idx_bits=idx_bits),
        grid=(b, nqb),
        in_specs=[colblk(ATT_Q_DIM), colblk(IDX_HEADS * IDX_DIM), colblk(IDX_HEADS),
                  res(ATT_KV_DIM), pl.BlockSpec((ATT_KV_DIM, t), lambda bi, i: (0, bi)), res(IDX_DIM)],
        out_specs=pl.BlockSpec((Q_BLOCK, ATT_Q_DIM), lambda bi, i: (bi * nqb + i, 0)),
        out_shape=jax.ShapeDtypeStruct((b * t, ATT_Q_DIM), BF16),
        scratch_shapes=[pltpu.VMEM((t, Q_BLOCK), I32),
                        pltpu.VMEM((ATT_KV_HEADS, 1, gq), F32),
                        pltpu.VMEM((ATT_KV_HEADS, ATT_HEAD_DIM + BF16_ROWS, gq), F32),
                        pltpu.VMEM((2, ck, Q_BLOCK), BF16), pltpu.VMEM((ATT_KV_HEADS, ck, gq), F32),
                        pltpu.VMEM((2, ck, gq), BF16)],
        compiler_params=_cparams(("parallel", "arbitrary"), 56),
        name="dsa_prompt",
    )(qt, qit, wit, kb, vbt, kib)


def _outproj_ln_kernel(o_ref, h_ref, w_ref, g_ref, b_ref, out_ref):
    y = jnp.dot(o_ref[...], w_ref[...], preferred_element_type=F32)
    out_ref[...] = _layer_norm(DEEPNORM_ALPHA * h_ref[...] + y, g_ref[...], b_ref[...])


def _outproj_ln(o2d, h2d, w_out, g, b):
    m, kdim = o2d.shape
    tm = min(m, 512)
    row = lambda n: pl.BlockSpec((tm, n), lambda i: (i, 0))
    return pl.pallas_call(
        _outproj_ln_kernel,
        grid=(m // tm,),
        in_specs=[row(kdim), row(D_MODEL), pl.BlockSpec((kdim, D_MODEL), lambda i: (0, 0)),
                  pl.BlockSpec((1, D_MODEL), lambda i: (0, 0)), pl.BlockSpec((1, D_MODEL), lambda i: (0, 0))],
        out_specs=row(D_MODEL),
        out_shape=jax.ShapeDtypeStruct((m, D_MODEL), F32),
        compiler_params=_cparams(("parallel",)),
        name="outproj_ln",
    )(o2d, h2d, w_out.astype(BF16), g.reshape(1, -1), b.reshape(1, -1))


def _dsa_sample_kernel(pt_ref, qbd_ref, qi_ref, wi_ref, kn_ref, vn_ref, kin_ref,
                       cki_hbm, ck_hbm, cv_hbm, o_ref,
                       ki_buf, k_buf, v_buf, s_ref, keys_ref, cut_ref, sem,
                       *, n_pages, t_new, n_sel, idx_bits, layer_off, lchunk):
    b = pl.program_id(0)
    past = n_pages * PAGE_SIZE
    ltot = past + LANES
    rows = keys_ref.shape[0]

    def page_copy(hbm, buf, p, s):
        page = pt_ref[b, p] + layer_off
        return pltpu.make_async_copy(hbm.at[page],
                                     buf.at[:, pl.ds(pl.multiple_of(p * PAGE_SIZE, PAGE_SIZE), PAGE_SIZE)],
                                     sem.at[s])

    def start_all(p, c):
        page_copy(cki_hbm, ki_buf, p, 0).start()
        page_copy(ck_hbm, k_buf, p, 1).start()
        page_copy(cv_hbm, v_buf, p, 2).start()
        return c
    lax.fori_loop(0, n_pages, start_all, 0)

    def wait_all(hbm, buf, s):
        def body(p, c):
            page_copy(hbm, buf, p, s).wait()
            return c
        lax.fori_loop(0, n_pages, body, 0)

    qi = qi_ref[0]
    wcol = wi_ref[0]
    wait_all(cki_hbm, ki_buf, 0)

    def idx_scores(kic, dims=NN_DIMS):
        d = lax.dot_general(qi, kic, dims, preferred_element_type=F32)
        d = jnp.maximum(d, 0.0) * wcol
        return jnp.sum(d.reshape(t_new, IDX_HEADS, d.shape[1]), axis=1)

    keys_ref[...] = jnp.full(keys_ref.shape, INT_MIN, I32)
    for c in range(past // lchunk):
        sc = idx_scores(ki_buf[:, c * lchunk:(c + 1) * lchunk].astype(BF16))
        keys_ref[0:t_new, c * lchunk:(c + 1) * lchunk] = _sort_key(sc)
    scn = idx_scores(kin_ref[0], NT_DIMS)
    tpos = lax.broadcasted_iota(I32, (t_new, LANES), 0)
    lpos = lax.broadcasted_iota(I32, (t_new, LANES), 1)
    keys_ref[0:t_new, past:ltot] = jnp.where((lpos <= tpos) & (lpos < t_new), _sort_key(scn), INT_MIN)

    sck = max(w for w in range(LANES, 8 * LANES + 1, LANES) if ltot % w == 0)
    theta = _select_threshold(keys_ref, cut_ref, ltot // sck, sck, rows, n_sel, idx_bits)
    key = keys_ref[...]
    lane = lax.broadcasted_iota(I32, (rows, ltot), 1)
    sel = ((key > theta) | ((key == theta) & (lane < cut_ref[:, :1]))) & (key != INT_MIN)
    bias = jnp.where(sel, 0.0, NEG_INF)[0:t_new]
    bias = jnp.broadcast_to(bias[:, None, :], (t_new, ATT_HEADS, ltot)).reshape(t_new * ATT_HEADS, ltot)

    qbd = qbd_ref[0]
    wait_all(ck_hbm, k_buf, 1)
    for c in range(past // lchunk):
        kc = k_buf[:, c * lchunk:(c + 1) * lchunk].astype(BF16)
        s_ref[:, c * lchunk:(c + 1) * lchunk] = jnp.dot(qbd, kc, preferred_element_type=F32)
    s_ref[:, past:ltot] = lax.dot_general(qbd, kn_ref[0], NT_DIMS, preferred_element_type=F32)
    s = s_ref[...] + bias
    m = jnp.max(s, axis=1, keepdims=True)
    p = jnp.exp(s - m)
    linv = 1.0 / jnp.sum(p, axis=1, keepdims=True)
    s_ref[...] = p
    wait_all(cv_hbm, v_buf, 2)
    o = jnp.dot(s_ref[:, past:ltot].astype(BF16), vn_ref[0], preferred_element_type=F32)
    for c in range(past // lchunk):
        vc = v_buf[:, c * lchunk:(c + 1) * lchunk].astype(BF16)
        o = o + lax.dot_general(s_ref[:, c * lchunk:(c + 1) * lchunk].astype(BF16), vc, NT_DIMS,
                                preferred_element_type=F32)
    o = o * linv
    nrow = (lax.broadcasted_iota(I32, (t_new * ATT_HEADS, ATT_HEAD_DIM), 0) % ATT_HEADS) // ATT_GROUP
    out = jnp.zeros((t_new * ATT_HEADS, ATT_HEAD_DIM), F32)
    for n in range(ATT_KV_HEADS):
        out = out + jnp.where(nrow == n, o[:, n * ATT_HEAD_DIM:(n + 1) * ATT_HEAD_DIM], 0.0)
    o_ref[0] = out.astype(o_ref.dtype)


def _dsa_sample(page_table, qs, qis, wi, kb, vb, kib, cache_ki, cache_k, cache_v, layer_off, n_sel):
    b, t, _ = qs.shape
    n_pages = page_table.shape[1]
    past = n_pages * PAGE_SIZE
    ltot = past + LANES
    rows = 8
    q5 = qs.reshape(b, t, ATT_KV_HEADS, ATT_GROUP, 1, ATT_HEAD_DIM)
    eye = jnp.eye(ATT_KV_HEADS, dtype=qs.dtype).reshape(1, 1, ATT_KV_HEADS, 1, ATT_KV_HEADS, 1)
    qbd = (q5 * eye).reshape(b, t * ATT_HEADS, ATT_KV_DIM)
    qi = qis.reshape(b, t * IDX_HEADS, IDX_DIM)
    wcol = wi.reshape(b, t * IDX_HEADS, 1)
    padr = lambda a: jnp.pad(a, ((0, 0), (0, LANES - t), (0, 0)))
    kn, vn, kin = padr(kb), padr(vb), padr(kib)
    idx_bits = max(1, math.ceil(math.log2(ltot))) + 1
    lchunk = min(1024, past)
    bl = lambda a: pl.BlockSpec((1,) + a.shape[1:], lambda i, pt: (i, 0, 0))
    anyspec = pl.BlockSpec(memory_space=pl.ANY)
    grid_spec = pltpu.PrefetchScalarGridSpec(
        num_scalar_prefetch=1,
        grid=(b,),
        in_specs=[bl(qbd), bl(qi), bl(wcol), bl(kn), bl(vn), bl(kin), anyspec, anyspec, anyspec],
        out_specs=pl.BlockSpec((1, t * ATT_HEADS, ATT_HEAD_DIM), lambda i, pt: (i, 0, 0)),
        scratch_shapes=[pltpu.VMEM((IDX_DIM, past), F32), pltpu.VMEM((ATT_KV_DIM, past), F32),
                        pltpu.VMEM((ATT_KV_DIM, past), F32), pltpu.VMEM((t * ATT_HEADS, ltot), F32),
                        pltpu.VMEM((rows, ltot), I32), pltpu.VMEM((rows, LANES), I32),
                        pltpu.SemaphoreType.DMA((3,))],
    )
    o = pl.pallas_call(
        functools.partial(_dsa_sample_kernel, n_pages=n_pages, t_new=t, n_sel=n_sel, idx_bits=idx_bits,
                          layer_off=layer_off, lchunk=lchunk),
        grid_spec=grid_spec,
        out_shape=jax.ShapeDtypeStruct((b, t * ATT_HEADS, ATT_HEAD_DIM), BF16),
        compiler_params=_cparams(("arbitrary",), 56),
        name="dsa_sample",
    )(page_table, qbd, qi, wcol, kn, vn, kin, cache_ki, cache_k, cache_v)
    return o.reshape(b, t, ATT_Q_DIM)


def _gla_proj_kernel(x_ref, wq_ref, wk_ref, wv_ref, wg_ref, wr_ref, wup_ref, bg_ref,
                     q_ref, k_ref, v_ref, la_ref, r_ref):
    xb = x_ref[...].astype(BF16)
    q_ref[...] = jnp.dot(xb, wq_ref[...], preferred_element_type=F32) * (GLA_DK ** -0.5)
    k_ref[...] = jnp.dot(xb, wk_ref[...], preferred_element_type=F32)
    v_ref[...] = jnp.dot(xb, wv_ref[...], preferred_element_type=F32)
    r_ref[...] = jnp.dot(xb, wr_ref[...], preferred_element_type=F32)
    gd = jnp.dot(xb, wg_ref[...], preferred_element_type=F32)
    glogit = jnp.dot(gd.astype(BF16), wup_ref[...], preferred_element_type=F32) + bg_ref[...]
    la_ref[...] = jax.nn.log_sigmoid(glogit) / GLA_TAU


def _gla_proj(x2d, w_in, w_gate_up, b_gate):
    m = x2d.shape[0]
    tm = min(m, 512)
    dq = GLA_HEADS * GLA_DK
    dv = GLA_HEADS * GLA_DV
    wq = w_in[:, :dq].astype(BF16)
    wk = w_in[:, dq:2 * dq].astype(BF16)
    wv = w_in[:, 2 * dq:2 * dq + dv].astype(BF16)
    wg = jnp.pad(w_in[:, 2 * dq + dv:2 * dq + dv + GLA_GATE_RANK], ((0, 0), (0, LANES - GLA_GATE_RANK))).astype(BF16)
    wr = w_in[:, 2 * dq + dv + GLA_GATE_RANK:].astype(BF16)
    wup = jnp.pad(w_gate_up, ((0, LANES - GLA_GATE_RANK), (0, 0))).astype(BF16)
    row = lambda n: pl.BlockSpec((tm, n), lambda i: (i, 0))
    full = lambda a: pl.BlockSpec(a.shape, lambda i: (0, 0))
    bg = b_gate.reshape(1, -1)
    outs = [dq, dq, dv, dq, dv]
    return pl.pallas_call(
        _gla_proj_kernel,
        grid=(m // tm,),
        in_specs=[row(D_MODEL), full(wq), full(wk), full(wv), full(wg), full(wr), full(wup), full(bg)],
        out_specs=[row(n) for n in outs],
        out_shape=[jax.ShapeDtypeStruct((m, n), F32) for n in outs],
        compiler_params=_cparams(("parallel",)),
        name="gla_proj",
    )(x2d, wq, wk, wv, wg, wr, wup, bg)


def _gla_chunk(q, k, v, g, st, c):
    sub = GLA_SUB
    n_sub = c // sub
    ri = lax.broadcasted_iota(I32, (c, c), 0)
    ci = lax.broadcasted_iota(I32, (c, c), 1)
    tri = jnp.where(ci <= ri, 1.0, 0.0).astype(F32)
    bc = jnp.dot(tri, g, preferred_element_type=F32, precision=lax.Precision.HIGHEST)
    o = lax.dot_general((q * jnp.exp(bc)).astype(BF16), st.astype(BF16), NT_DIMS, preferred_element_type=F32)
    if n_sub > 1:
        mrow = jnp.concatenate([jnp.zeros((sub, GLA_DK), F32)] +
                               [jnp.broadcast_to(bc[i * sub - 1:i * sub], (sub, GLA_DK)) for i in range(1, n_sub)], axis=0)
        qh = (q * jnp.exp(bc - mrow)).astype(BF16)
        parts = [jnp.zeros((sub, GLA_DV), F32)]
        for i in range(1, n_sub):
            kh = (k[:i * sub] * jnp.exp(bc[i * sub - 1:i * sub] - bc[:i * sub])).astype(BF16)
            a = lax.dot_general(qh[i * sub:(i + 1) * sub], kh, NT_DIMS, preferred_element_type=F32)
            parts.append(jnp.dot(a.astype(BF16), v[:i * sub].astype(BF16), preferred_element_type=F32))
        o = o + jnp.concatenate(parts, axis=0)
    rsub = lax.broadcasted_iota(I32, (c, 1), 0) % sub
    for dlt in range(sub):
        ks = k if dlt == 0 else pltpu.roll(k, dlt, 0)
        bs = bc if dlt == 0 else pltpu.roll(bc, dlt, 0)
        vs = v if dlt == 0 else pltpu.roll(v, dlt, 0)
        w = jnp.sum(q * ks * jnp.exp(jnp.minimum(bc - bs, 0.0)), axis=1, keepdims=True)
        o = o + jnp.where(rsub >= dlt, w, 0.0) * vs
    bl = bc[c - 1:c]
    kt = (k * jnp.exp(bl - bc)).astype(BF16)
    st_new = st * jnp.exp(bl) + lax.dot_general(v.astype(BF16), kt, TN_DIMS, preferred_element_type=F32)
    return o, st_new


def _gla_kernel(q_ref, k_ref, v_ref, g_ref, r_ref, ng_ref, s0_ref, o_ref, sf_ref, st_ref, *, c, n_c):
    tb = pl.program_id(2)

    @pl.when(tb == 0)
    def _():
        st_ref[...] = s0_ref[0, 0].T

    st = st_ref[...]
    for ci in range(n_c):
        sl = slice(ci * c, (ci + 1) * c)
        o, st = _gla_chunk(q_ref[0, sl, :], k_ref[0, sl, :], v_ref[0, sl, :], g_ref[0, sl, :], st, c)
        o = o * lax.rsqrt(jnp.mean(o * o, axis=-1, keepdims=True) + LN_EPS) * ng_ref[...]
        o_ref[0, sl, :] = (o * jax.nn.silu(r_ref[0, sl, :])).astype(o_ref.dtype)
    st_ref[...] = st

    @pl.when(tb == pl.num_programs(2) - 1)
    def _():
        sf_ref[0, 0] = st.T


def _gla(q, k, v, la, r, norm_g, s0):
    b, t, _ = q.shape
    c = math.gcd(t, GLA_CHUNK)
    tblk = math.gcd(t, 8 * GLA_CHUNK)
    n_c = tblk // c
    kspec = pl.BlockSpec((1, tblk, GLA_DK), lambda bi, h, i: (bi, i, h))
    vspec = pl.BlockSpec((1, tblk, GLA_DV), lambda bi, h, i: (bi, i, h))
    sspec = pl.BlockSpec((1, 1, GLA_DK, GLA_DV), lambda bi, h, i: (bi, h, 0, 0))
    return pl.pallas_call(
        functools.partial(_gla_kernel, c=c, n_c=n_c),
        grid=(b, GLA_HEADS, t // tblk),
        in_specs=[kspec, kspec, vspec, kspec, vspec, pl.BlockSpec((1, GLA_DV), lambda bi, h, i: (0, 0)), sspec],
        out_specs=[vspec, sspec],
        out_shape=[jax.ShapeDtypeStruct((b, t, GLA_HEADS * GLA_DV), BF16),
                   jax.ShapeDtypeStruct((b, GLA_HEADS, GLA_DK, GLA_DV), F32)],
        scratch_shapes=[pltpu.VMEM((GLA_DV, GLA_DK), F32)],
        compiler_params=_cparams(("parallel", "parallel", "arbitrary")),
        name="gla_scan",
    )(q, k, v, la, r, norm_g.reshape(1, -1), s0)


def _gla_mixer(x, s0, w_in, w_gate_up, b_gate, norm_g):
    b, t, _ = x.shape
    q, k, v, la, r = _gla_proj(x.reshape(b * t, D_MODEL), w_in, w_gate_up, b_gate)
    tp = -(-t // GLA_SUB) * GLA_SUB
    r3 = lambda a: jnp.pad(a.reshape(b, t, -1), ((0, 0), (0, tp - t), (0, 0)))
    o, sf = _gla(r3(q), r3(k), r3(v), r3(la), r3(r), norm_g, s0)
    return o[:, :t], sf


def _top_values(s, k):
    vals = []
    cur = s
    for _ in range(k):
        mx = jnp.max(cur, axis=0, keepdims=True)
        vals.append(mx)
        cur = jnp.where(cur == mx, NEG_INF, cur)
    return vals


def _top_values_ranked(s, k):
    vals = []
    cur = s
    rank = jnp.full(s.shape, float(k), F32)
    for i in range(k):
        mx = jnp.max(cur, axis=0, keepdims=True)
        vals.append(mx)
        top = cur == mx
        rank = jnp.where(top, float(i), rank)
        cur = jnp.where(top, NEG_INF, cur)
    return vals, rank


def _peer_select_kernel(x_ref, wq_ref, k1_ref, k2_ref, c1_ref, p_ref, r2_ref, e_ref):
    xb = x_ref[...].astype(BF16)
    q = jnp.dot(xb, wq_ref[...], preferred_element_type=F32).astype(BF16)
    tt = q.shape[0]
    k1 = k1_ref[...]
    k2 = k2_ref[...]
    sub8 = lax.broadcasted_iota(I32, (8, tt), 0)
    for h in range(PEER_HEADS):
        s1 = lax.dot_general(k1, q[:, h * PEER_D_KEY:h * PEER_D_KEY + PEER_HALF], NT_DIMS,
                             preferred_element_type=F32)
        s2 = lax.dot_general(k2, q[:, h * PEER_D_KEY + PEER_HALF:(h + 1) * PEER_D_KEY], NT_DIMS,
                             preferred_element_type=F32)
        t1, rank1 = _top_values_ranked(s1, PEER_TOPK)
        t2, rank2 = _top_values_ranked(s2, PEER_TOPK)
        t2a = jnp.concatenate(t2, axis=0)
        cands = [t1[0] + t2a]
        for a in range(1, PEER_TOPK):
            nb = PEER_TOPK // (a + 1)
            cands.append(jnp.where(sub8 < nb, t1[a] + t2a[:8], NEG_INF))
        best = _top_values(jnp.concatenate(cands, axis=0), PEER_TOPK)
        theta = best[PEER_TOPK - 1]
        z = best[0] * 0.0
        for bv in best:
            z = z + jnp.exp(bv - best[0])
        count1 = jnp.zeros(s1.shape, F32)
        for a in range(PEER_TOPK):
            cnt_a = jnp.sum(jnp.where(t1[a] + t2a >= theta, 1.0, 0.0), axis=0, keepdims=True)
            count1 = jnp.where(rank1 == float(a), cnt_a, count1)
        c1_ref[h] = count1
        p_ref[h] = jnp.exp(s1 - t1[0]) * (0.5 / z)
        r2_ref[h] = rank2
        e_ref[h] = jnp.exp(s2 - t2[0])


def _peer_select(x2d, w_q, keys1, keys2):
    n = x2d.shape[0]
    tt = min(n, 256)
    wq = w_q.astype(BF16)
    k1 = keys1.astype(BF16)
    k2 = keys2.astype(BF16)
    full = lambda a: pl.BlockSpec(a.shape, lambda i: (0,) * a.ndim)
    hspec = pl.BlockSpec((PEER_HEADS, PEER_N_KEYS, tt), lambda i: (0, 0, i))
    big = lambda dt: jax.ShapeDtypeStruct((PEER_HEADS, PEER_N_KEYS, n), dt)
    return pl.pallas_call(
        _peer_select_kernel,
        grid=(n // tt,),
        in_specs=[pl.BlockSpec((tt, D_MODEL), lambda i: (i, 0)), full(wq), full(k1), full(k2)],
        out_specs=[hspec, hspec, hspec, hspec],
        out_shape=[big(F32), big(F32), big(F32), big(F32)],
        compiler_params=_cparams(("parallel",)),
        name="peer_select",
    )(x2d, wq, k1, k2)


def _peer_mix_kernel(h_ref, c1_ref, p_ref, r2_ref, e_ref, u_ref, vt_ref, g_ref, bb_ref, out_ref,
                     xb_ref, ht_ref, wt_ref, yt_ref, *, n_i1):
    eb = pl.program_id(1)
    tt = h_ref.shape[0]

    @pl.when(eb == 0)
    def _():
        xb_ref[...] = h_ref[...].T.astype(BF16)
        yt_ref[...] = jnp.zeros(yt_ref.shape, F32)

    ht_ref[...] = jnp.dot(u_ref[...], xb_ref[...], preferred_element_type=F32)

    i1_base = pl.multiple_of(eb * n_i1, n_i1)

    for lg in range(tt // LANES):
        ls = slice(lg * LANES, (lg + 1) * LANES)
        c8 = [c1_ref[h, pl.ds(i1_base, n_i1), ls] for h in range(PEER_HEADS)]
        p8 = [p_ref[h, pl.ds(i1_base, n_i1), ls] for h in range(PEER_HEADS)]
        for j in range(n_i1):
            cb = [jnp.broadcast_to(c8[h][j:j + 1], (BF16_ROWS, LANES)) for h in range(PEER_HEADS)]
            pb = [jnp.broadcast_to(p8[h][j:j + 1], (BF16_ROWS, LANES)) for h in range(PEER_HEADS)]
            for r in range(PEER_N_KEYS // BF16_ROWS):
                ks = slice(r * BF16_ROWS, (r + 1) * BF16_ROWS)
                acc = jnp.zeros((BF16_ROWS, LANES), F32)
                for h in range(PEER_HEADS):
                    acc = acc + jnp.where(r2_ref[h, ks, ls] < cb[h], pb[h] * e_ref[h, ks, ls], 0.0)
                rs = slice(j * PEER_N_KEYS + r * BF16_ROWS, j * PEER_N_KEYS + (r + 1) * BF16_ROWS)
                x = ht_ref[rs, ls]
                gelu2 = x * (1.0 + lax.erf(x * (2.0 ** -0.5)))
                wt_ref[rs, ls] = (gelu2 * acc).astype(BF16)

    yt_ref[...] += jnp.dot(vt_ref[...], wt_ref[...], preferred_element_type=F32)

    @pl.when(eb == pl.num_programs(1) - 1)
    def _():
        y = yt_ref[...].T
        out_ref[...] = _layer_norm(DEEPNORM_ALPHA * h_ref[...] + y, g_ref[...], bb_ref[...])


def _block_transpose_kernel(v_ref, o_ref):
    o_ref[...] = v_ref[...].T.astype(o_ref.dtype)


def _value_blocks(v, eblk):
    n_e, d = v.shape
    return pl.pallas_call(
        _block_transpose_kernel,
        grid=(n_e // eblk,),
        in_specs=[pl.BlockSpec((eblk, d), lambda j: (j, 0))],
        out_specs=pl.BlockSpec((None, d, eblk), lambda j: (j, 0, 0)),
        out_shape=jax.ShapeDtypeStruct((n_e // eblk, d, eblk), BF16),
        compiler_params=_cparams(("parallel",)),
        name="value_blocks",
    )(v)


def _peer_ln(h2d, w_q, keys1, keys2, ub, vt, g, b):
    n_real = h2d.shape[0]
    if n_real % LANES:
        h2d = jnp.pad(h2d, ((0, LANES - n_real % LANES), (0, 0)))
        return _peer_ln(h2d, w_q, keys1, keys2, ub, vt, g, b)[:n_real]
    n = n_real
    c1, p, r2, e = _peer_select(h2d, w_q, keys1, keys2)
    tt = min(n, 512)
    n_blk, _, eblk = vt.shape
    hspec = pl.BlockSpec((PEER_HEADS, PEER_N_KEYS, tt), lambda i, j: (0, 0, i))
    vec = pl.BlockSpec((1, D_MODEL), lambda i, j: (0, 0))
    return pl.pallas_call(
        functools.partial(_peer_mix_kernel, n_i1=eblk // PEER_N_KEYS),
        grid=(n // tt, n_blk),
        in_specs=[pl.BlockSpec((tt, D_MODEL), lambda i, j: (i, 0)), hspec, hspec, hspec, hspec,
                  pl.BlockSpec((eblk, D_MODEL), lambda i, j: (j, 0)),
                  pl.BlockSpec((None, D_MODEL, eblk), lambda i, j: (j, 0, 0)), vec, vec],
        out_specs=pl.BlockSpec((tt, D_MODEL), lambda i, j: (i, 0)),
        out_shape=jax.ShapeDtypeStruct((n, D_MODEL), F32),
        scratch_shapes=[pltpu.VMEM((D_MODEL, tt), BF16), pltpu.VMEM((eblk, tt), F32),
                        pltpu.VMEM((eblk, tt), BF16), pltpu.VMEM((D_MODEL, tt), F32)],
        compiler_params=_cparams(("parallel", "arbitrary"), 56),
        name="peer_mix",
    )(h2d, c1, p, r2, e, ub, vt, g.reshape(1, -1), b.reshape(1, -1))


def kernel(x_prompt, x_sample, cache_k, cache_v, cache_kidx, state_gla, page_table,
           attn_w_in, attn_w_out, gla_w_in, gla_w_gate_up, gla_b_gate, gla_norm_g, gla_w_out,
           peer_w_q, peer_keys1, peer_keys2, peer_u, peer_v, ln1_g, ln1_b, ln2_g, ln2_b):
    bp, tp, _ = x_prompt.shape
    bs, ts, _ = x_sample.shape
    n_pool = cache_k.shape[1]
    past = page_table.shape[1] * PAGE_SIZE
    ck = jnp.swapaxes(cache_k.reshape(-1, PAGE_SIZE, ATT_KV_DIM), 1, 2)
    cv = jnp.swapaxes(cache_v.reshape(-1, PAGE_SIZE, ATT_KV_DIM), 1, 2)
    cki = jnp.swapaxes(cache_kidx.reshape(-1, PAGE_SIZE, IDX_DIM), 1, 2)
    hp = x_prompt.reshape(bp * tp, D_MODEL)
    hs = x_sample.reshape(bs * ts, D_MODEL)
    kp_l, vp_l, kip_l, sp_l = [], [], [], []
    ks_l, vs_l, kis_l, ss_l = [], [], [], []
    for i in range(DEPTH):
        j = i // 2
        if i % 2 == 0:
            k, v, ki, kb, kib, qt, qit, wit, vbt = _attn_proj_t(hp, attn_w_in[j])
            r3 = lambda a: a.reshape(bp, tp, -1)
            op = _dsa_prompt(qt, qit, wit, r3(kb), vbt, r3(kib), bp, min(TOPK_MAX, tp // 4))
            kp_l.append(k.reshape(bp, tp, ATT_KV_HEADS, ATT_HEAD_DIM))
            vp_l.append(v.reshape(bp, tp, ATT_KV_HEADS, ATT_HEAD_DIM))
            kip_l.append(ki.reshape(bp, tp, IDX_DIM))
            qs, k, v, kb, vb, qis, ki, kib, wi = _attn_proj(hs, attn_w_in[j])
            r3 = lambda a: a.reshape(bs, ts, -1)
            os_ = _dsa_sample(page_table, r3(qs), r3(qis), r3(wi), r3(kb), r3(vb), r3(kib), cki, ck, cv,
                              j * n_pool, min(TOPK_MAX, (past + ts) // 4))
            ks_l.append(k.reshape(bs, ts, ATT_KV_HEADS, ATT_HEAD_DIM))
            vs_l.append(v.reshape(bs, ts, ATT_KV_HEADS, ATT_HEAD_DIM))
            kis_l.append(ki.reshape(bs, ts, IDX_DIM))
            w_out = attn_w_out[j]
        else:
            s0 = jnp.zeros((bp, GLA_HEADS, GLA_DK, GLA_DV), F32)
            op, sp = _gla_mixer(hp.reshape(bp, tp, D_MODEL), s0, gla_w_in[j], gla_w_gate_up[j], gla_b_gate[j],
                                gla_norm_g[j])
            os_, ss = _gla_mixer(hs.reshape(bs, ts, D_MODEL), state_gla[j], gla_w_in[j], gla_w_gate_up[j],
                                 gla_b_gate[j], gla_norm_g[j])
            sp_l.append(sp)
            ss_l.append(ss)
            w_out = gla_w_out[j]
        hp = _outproj_ln(op.reshape(bp * tp, -1), hp, w_out, ln1_g[i], ln1_b[i])
        hs = _outproj_ln(os_.reshape(bs * ts, -1), hs, w_out, ln1_g[i], ln1_b[i])
        pw = (peer_w_q[i], peer_keys1[i], peer_keys2[i], peer_u[i].astype(BF16),
              _value_blocks(peer_v[i], PEER_EXPERT_BLOCK), ln2_g[i], ln2_b[i])
        hp = _peer_ln(hp, *pw)
        hs = _peer_ln(hs, *pw)
    return (hp.reshape(bp, tp, D_MODEL), hs.reshape(bs, ts, D_MODEL),
            jnp.stack(kp_l), jnp.stack(vp_l), jnp.stack(kip_l), jnp.stack(sp_l),
            jnp.stack(ks_l), jnp.stack(vs_l), jnp.stack(kis_l), jnp.stack(ss_l))
```

```python
import functools
import math

import jax
import jax.numpy as jnp
from jax import lax
from jax.experimental import pallas as pl
from jax.experimental.pallas import tpu as pltpu

F32 = jnp.float32
BF16 = jnp.bfloat16
I32 = jnp.int32

D_MODEL = 1024
DEPTH = 2
PAGE_SIZE = 128
DEEPNORM_ALPHA = (2.0 * DEPTH) ** 0.25
LN_EPS = 1e-5

ATT_HEADS = 16
ATT_KV_HEADS = 4
ATT_HEAD_DIM = 64
ATT_GROUP = ATT_HEADS // ATT_KV_HEADS
IDX_HEADS = 8
IDX_DIM = 64
TOPK_MAX = 256
Q_BLOCK = 128
ATT_Q_DIM = ATT_HEADS * ATT_HEAD_DIM
ATT_KV_DIM = ATT_KV_HEADS * ATT_HEAD_DIM

GLA_HEADS = 4
GLA_DK = D_MODEL // 2 // GLA_HEADS
GLA_DV = D_MODEL // GLA_HEADS
GLA_GATE_RANK = 16
GLA_TAU = 16.0
GLA_CHUNK = 64
GLA_SUB = 16

PEER_HEADS = 8
PEER_N_KEYS = 128
PEER_D_KEY = 256
PEER_HALF = PEER_D_KEY // 2
PEER_TOPK = 16
PEER_EXPERT_BLOCK = 8 * PEER_N_KEYS

LANES = 128
BF16_ROWS = 16
LOG2_E = 1.4426950408889634
INT_MIN = -(2 ** 31)
NEG_INF = float("-inf")
MASKED = -1e30

NN_DIMS = (((1,), (0,)), ((), ()))
NT_DIMS = (((1,), (1,)), ((), ()))
TN_DIMS = (((0,), (0,)), ((), ()))


def _cparams(sem, vmem_mib=None, flags=None):
    kw = dict(dimension_semantics=sem)
    if vmem_mib is not None:
        kw["vmem_limit_bytes"] = vmem_mib * 1024 * 1024
    if flags:
        kw["flags"] = flags
    return pltpu.CompilerParams(**kw)


def _sort_key(x):
    b = pltpu.bitcast(x + 0.0, I32)
    return b ^ ((b >> 31) & 0x7FFFFFFF)


def _layer_norm(z, g, b):
    mu = jnp.mean(z, axis=-1, keepdims=True)
    zc = z - mu
    var = jnp.mean(zc * zc, axis=-1, keepdims=True)
    return zc * lax.rsqrt(var + LN_EPS) * g + b


def _row_count(keys_ref, n_chunks, ck, rows, pred):
    def body(c, acc):
        base = pl.multiple_of(c * ck, ck)
        return acc + jnp.where(pred(keys_ref[:, pl.ds(base, ck)], base), 1.0, 0.0)
    acc = lax.fori_loop(0, n_chunks, body, jnp.zeros((rows, ck), F32))
    return jnp.sum(acc, axis=1, keepdims=True)


def _select_threshold(keys_ref, cut_ref, n_chunks, ck, rows, kth, idx_bits):
    kth_f = float(kth)

    def count_ge(t):
        tb = jnp.broadcast_to(t, (rows, ck))
        return _row_count(keys_ref, n_chunks, ck, rows, lambda blk, base: blk >= tb)

    theta = jnp.full((rows, 1), INT_MIN, I32)
    zero = jnp.zeros((rows, 1), I32)
    theta = jnp.where(count_ge(zero) >= kth_f, zero, theta)

    def bit_body(i, t):
        cand = t | lax.shift_left(jnp.int32(1), jnp.int32(30) - i)
        return jnp.where(count_ge(cand) >= kth_f, cand, t)
    theta = lax.fori_loop(0, 31, bit_body, theta)

    n_gt = count_ge(theta + 1)
    n_ge = count_ge(theta)
    need = kth_f - n_gt
    ambiguous = (n_ge - n_gt > need) & (theta > INT_MIN)
    cut_ref[...] = jnp.full(cut_ref.shape, 2 ** idx_bits, I32)

    @pl.when(jnp.max(jnp.where(ambiguous, 1.0, 0.0)) > 0.5)
    def _():
        thb = jnp.broadcast_to(theta, (rows, ck))
        lane = lax.broadcasted_iota(I32, (rows, ck), 1)

        def ties_before(jc):
            jb = jnp.broadcast_to(jc, (rows, ck))
            return _row_count(keys_ref, n_chunks, ck, rows,
                              lambda blk, base: (blk == thb) & (lane + base < jb))

        def jbit(i, jcur):
            cand = jcur | lax.shift_left(jnp.int32(1), jnp.int32(idx_bits - 1) - i)
            return jnp.where(ties_before(cand) <= need, cand, jcur)
        jfin = lax.fori_loop(0, idx_bits, jbit, jnp.zeros((rows, 1), I32))
        jfin = jnp.where(ambiguous, jfin, 2 ** idx_bits)
        cut_ref[...] = jnp.broadcast_to(jfin, cut_ref.shape)
    return theta


def _attn_proj_kernel(x_ref, wq_ref, wkv_ref, wqi_ref, wkw_ref,
                      q_ref, k_ref, v_ref, kb_ref, vb_ref, qi_ref, ki_ref, kib_ref, wi_ref):
    xb = x_ref[...].astype(BF16)
    q = jnp.dot(xb, wq_ref[...], preferred_element_type=F32)
    q_ref[...] = (q * (ATT_HEAD_DIM ** -0.5)).astype(BF16)
    kv = jnp.dot(xb, wkv_ref[...], preferred_element_type=F32)
    k = kv[:, :ATT_KV_DIM]
    v = kv[:, ATT_KV_DIM:]
    k_ref[...] = k
    v_ref[...] = v
    kb_ref[...] = k.astype(BF16)
    vb_ref[...] = v.astype(BF16)
    qi = jnp.dot(xb, wqi_ref[...], preferred_element_type=F32)
    qi_ref[...] = (qi * (IDX_DIM ** -0.5)).astype(BF16)
    kw = jnp.dot(xb, wkw_ref[...], preferred_element_type=F32)
    ki = kw[:, :IDX_DIM]
    ki_ref[...] = ki
    kib_ref[...] = ki.astype(BF16)
    wi_ref[...] = kw[:, IDX_DIM:IDX_DIM + IDX_HEADS] * (IDX_HEADS ** -0.5)


def _attn_weights(w_in):
    s0 = ATT_Q_DIM
    s2 = s0 + 2 * ATT_KV_DIM
    s3 = s2 + IDX_HEADS * IDX_DIM
    wkw = jnp.pad(w_in[:, s3:], ((0, 0), (0, LANES - (w_in.shape[1] - s3))))
    return (w_in[:, :s0].astype(BF16), w_in[:, s0:s2].astype(BF16), w_in[:, s2:s3].astype(BF16),
            wkw.astype(BF16))


def _attn_proj(x2d, w_in):
    m = x2d.shape[0]
    tm = min(m, 512)
    wq, wkv, wqi, wkw = _attn_weights(w_in)
    row = lambda n: pl.BlockSpec((tm, n), lambda i: (i, 0))
    full = lambda a: pl.BlockSpec(a.shape, lambda i: (0, 0))
    outs = [(ATT_Q_DIM, BF16), (ATT_KV_DIM, F32), (ATT_KV_DIM, F32), (ATT_KV_DIM, BF16),
            (ATT_KV_DIM, BF16), (IDX_HEADS * IDX_DIM, BF16), (IDX_DIM, F32), (IDX_DIM, BF16),
            (IDX_HEADS, F32)]
    return pl.pallas_call(
        _attn_proj_kernel,
        grid=(m // tm,),
        in_specs=[row(D_MODEL), full(wq), full(wkv), full(wqi), full(wkw)],
        out_specs=[row(n) for n, _ in outs],
        out_shape=[jax.ShapeDtypeStruct((m, n), dt) for n, dt in outs],
        compiler_params=_cparams(("parallel",)),
        name="attn_proj",
    )(x2d, wq, wkv, wqi, wkw)


def _attn_proj_t_kernel(x_ref, wq_ref, wkv_ref, wqi_ref, wkw_ref,
                        k_ref, v_ref, ki_ref, kb_ref, kib_ref, qt_ref, qit_ref, wit_ref, vbt_ref):
    xb = x_ref[...].astype(BF16)
    q = jnp.dot(xb, wq_ref[...], preferred_element_type=F32)
    qt_ref[...] = (q * (ATT_HEAD_DIM ** -0.5 * LOG2_E)).T.astype(BF16)
    kv = jnp.dot(xb, wkv_ref[...], preferred_element_type=F32)
    k = kv[:, :ATT_KV_DIM]
    v = kv[:, ATT_KV_DIM:]
    k_ref[...] = k
    v_ref[...] = v
    kb_ref[...] = k.astype(BF16)
    vbt_ref[...] = v.T.astype(BF16)
    qi = jnp.dot(xb, wqi_ref[...], preferred_element_type=F32)
    qit_ref[...] = (qi * (IDX_DIM ** -0.5)).T.astype(BF16)
    kw = jnp.dot(xb, wkw_ref[...], preferred_element_type=F32)
    ki = kw[:, :IDX_DIM]
    ki_ref[...] = ki
    kib_ref[...] = ki.astype(BF16)
    wit_ref[...] = kw.T[IDX_DIM:IDX_DIM + IDX_HEADS, :] * (IDX_HEADS ** -0.5)


def _attn_proj_t(x2d, w_in):
    m = x2d.shape[0]
    tm = min(m, 512)
    wq, wkv, wqi, wkw = _attn_weights(w_in)
    row = lambda n: pl.BlockSpec((tm, n), lambda i: (i, 0))
    col = lambda n: pl.BlockSpec((n, tm), lambda i: (0, i))
    full = lambda a: pl.BlockSpec(a.shape, lambda i: (0, 0))
    nat = [(ATT_KV_DIM, F32), (ATT_KV_DIM, F32), (IDX_DIM, F32), (ATT_KV_DIM, BF16), (IDX_DIM, BF16)]
    tr = [(ATT_Q_DIM, BF16), (IDX_HEADS * IDX_DIM, BF16), (IDX_HEADS, F32), (ATT_KV_DIM, BF16)]
    return pl.pallas_call(
        _attn_proj_t_kernel,
        grid=(m // tm,),
        in_specs=[row(D_MODEL), full(wq), full(wkv), full(wqi), full(wkw)],
        out_specs=[row(n) for n, _ in nat] + [col(n) for n, _ in tr],
        out_shape=[jax.ShapeDtypeStruct((m, n), dt) for n, dt in nat] +
                  [jax.ShapeDtypeStruct((n, m), dt) for n, dt in tr],
        compiler_params=_cparams(("parallel",)),
        name="attn_proj_t",
    )(x2d, wq, wkv, wqi, wkw)


def _col_count(keys_ref, n_chunks, ck, pred):
    width = keys_ref.shape[1]
    par = math.gcd(ck, 64)

    def body(c, acc):
        base = pl.multiple_of(c * ck, ck)
        hit = pred(keys_ref[pl.ds(base, ck), :], base)
        return acc + jnp.sum(jnp.where(hit, 1.0, 0.0).reshape(ck // par, par, width), axis=0)
    acc = lax.fori_loop(0, n_chunks, body, jnp.zeros((par, width), F32))
    return jnp.sum(acc, axis=0, keepdims=True)


def _select_threshold_cols(keys_ref, n_chunks, ck, kth, idx_bits):
    width = keys_ref.shape[1]
    kth_f = float(kth)

    def count_ge(t):
        return _col_count(keys_ref, n_chunks, ck, lambda blk, base: blk >= t)

    theta = jnp.full((1, width), INT_MIN, I32)
    zero = jnp.zeros((1, width), I32)
    theta = jnp.where(count_ge(zero) >= kth_f, zero, theta)

    def bit_body(i, t):
        cand = t | lax.shift_left(jnp.int32(1), jnp.int32(30) - i)
        return jnp.where(count_ge(cand) >= kth_f, cand, t)
    theta = lax.fori_loop(0, 31, bit_body, theta)

    n_gt = count_ge(theta + 1)
    n_ge = count_ge(theta)
    need = kth_f - n_gt
    ambiguous = (n_ge - n_gt > need) & (theta > INT_MIN)
    no_cut = jnp.full((1, width), 2 ** idx_bits, I32)

    def tie_search():
        row = lax.broadcasted_iota(I32, (ck, width), 0)

        def ties_before(jc):
            return _col_count(keys_ref, n_chunks, ck, lambda blk, base: (blk == theta) & (row + base < jc))

        def jbit(i, jcur):
            cand = jcur | lax.shift_left(jnp.int32(1), jnp.int32(idx_bits - 1) - i)
            return jnp.where(ties_before(cand) <= need, cand, jcur)
        jfin = lax.fori_loop(0, idx_bits, jbit, jnp.zeros((1, width), I32))
        return jnp.where(ambiguous, jfin, no_cut)

    cut = lax.cond(jnp.max(jnp.where(ambiguous, 1.0, 0.0)) > 0.5, tie_search, lambda: no_cut)
    return theta, cut


def _dsa_prompt_kernel(qt_ref, qit_ref, wit_ref, kb_ref, vbt_ref, kib_ref, o_ref,
                       keys_ref, m_ref, acc_ref, bias_ref, s_ref, p_ref, *, ck, n_sel, idx_bits):
    i = pl.program_id(1)
    qb = Q_BLOCK
    n_chunks = ((i + 1) * qb + ck - 1) // ck
    q_pos = i * qb + lax.broadcasted_iota(I32, (ck, qb), 1)
    row = lax.broadcasted_iota(I32, (ck, qb), 0)

    qit = qit_ref[...]
    qi_all = jnp.concatenate([qit[h * IDX_DIM:(h + 1) * IDX_DIM, :] for h in range(IDX_HEADS)], axis=1)
    wit = wit_ref[...]

    def score_body(c, carry):
        base = pl.multiple_of(c * ck, ck)
        d = jnp.dot(kib_ref[0, pl.ds(base, ck), :], qi_all, preferred_element_type=F32)
        acc = jnp.zeros((ck, qb), F32)
        for h in range(IDX_HEADS):
            acc = acc + jnp.maximum(d[:, h * qb:(h + 1) * qb], 0.0) * wit[h:h + 1, :]
        keys_ref[pl.ds(base, ck), :] = jnp.where(row + base <= q_pos, _sort_key(acc), INT_MIN)
        return carry
    lax.fori_loop(0, n_chunks, score_body, 0)

    theta, cut = _select_threshold_cols(keys_ref, n_chunks, ck, n_sel, idx_bits)

    qt = qt_ref[...]
    qn = [jnp.concatenate([qt[(n * ATT_GROUP + g) * ATT_HEAD_DIM:(n * ATT_GROUP + g + 1) * ATT_HEAD_DIM, :]
                           for g in range(ATT_GROUP)], axis=1) for n in range(ATT_KV_HEADS)]
    m_ref[...] = jnp.full(m_ref.shape, NEG_INF, F32)
    acc_ref[...] = jnp.zeros(acc_ref.shape, F32)
    hd = ATT_HEAD_DIM
    ones = jnp.ones((acc_ref.shape[1] - hd, ck), BF16)
    par = math.gcd(ck, 64)

    n_sub = ck // par
    pair = 2

    eye = (lax.broadcasted_iota(I32, (qb, ATT_GROUP * qb), 1) % qb ==
           lax.broadcasted_iota(I32, (qb, ATT_GROUP * qb), 0))
    eye = jnp.where(eye, 1.0, 0.0).astype(BF16)
    qa = [jnp.concatenate([eye, qn[n]], axis=0) for n in range(ATT_KV_HEADS)]

    def mask_rows(kbase, slot):
        key = keys_ref[pl.ds(kbase, ck), :]
        sel = ((key > theta) | ((key == theta) & (row + kbase < cut))) & (key != INT_MIN)
        bias_ref[slot] = jnp.where(sel, 0.0, MASKED).astype(BF16)

    def scores(kbase, slot, n):
        kc = jnp.concatenate([bias_ref[slot], kb_ref[0, pl.ds(kbase, ck), n * hd:(n + 1) * hd]], axis=1)
        s_ref[n] = jnp.dot(kc, qa[n], preferred_element_type=F32)
    mask_rows(0, 0)
    for n in range(pair):
        scores(0, 0, n)

    def attn_body(c, carry):
        base = pl.multiple_of(c * ck, ck)
        slot = c % 2

        for n0 in range(0, ATT_KV_HEADS, pair):
            heads = range(n0, n0 + pair)
            if n0 + pair < ATT_KV_HEADS:
                for n in heads:
                    scores(base, slot, n + pair)
            else:
                nbase = pl.multiple_of(jnp.minimum(c + 1, n_chunks - 1) * ck, ck)
                mask_rows(nbase, 1 - slot)
                for n in heads:
                    scores(nbase, 1 - slot, n - n0)

            def masked(n, j):
                return s_ref[n, j * par:(j + 1) * par, :]
            mx = {n: masked(n, 0) for n in heads}
            for j in range(1, n_sub):
                for n in heads:
                    mx[n] = jnp.maximum(mx[n], masked(n, j))
            m_safe, alpha, m_new = {}, {}, {}
            for n in heads:
                m_prev = m_ref[n]
                m_new[n] = jnp.maximum(m_prev, jnp.max(mx[n], axis=0, keepdims=True))
                m_safe[n] = jnp.where(m_new[n] == NEG_INF, 0.0, m_new[n])
                alpha[n] = jnp.exp2(m_prev - m_safe[n])
            for j in range(n_sub):
                for n in heads:
                    p_ref[n - n0, j * par:(j + 1) * par, :] = jnp.exp2((masked(n, j) - m_safe[n]).astype(BF16))
            for n in heads:
                vtc = jnp.concatenate([vbt_ref[n * hd:(n + 1) * hd, pl.ds(base, ck)], ones], axis=0)
                acc_ref[n] = alpha[n] * acc_ref[n] + jnp.dot(vtc, p_ref[n - n0], preferred_element_type=F32)
                m_ref[n] = m_new[n]
        return carry
    lax.fori_loop(0, n_chunks, attn_body, 0)

    ot = jnp.concatenate([acc_ref[n, :hd] / acc_ref[n, hd:hd + 1] for n in range(ATT_KV_HEADS)], axis=0)
    for g in range(ATT_GROUP):
        og = ot[:, g * qb:(g + 1) * qb].T
        for n in range(ATT_KV_HEADS):
            h = n * ATT_GROUP + g
            o_ref[:, h * ATT_HEAD_DIM:(h + 1) * ATT_HEAD_DIM] = (
                og[:, n * ATT_HEAD_DIM:(n + 1) * ATT_HEAD_DIM].astype(o_ref.dtype))


def _dsa_prompt(qt, qit, wit, kb, vbt, kib, b, n_sel):
    t = kb.shape[1]
    ck = min(512, t)
    nqb = t // Q_BLOCK
    idx_bits = max(1, math.ceil(math.log2(t))) + 1
    colblk = lambda n: pl.BlockSpec((n, Q_BLOCK), lambda bi, i: (0, bi * nqb + i))
    res = lambda n: pl.BlockSpec((1, t, n), lambda bi, i: (bi, 0, 0))
    gq = ATT_GROUP * Q_BLOCK
    return pl.pallas_call(
        functools.partial(_dsa_prompt_kernel, ck=ck, n_sel=n_sel, idx_bits=idx_bits),
        grid=(b, nqb),
        in_specs=[colblk(ATT_Q_DIM), colblk(IDX_HEADS * IDX_DIM), colblk(IDX_HEADS),
                  res(ATT_KV_DIM), pl.BlockSpec((ATT_KV_DIM, t), lambda bi, i: (0, bi)), res(IDX_DIM)],
        out_specs=pl.BlockSpec((Q_BLOCK, ATT_Q_DIM), lambda bi, i: (bi * nqb + i, 0)),
        out_shape=jax.ShapeDtypeStruct((b * t, ATT_Q_DIM), BF16),
        scratch_shapes=[pltpu.VMEM((t, Q_BLOCK), I32),
                        pltpu.VMEM((ATT_KV_HEADS, 1, gq), F32),
                        pltpu.VMEM((ATT_KV_HEADS, ATT_HEAD_DIM + BF16_ROWS, gq), F32),
                        pltpu.VMEM((2, ck, Q_BLOCK), BF16), pltpu.VMEM((ATT_KV_HEADS, ck, gq), F32),
                        pltpu.VMEM((2, ck, gq), BF16)],
        compiler_params=_cparams(("parallel", "arbitrary"), 56),
        name="dsa_prompt",
    )(qt, qit, wit, kb, vbt, kib)


def _outproj_ln_kernel(o_ref, h_ref, w_ref, g_ref, b_ref, out_ref):
    y = jnp.dot(o_ref[...], w_ref[...], preferred_element_type=F32)
    out_ref[...] = _layer_norm(DEEPNORM_ALPHA * h_ref[...] + y, g_ref[...], b_ref[...])


def _outproj_ln(o2d, h2d, w_out, g, b):
    m, kdim = o2d.shape
    tm = min(m, 512)
    row = lambda n: pl.BlockSpec((tm, n), lambda i: (i, 0))
    return pl.pallas_call(
        _outproj_ln_kernel,
        grid=(m // tm,),
        in_specs=[row(kdim), row(D_MODEL), pl.BlockSpec((kdim, D_MODEL), lambda i: (0, 0)),
                  pl.BlockSpec((1, D_MODEL), lambda i: (0, 0)), pl.BlockSpec((1, D_MODEL), lambda i: (0, 0))],
        out_specs=row(D_MODEL),
        out_shape=jax.ShapeDtypeStruct((m, D_MODEL), F32),
        compiler_params=_cparams(("parallel",)),
        name="outproj_ln",
    )(o2d, h2d, w_out.astype(BF16), g.reshape(1, -1), b.reshape(1, -1))


def _dsa_sample_kernel(pt_ref, qbd_ref, qi_ref, wi_ref, kn_ref, vn_ref, kin_ref,
                       cki_hbm, ck_hbm, cv_hbm, o_ref,
                       ki_buf, k_buf, v_buf, s_ref, keys_ref, cut_ref, sem,
                       *, n_pages, t_new, n_sel, idx_bits, layer_off, lchunk):
    b = pl.program_id(0)
    past = n_pages * PAGE_SIZE
    ltot = past + LANES
    rows = keys_ref.shape[0]

    def page_copy(hbm, buf, p, s):
        page = pt_ref[b, p] + layer_off
        return pltpu.make_async_copy(hbm.at[page],
                                     buf.at[:, pl.ds(pl.multiple_of(p * PAGE_SIZE, PAGE_SIZE), PAGE_SIZE)],
                                     sem.at[s])

    def start_all(p, c):
        page_copy(cki_hbm, ki_buf, p, 0).start()
        page_copy(ck_hbm, k_buf, p, 1).start()
        page_copy(cv_hbm, v_buf, p, 2).start()
        return c
    lax.fori_loop(0, n_pages, start_all, 0)

    def wait_all(hbm, buf, s):
        def body(p, c):
            page_copy(hbm, buf, p, s).wait()
            return c
        lax.fori_loop(0, n_pages, body, 0)

    qi = qi_ref[0]
    wcol = wi_ref[0]
    wait_all(cki_hbm, ki_buf, 0)

    def idx_scores(kic, dims=NN_DIMS):
        d = lax.dot_general(qi, kic, dims, preferred_element_type=F32)
        d = jnp.maximum(d, 0.0) * wcol
        return jnp.sum(d.reshape(t_new, IDX_HEADS, d.shape[1]), axis=1)

    keys_ref[...] = jnp.full(keys_ref.shape, INT_MIN, I32)
    for c in range(past // lchunk):
        sc = idx_scores(ki_buf[:, c * lchunk:(c + 1) * lchunk].astype(BF16))
        keys_ref[0:t_new, c * lchunk:(c + 1) * lchunk] = _sort_key(sc)
    scn = idx_scores(kin_ref[0], NT_DIMS)
    tpos = lax.broadcasted_iota(I32, (t_new, LANES), 0)
    lpos = lax.broadcasted_iota(I32, (t_new, LANES), 1)
    keys_ref[0:t_new, past:ltot] = jnp.where((lpos <= tpos) & (lpos < t_new), _sort_key(scn), INT_MIN)

    sck = max(w for w in range(LANES, 8 * LANES + 1, LANES) if ltot % w == 0)
    theta = _select_threshold(keys_ref, cut_ref, ltot // sck, sck, rows, n_sel, idx_bits)
    key = keys_ref[...]
    lane = lax.broadcasted_iota(I32, (rows, ltot), 1)
    sel = ((key > theta) | ((key == theta) & (lane < cut_ref[:, :1]))) & (key != INT_MIN)
    bias = jnp.where(sel, 0.0, NEG_INF)[0:t_new]
    bias = jnp.broadcast_to(bias[:, None, :], (t_new, ATT_HEADS, ltot)).reshape(t_new * ATT_HEADS, ltot)

    qbd = qbd_ref[0]
    wait_all(ck_hbm, k_buf, 1)
    for c in range(past // lchunk):
        kc = k_buf[:, c * lchunk:(c + 1) * lchunk].astype(BF16)
        s_ref[:, c * lchunk:(c + 1) * lchunk] = jnp.dot(qbd, kc, preferred_element_type=F32)
    s_ref[:, past:ltot] = lax.dot_general(qbd, kn_ref[0], NT_DIMS, preferred_element_type=F32)
    s = s_ref[...] + bias
    m = jnp.max(s, axis=1, keepdims=True)
    p = jnp.exp(s - m)
    linv = 1.0 / jnp.sum(p, axis=1, keepdims=True)
    s_ref[...] = p
    wait_all(cv_hbm, v_buf, 2)
    o = jnp.dot(s_ref[:, past:ltot].astype(BF16), vn_ref[0], preferred_element_type=F32)
    for c in range(past // lchunk):
        vc = v_buf[:, c * lchunk:(c + 1) * lchunk].astype(BF16)
        o = o + lax.dot_general(s_ref[:, c * lchunk:(c + 1) * lchunk].astype(BF16), vc, NT_DIMS,
                                preferred_element_type=F32)
    o = o * linv
    nrow = (lax.broadcasted_iota(I32, (t_new * ATT_HEADS, ATT_HEAD_DIM), 0) % ATT_HEADS) // ATT_GROUP
    out = jnp.zeros((t_new * ATT_HEADS, ATT_HEAD_DIM), F32)
    for n in range(ATT_KV_HEADS):
        out = out + jnp.where(nrow == n, o[:, n * ATT_HEAD_DIM:(n + 1) * ATT_HEAD_DIM], 0.0)
    o_ref[0] = out.astype(o_ref.dtype)


def _dsa_sample(page_table, qs, qis, wi, kb, vb, kib, cache_ki, cache_k, cache_v, layer_off, n_sel):
    b, t, _ = qs.shape
    n_pages = page_table.shape[1]
    past = n_pages * PAGE_SIZE
    ltot = past + LANES
    rows = 8
    q5 = qs.reshape(b, t, ATT_KV_HEADS, ATT_GROUP, 1, ATT_HEAD_DIM)
    eye = jnp.eye(ATT_KV_HEADS, dtype=qs.dtype).reshape(1, 1, ATT_KV_HEADS, 1, ATT_KV_HEADS, 1)
    qbd = (q5 * eye).reshape(b, t * ATT_HEADS, ATT_KV_DIM)
    qi = qis.reshape(b, t * IDX_HEADS, IDX_DIM)
    wcol = wi.reshape(b, t * IDX_HEADS, 1)
    padr = lambda a: jnp.pad(a, ((0, 0), (0, LANES - t), (0, 0)))
    kn, vn, kin = padr(kb), padr(vb), padr(kib)
    idx_bits = max(1, math.ceil(math.log2(ltot))) + 1
    lchunk = min(1024, past)
    bl = lambda a: pl.BlockSpec((1,) + a.shape[1:], lambda i, pt: (i, 0, 0))
    anyspec = pl.BlockSpec(memory_space=pl.ANY)
    grid_spec = pltpu.PrefetchScalarGridSpec(
        num_scalar_prefetch=1,
        grid=(b,),
        in_specs=[bl(qbd), bl(qi), bl(wcol), bl(kn), bl(vn), bl(kin), anyspec, anyspec, anyspec],
        out_specs=pl.BlockSpec((1, t * ATT_HEADS, ATT_HEAD_DIM), lambda i, pt: (i, 0, 0)),
        scratch_shapes=[pltpu.VMEM((IDX_DIM, past), F32), pltpu.VMEM((ATT_KV_DIM, past), F32),
                        pltpu.VMEM((ATT_KV_DIM, past), F32), pltpu.VMEM((t * ATT_HEADS, ltot), F32),
                        pltpu.VMEM((rows, ltot), I32), pltpu.VMEM((rows, LANES), I32),
                        pltpu.SemaphoreType.DMA((3,))],
    )
    o = pl.pallas_call(
        functools.partial(_dsa_sample_kernel, n_pages=n_pages, t_new=t, n_sel=n_sel, idx_bits=idx_bits,
                          layer_off=layer_off, lchunk=lchunk),
        grid_spec=grid_spec,
        out_shape=jax.ShapeDtypeStruct((b, t * ATT_HEADS, ATT_HEAD_DIM), BF16),
        compiler_params=_cparams(("arbitrary",), 56),
        name="dsa_sample",
    )(page_table, qbd, qi, wcol, kn, vn, kin, cache_ki, cache_k, cache_v)
    return o.reshape(b, t, ATT_Q_DIM)


def _gla_proj_kernel(x_ref, wq_ref, wk_ref, wv_ref, wg_ref, wr_ref, wup_ref, bg_ref,
                     q_ref, k_ref, v_ref, la_ref, r_ref):
    xb = x_ref[...].astype(BF16)
    q_ref[...] = jnp.dot(xb, wq_ref[...], preferred_element_type=F32) * (GLA_DK ** -0.5)
    k_ref[...] = jnp.dot(xb, wk_ref[...], preferred_element_type=F32)
    v_ref[...] = jnp.dot(xb, wv_ref[...], preferred_element_type=F32)
    r_ref[...] = jnp.dot(xb, wr_ref[...], preferred_element_type=F32)
    gd = jnp.dot(xb, wg_ref[...], preferred_element_type=F32)
    glogit = jnp.dot(gd.astype(BF16), wup_ref[...], preferred_element_type=F32) + bg_ref[...]
    la_ref[...] = jax.nn.log_sigmoid(glogit) / GLA_TAU


def _gla_proj(x2d, w_in, w_gate_up, b_gate):
    m = x2d.shape[0]
    tm = min(m, 512)
    dq = GLA_HEADS * GLA_DK
    dv = GLA_HEADS * GLA_DV
    wq = w_in[:, :dq].astype(BF16)
    wk = w_in[:, dq:2 * dq].astype(BF16)
    wv = w_in[:, 2 * dq:2 * dq + dv].astype(BF16)
    wg = jnp.pad(w_in[:, 2 * dq + dv:2 * dq + dv + GLA_GATE_RANK], ((0, 0), (0, LANES - GLA_GATE_RANK))).astype(BF16)
    wr = w_in[:, 2 * dq + dv + GLA_GATE_RANK:].astype(BF16)
    wup = jnp.pad(w_gate_up, ((0, LANES - GLA_GATE_RANK), (0, 0))).astype(BF16)
    row = lambda n: pl.BlockSpec((tm, n), lambda i: (i, 0))
    full = lambda a: pl.BlockSpec(a.shape, lambda i: (0, 0))
    bg = b_gate.reshape(1, -1)
    outs = [dq, dq, dv, dq, dv]
    return pl.pallas_call(
        _gla_proj_kernel,
        grid=(m // tm,),
        in_specs=[row(D_MODEL), full(wq), full(wk), full(wv), full(wg), full(wr), full(wup), full(bg)],
        out_specs=[row(n) for n in outs],
        out_shape=[jax.ShapeDtypeStruct((m, n), F32) for n in outs],
        compiler_params=_cparams(("parallel",)),
        name="gla_proj",
    )(x2d, wq, wk, wv, wg, wr, wup, bg)


def _gla_chunk(q, k, v, g, st, c):
    sub = GLA_SUB
    n_sub = c // sub
    ri = lax.broadcasted_iota(I32, (c, c), 0)
    ci = lax.broadcasted_iota(I32, (c, c), 1)
    tri = jnp.where(ci <= ri, 1.0, 0.0).astype(F32)
    bc = jnp.dot(tri, g, preferred_element_type=F32, precision=lax.Precision.HIGHEST)
    o = lax.dot_general((q * jnp.exp(bc)).astype(BF16), st.astype(BF16), NT_DIMS, preferred_element_type=F32)
    if n_sub > 1:
        mrow = jnp.concatenate([jnp.zeros((sub, GLA_DK), F32)] +
                               [jnp.broadcast_to(bc[i * sub - 1:i * sub], (sub, GLA_DK)) for i in range(1, n_sub)], axis=0)
        qh = (q * jnp.exp(bc - mrow)).astype(BF16)
        parts = [jnp.zeros((sub, GLA_DV), F32)]
        for i in range(1, n_sub):
            kh = (k[:i * sub] * jnp.exp(bc[i * sub - 1:i * sub] - bc[:i * sub])).astype(BF16)
            a = lax.dot_general(qh[i * sub:(i + 1) * sub], kh, NT_DIMS, preferred_element_type=F32)
            parts.append(jnp.dot(a.astype(BF16), v[:i * sub].astype(BF16), preferred_element_type=F32))
        o = o + jnp.concatenate(parts, axis=0)
    rsub = lax.broadcasted_iota(I32, (c, 1), 0) % sub
    for dlt in range(sub):
        ks = k if dlt == 0 else pltpu.roll(k, dlt, 0)
        bs = bc if dlt == 0 else pltpu.roll(bc, dlt, 0)
        vs = v if dlt == 0 else pltpu.roll(v, dlt, 0)
        w = jnp.sum(q * ks * jnp.exp(jnp.minimum(bc - bs, 0.0)), axis=1, keepdims=True)
        o = o + jnp.where(rsub >= dlt, w, 0.0) * vs
    bl = bc[c - 1:c]
    kt = (k * jnp.exp(bl - bc)).astype(BF16)
    st_new = st * jnp.exp(bl) + lax.dot_general(v.astype(BF16), kt, TN_DIMS, preferred_element_type=F32)
    return o, st_new


def _gla_kernel(q_ref, k_ref, v_ref, g_ref, r_ref, ng_ref, s0_ref, o_ref, sf_ref, st_ref, *, c, n_c):
    tb = pl.program_id(2)

    @pl.when(tb == 0)
    def _():
        st_ref[...] = s0_ref[0, 0].T

    st = st_ref[...]
    for ci in range(n_c):
        sl = slice(ci * c, (ci + 1) * c)
        o, st = _gla_chunk(q_ref[0, sl, :], k_ref[0, sl, :], v_ref[0, sl, :], g_ref[0, sl, :], st, c)
        o = o * lax.rsqrt(jnp.mean(o * o, axis=-1, keepdims=True) + LN_EPS) * ng_ref[...]
        o_ref[0, sl, :] = (o * jax.nn.silu(r_ref[0, sl, :])).astype(o_ref.dtype)
    st_ref[...] = st

    @pl.when(tb == pl.num_programs(2) - 1)
    def _():
        sf_ref[0, 0] = st.T


def _gla(q, k, v, la, r, norm_g, s0):
    b, t, _ = q.shape
    c = math.gcd(t, GLA_CHUNK)
    tblk = math.gcd(t, 16 * GLA_CHUNK)
    n_c = tblk // c
    kspec = pl.BlockSpec((1, tblk, GLA_DK), lambda bi, h, i: (bi, i, h))
    vspec = pl.BlockSpec((1, tblk, GLA_DV), lambda bi, h, i: (bi, i, h))
    sspec = pl.BlockSpec((1, 1, GLA_DK, GLA_DV), lambda bi, h, i: (bi, h, 0, 0))
    return pl.pallas_call(
        functools.partial(_gla_kernel, c=c, n_c=n_c),
        grid=(b, GLA_HEADS, t // tblk),
        in_specs=[kspec, kspec, vspec, kspec, vspec, pl.BlockSpec((1, GLA_DV), lambda bi, h, i: (0, 0)), sspec],
        out_specs=[vspec, sspec],
        out_shape=[jax.ShapeDtypeStruct((b, t, GLA_HEADS * GLA_DV), BF16),
                   jax.ShapeDtypeStruct((b, GLA_HEADS, GLA_DK, GLA_DV), F32)],
        scratch_shapes=[pltpu.VMEM((GLA_DV, GLA_DK), F32)],
        compiler_params=_cparams(("parallel", "parallel", "arbitrary")),
        name="gla_scan",
    )(q, k, v, la, r, norm_g.reshape(1, -1), s0)


def _gla_mixer(x, s0, w_in, w_gate_up, b_gate, norm_g):
    b, t, _ = x.shape
    q, k, v, la, r = _gla_proj(x.reshape(b * t, D_MODEL), w_in, w_gate_up, b_gate)
    tp = -(-t // GLA_SUB) * GLA_SUB
    r3 = lambda a: jnp.pad(a.reshape(b, t, -1), ((0, 0), (0, tp - t), (0, 0)))
    o, sf = _gla(r3(q), r3(k), r3(v), r3(la), r3(r), norm_g, s0)
    return o[:, :t], sf


def _top_values(s, k):
    vals = []
    cur = s
    for _ in range(k):
        mx = jnp.max(cur, axis=0, keepdims=True)
        vals.append(mx)
        cur = jnp.where(cur == mx, NEG_INF, cur)
    return vals


def _top_values_ranked(s, k):
    vals = []
    cur = s
    rank = jnp.full(s.shape, float(k), F32)
    for i in range(k):
        mx = jnp.max(cur, axis=0, keepdims=True)
        vals.append(mx)
        top = cur == mx
        rank = jnp.where(top, float(i), rank)
        cur = jnp.where(top, NEG_INF, cur)
    return vals, rank


def _peer_select_kernel(x_ref, wq_ref, k1_ref, k2_ref, c1_ref, p_ref, r2_ref, e_ref):
    xb = x_ref[...].astype(BF16)
    q = jnp.dot(xb, wq_ref[...], preferred_element_type=F32).astype(BF16)
    tt = q.shape[0]
    k1 = k1_ref[...]
    k2 = k2_ref[...]
    sub8 = lax.broadcasted_iota(I32, (8, tt), 0)
    for h in range(PEER_HEADS):
        s1 = lax.dot_general(k1, q[:, h * PEER_D_KEY:h * PEER_D_KEY + PEER_HALF], NT_DIMS,
                             preferred_element_type=F32)
        s2 = lax.dot_general(k2, q[:, h * PEER_D_KEY + PEER_HALF:(h + 1) * PEER_D_KEY], NT_DIMS,
                             preferred_element_type=F32)
        t1, rank1 = _top_values_ranked(s1, PEER_TOPK)
        t2, rank2 = _top_values_ranked(s2, PEER_TOPK)
        t2a = jnp.concatenate(t2, axis=0)
        cands = [t1[0] + t2a]
        for a in range(1, PEER_TOPK):
            nb = PEER_TOPK // (a + 1)
            cands.append(jnp.where(sub8 < nb, t1[a] + t2a[:8], NEG_INF))
        best = _top_values(jnp.concatenate(cands, axis=0), PEER_TOPK)
        theta = best[PEER_TOPK - 1]
        z = best[0] * 0.0
        for bv in best:
            z = z + jnp.exp(bv - best[0])
        count1 = jnp.zeros(s1.shape, F32)
        for a in range(PEER_TOPK):
            cnt_a = jnp.sum(jnp.where(t1[a] + t2a >= theta, 1.0, 0.0), axis=0, keepdims=True)
            count1 = jnp.where(rank1 == float(a), cnt_a, count1)
        c1_ref[h] = count1
        p_ref[h] = jnp.exp(s1 - t1[0]) * (0.5 / z)
        r2_ref[h] = rank2
        e_ref[h] = jnp.exp(s2 - t2[0])


def _peer_select(x2d, w_q, keys1, keys2):
    n = x2d.shape[0]
    tt = min(n, 256)
    wq = w_q.astype(BF16)
    k1 = keys1.astype(BF16)
    k2 = keys2.astype(BF16)
    full = lambda a: pl.BlockSpec(a.shape, lambda i: (0,) * a.ndim)
    hspec = pl.BlockSpec((PEER_HEADS, PEER_N_KEYS, tt), lambda i: (0, 0, i))
    big = lambda dt: jax.ShapeDtypeStruct((PEER_HEADS, PEER_N_KEYS, n), dt)
    return pl.pallas_call(
        _peer_select_kernel,
        grid=(n // tt,),
        in_specs=[pl.BlockSpec((tt, D_MODEL), lambda i: (i, 0)), full(wq), full(k1), full(k2)],
        out_specs=[hspec, hspec, hspec, hspec],
        out_shape=[big(F32), big(F32), big(F32), big(F32)],
        compiler_params=_cparams(("parallel",)),
        name="peer_select",
    )(x2d, wq, k1, k2)


def _peer_mix_kernel(h_ref, c1_ref, p_ref, r2_ref, e_ref, u_ref, vt_ref, g_ref, bb_ref, out_ref,
                     xb_ref, ht_ref, wt_ref, yt_ref, *, n_i1):
    eb = pl.program_id(1)
    tt = h_ref.shape[0]

    @pl.when(eb == 0)
    def _():
        xb_ref[...] = h_ref[...].T.astype(BF16)
        yt_ref[...] = jnp.zeros(yt_ref.shape, F32)

    ht_ref[...] = jnp.dot(u_ref[...], xb_ref[...], preferred_element_type=F32)

    i1_base = pl.multiple_of(eb * n_i1, n_i1)

    for lg in range(tt // LANES):
        ls = slice(lg * LANES, (lg + 1) * LANES)
        c8 = [c1_ref[h, pl.ds(i1_base, n_i1), ls] for h in range(PEER_HEADS)]
        p8 = [p_ref[h, pl.ds(i1_base, n_i1), ls] for h in range(PEER_HEADS)]
        for j in range(n_i1):
            cb = [jnp.broadcast_to(c8[h][j:j + 1], (BF16_ROWS, LANES)) for h in range(PEER_HEADS)]
            pb = [jnp.broadcast_to(p8[h][j:j + 1], (BF16_ROWS, LANES)) for h in range(PEER_HEADS)]
            for r in range(PEER_N_KEYS // BF16_ROWS):
                ks = slice(r * BF16_ROWS, (r + 1) * BF16_ROWS)
                acc = jnp.zeros((BF16_ROWS, LANES), F32)
                for h in range(PEER_HEADS):
                    acc = acc + jnp.where(r2_ref[h, ks, ls] < cb[h], pb[h] * e_ref[h, ks, ls], 0.0)
                rs = slice(j * PEER_N_KEYS + r * BF16_ROWS, j * PEER_N_KEYS + (r + 1) * BF16_ROWS)
                x = ht_ref[rs, ls]
                gelu2 = x * (1.0 + lax.erf(x * (2.0 ** -0.5)))
                wt_ref[rs, ls] = (gelu2 * acc).astype(BF16)

    yt_ref[...] += jnp.dot(vt_ref[...], wt_ref[...], preferred_element_type=F32)

    @pl.when(eb == pl.num_programs(1) - 1)
    def _():
        y = yt_ref[...].T
        out_ref[...] = _layer_norm(DEEPNORM_ALPHA * h_ref[...] + y, g_ref[...], bb_ref[...])


def _block_transpose_kernel(v_ref, o_ref):
    o_ref[...] = v_ref[...].T.astype(o_ref.dtype)


def _cast_kernel(x_ref, o_ref):
    o_ref[...] = x_ref[...].astype(o_ref.dtype)


def _key_table(u_all, layer, eblk):
    _, n_e, d = u_all.shape
    return pl.pallas_call(
        _cast_kernel,
        grid=(n_e // eblk,),
        in_specs=[pl.BlockSpec((None, eblk, d), lambda j: (layer, j, 0))],
        out_specs=pl.BlockSpec((eblk, d), lambda j: (j, 0)),
        out_shape=jax.ShapeDtypeStruct((n_e, d), BF16),
        compiler_params=_cparams(("parallel",)),
        name="key_table",
    )(u_all)


def _value_blocks(v_all, layer, eblk):
    _, n_e, d = v_all.shape
    return pl.pallas_call(
        _block_transpose_kernel,
        grid=(n_e // eblk,),
        in_specs=[pl.BlockSpec((None, eblk, d), lambda j: (layer, j, 0))],
        out_specs=pl.BlockSpec((None, d, eblk), lambda j: (j, 0, 0)),
        out_shape=jax.ShapeDtypeStruct((n_e // eblk, d, eblk), BF16),
        compiler_params=_cparams(("parallel",)),
        name="value_blocks",
    )(v_all)


def _peer_ln(h2d, w_q, keys1, keys2, ub, vt, g, b):
    n_real = h2d.shape[0]
    if n_real % LANES:
        h2d = jnp.pad(h2d, ((0, LANES - n_real % LANES), (0, 0)))
        return _peer_ln(h2d, w_q, keys1, keys2, ub, vt, g, b)[:n_real]
    n = n_real
    c1, p, r2, e = _peer_select(h2d, w_q, keys1, keys2)
    tt = min(n, 512)
    n_blk, _, eblk = vt.shape
    hspec = pl.BlockSpec((PEER_HEADS, PEER_N_KEYS, tt), lambda i, j: (0, 0, i))
    vec = pl.BlockSpec((1, D_MODEL), lambda i, j: (0, 0))
    return pl.pallas_call(
        functools.partial(_peer_mix_kernel, n_i1=eblk // PEER_N_KEYS),
        grid=(n // tt, n_blk),
        in_specs=[pl.BlockSpec((tt, D_MODEL), lambda i, j: (i, 0)), hspec, hspec, hspec, hspec,
                  pl.BlockSpec((eblk, D_MODEL), lambda i, j: (j, 0)),
                  pl.BlockSpec((None, D_MODEL, eblk), lambda i, j: (j, 0, 0)), vec, vec],
        out_specs=pl.BlockSpec((tt, D_MODEL), lambda i, j: (i, 0)),
        out_shape=jax.ShapeDtypeStruct((n, D_MODEL), F32),
        scratch_shapes=[pltpu.VMEM((D_MODEL, tt), BF16), pltpu.VMEM((eblk, tt), F32),
                        pltpu.VMEM((eblk, tt), BF16), pltpu.VMEM((D_MODEL, tt), F32)],
        compiler_params=_cparams(("parallel", "arbitrary"), 56),
        name="peer_mix",
    )(h2d, c1, p, r2, e, ub, vt, g.reshape(1, -1), b.reshape(1, -1))


def kernel(x_prompt, x_sample, cache_k, cache_v, cache_kidx, state_gla, page_table,
           attn_w_in, attn_w_out, gla_w_in, gla_w_gate_up, gla_b_gate, gla_norm_g, gla_w_out,
           peer_w_q, peer_keys1, peer_keys2, peer_u, peer_v, ln1_g, ln1_b, ln2_g, ln2_b):
    bp, tp, _ = x_prompt.shape
    bs, ts, _ = x_sample.shape
    n_pool = cache_k.shape[1]
    past = page_table.shape[1] * PAGE_SIZE
    ck = jnp.swapaxes(cache_k.reshape(-1, PAGE_SIZE, ATT_KV_DIM), 1, 2)
    cv = jnp.swapaxes(cache_v.reshape(-1, PAGE_SIZE, ATT_KV_DIM), 1, 2)
    cki = jnp.swapaxes(cache_kidx.reshape(-1, PAGE_SIZE, IDX_DIM), 1, 2)
    hp = x_prompt.reshape(bp * tp, D_MODEL)
    hs = x_sample.reshape(bs * ts, D_MODEL)
    kp_l, vp_l, kip_l, sp_l = [], [], [], []
    ks_l, vs_l, kis_l, ss_l = [], [], [], []
    for i in range(DEPTH):
        j = i // 2
        if i % 2 == 0:
            k, v, ki, kb, kib, qt, qit, wit, vbt = _attn_proj_t(hp, attn_w_in[j])
            r3 = lambda a: a.reshape(bp, tp, -1)
            op = _dsa_prompt(qt, qit, wit, r3(kb), vbt, r3(kib), bp, min(TOPK_MAX, tp // 4))
            kp_l.append(k.reshape(bp, tp, ATT_KV_HEADS, ATT_HEAD_DIM))
            vp_l.append(v.reshape(bp, tp, ATT_KV_HEADS, ATT_HEAD_DIM))
            kip_l.append(ki.reshape(bp, tp, IDX_DIM))
            qs, k, v, kb, vb, qis, ki, kib, wi = _attn_proj(hs, attn_w_in[j])
            r3 = lambda a: a.reshape(bs, ts, -1)
            os_ = _dsa_sample(page_table, r3(qs), r3(qis), r3(wi), r3(kb), r3(vb), r3(kib), cki, ck, cv,
                              j * n_pool, min(TOPK_MAX, (past + ts) // 4))
            ks_l.append(k.reshape(bs, ts, ATT_KV_HEADS, ATT_HEAD_DIM))
            vs_l.append(v.reshape(bs, ts, ATT_KV_HEADS, ATT_HEAD_DIM))
            kis_l.append(ki.reshape(bs, ts, IDX_DIM))
            w_out = attn_w_out[j]
        else:
            s0 = jnp.zeros((bp, GLA_HEADS, GLA_DK, GLA_DV), F32)
            op, sp = _gla_mixer(hp.reshape(bp, tp, D_MODEL), s0, gla_w_in[j], gla_w_gate_up[j], gla_b_gate[j],
                                gla_norm_g[j])
            os_, ss = _gla_mixer(hs.reshape(bs, ts, D_MODEL), state_gla[j], gla_w_in[j], gla_w_gate_up[j],
                                 gla_b_gate[j], gla_norm_g[j])
            sp_l.append(sp)
            ss_l.append(ss)
            w_out = gla_w_out[j]
        hp = _outproj_ln(op.reshape(bp * tp, -1), hp, w_out, ln1_g[i], ln1_b[i])
        hs = _outproj_ln(os_.reshape(bs * ts, -1), hs, w_out, ln1_g[i], ln1_b[i])
        pw = (peer_w_q[i], peer_keys1[i], peer_keys2[i], _key_table(peer_u, i, PEER_EXPERT_BLOCK),
              _value_blocks(peer_v, i, PEER_EXPERT_BLOCK), ln2_g[i], ln2_b[i])
        hp = _peer_ln(hp, *pw)
        hs = _peer_ln(hs, *pw)
    return (hp.reshape(bp, tp, D_MODEL), hs.reshape(bs, ts, D_MODEL),
            jnp.stack(kp_l), jnp.stack(vp_l), jnp.stack(kip_l), jnp.stack(sp_l),
            jnp.stack(ks_l), jnp.stack(vs_l), jnp.stack(kis_l), jnp.stack(ss_l))
```

```python
import functools
import math

import jax
import jax.numpy as jnp
from jax import lax
from jax.experimental import pallas as pl
from jax.experimental.pallas import tpu as pltpu

F32 = jnp.float32
BF16 = jnp.bfloat16
I32 = jnp.int32

D_MODEL = 1024
DEPTH = 2
PAGE_SIZE = 128
DEEPNORM_ALPHA = (2.0 * DEPTH) ** 0.25
LN_EPS = 1e-5

ATT_HEADS = 16
ATT_KV_HEADS = 4
ATT_HEAD_DIM = 64
ATT_GROUP = ATT_HEADS // ATT_KV_HEADS
IDX_HEADS = 8
IDX_DIM = 64
TOPK_MAX = 256
Q_BLOCK = 128
ATT_Q_DIM = ATT_HEADS * ATT_HEAD_DIM
ATT_KV_DIM = ATT_KV_HEADS * ATT_HEAD_DIM

GLA_HEADS = 4
GLA_DK = D_MODEL // 2 // GLA_HEADS
GLA_DV = D_MODEL // GLA_HEADS
GLA_GATE_RANK = 16
GLA_TAU = 16.0
GLA_CHUNK = 64
GLA_SUB = 16

PEER_HEADS = 8
PEER_N_KEYS = 128
PEER_D_KEY = 256
PEER_HALF = PEER_D_KEY // 2
PEER_TOPK = 16
PEER_EXPERT_BLOCK = 8 * PEER_N_KEYS

LANES = 128
BF16_ROWS = 16
LOG2_E = 1.4426950408889634
INT_MIN = -(2 ** 31)
NEG_INF = float("-inf")
MASKED = -1e30

NN_DIMS = (((1,), (0,)), ((), ()))
NT_DIMS = (((1,), (1,)), ((), ()))
TN_DIMS = (((0,), (0,)), ((), ()))


def _cparams(sem, vmem_mib=None, flags=None):
    kw = dict(dimension_semantics=sem)
    if vmem_mib is not None:
        kw["vmem_limit_bytes"] = vmem_mib * 1024 * 1024
    if flags:
        kw["flags"] = flags
    return pltpu.CompilerParams(**kw)


def _sort_key(x):
    b = pltpu.bitcast(x + 0.0, I32)
    return b ^ ((b >> 31) & 0x7FFFFFFF)


def _layer_norm(z, g, b):
    mu = jnp.mean(z, axis=-1, keepdims=True)
    zc = z - mu
    var = jnp.mean(zc * zc, axis=-1, keepdims=True)
    return zc * lax.rsqrt(var + LN_EPS) * g + b


def _row_count(keys_ref, n_chunks, ck, rows, pred):
    def body(c, acc):
        base = pl.multiple_of(c * ck, ck)
        return acc + jnp.where(pred(keys_ref[:, pl.ds(base, ck)], base), 1.0, 0.0)
    acc = lax.fori_loop(0, n_chunks, body, jnp.zeros((rows, ck), F32))
    return jnp.sum(acc, axis=1, keepdims=True)


def _select_threshold(keys_ref, cut_ref, n_chunks, ck, rows, kth, idx_bits):
    kth_f = float(kth)

    def count_ge(t):
        tb = jnp.broadcast_to(t, (rows, ck))
        return _row_count(keys_ref, n_chunks, ck, rows, lambda blk, base: blk >= tb)

    theta = jnp.full((rows, 1), INT_MIN, I32)
    zero = jnp.zeros((rows, 1), I32)
    theta = jnp.where(count_ge(zero) >= kth_f, zero, theta)

    def bit_body(i, t):
        cand = t | lax.shift_left(jnp.int32(1), jnp.int32(30) - i)
        return jnp.where(count_ge(cand) >= kth_f, cand, t)
    theta = lax.fori_loop(0, 31, bit_body, theta)

    n_gt = count_ge(theta + 1)
    n_ge = count_ge(theta)
    need = kth_f - n_gt
    ambiguous = (n_ge - n_gt > need) & (theta > INT_MIN)
    cut_ref[...] = jnp.full(cut_ref.shape, 2 ** idx_bits, I32)

    @pl.when(jnp.max(jnp.where(ambiguous, 1.0, 0.0)) > 0.5)
    def _():
        thb = jnp.broadcast_to(theta, (rows, ck))
        lane = lax.broadcasted_iota(I32, (rows, ck), 1)

        def ties_before(jc):
            jb = jnp.broadcast_to(jc, (rows, ck))
            return _row_count(keys_ref, n_chunks, ck, rows,
                              lambda blk, base: (blk == thb) & (lane + base < jb))

        def jbit(i, jcur):
            cand = jcur | lax.shift_left(jnp.int32(1), jnp.int32(idx_bits - 1) - i)
            return jnp.where(ties_before(cand) <= need, cand, jcur)
        jfin = lax.fori_loop(0, idx_bits, jbit, jnp.zeros((rows, 1), I32))
        jfin = jnp.where(ambiguous, jfin, 2 ** idx_bits)
        cut_ref[...] = jnp.broadcast_to(jfin, cut_ref.shape)
    return theta


def _attn_proj_kernel(x_ref, wq_ref, wkv_ref, wqi_ref, wkw_ref,
                      q_ref, k_ref, v_ref, kb_ref, vb_ref, qi_ref, ki_ref, kib_ref, wi_ref):
    xb = x_ref[...].astype(BF16)
    q = jnp.dot(xb, wq_ref[...], preferred_element_type=F32)
    q_ref[...] = (q * (ATT_HEAD_DIM ** -0.5)).astype(BF16)
    kv = jnp.dot(xb, wkv_ref[...], preferred_element_type=F32)
    k = kv[:, :ATT_KV_DIM]
    v = kv[:, ATT_KV_DIM:]
    k_ref[...] = k
    v_ref[...] = v
    kb_ref[...] = k.astype(BF16)
    vb_ref[...] = v.astype(BF16)
    qi = jnp.dot(xb, wqi_ref[...], preferred_element_type=F32)
    qi_ref[...] = (qi * (IDX_DIM ** -0.5)).astype(BF16)
    kw = jnp.dot(xb, wkw_ref[...], preferred_element_type=F32)
    ki = kw[:, :IDX_DIM]
    ki_ref[...] = ki
    kib_ref[...] = ki.astype(BF16)
    wi_ref[...] = kw[:, IDX_DIM:IDX_DIM + IDX_HEADS] * (IDX_HEADS ** -0.5)


def _attn_weights(w_in):
    s0 = ATT_Q_DIM
    s2 = s0 + 2 * ATT_KV_DIM
    s3 = s2 + IDX_HEADS * IDX_DIM
    wkw = jnp.pad(w_in[:, s3:], ((0, 0), (0, LANES - (w_in.shape[1] - s3))))
    return (w_in[:, :s0].astype(BF16), w_in[:, s0:s2].astype(BF16), w_in[:, s2:s3].astype(BF16),
            wkw.astype(BF16))


def _attn_proj(x2d, w_in):
    m = x2d.shape[0]
    tm = min(m, 512)
    wq, wkv, wqi, wkw = _attn_weights(w_in)
    row = lambda n: pl.BlockSpec((tm, n), lambda i: (i, 0))
    full = lambda a: pl.BlockSpec(a.shape, lambda i: (0, 0))
    outs = [(ATT_Q_DIM, BF16), (ATT_KV_DIM, F32), (ATT_KV_DIM, F32), (ATT_KV_DIM, BF16),
            (ATT_KV_DIM, BF16), (IDX_HEADS * IDX_DIM, BF16), (IDX_DIM, F32), (IDX_DIM, BF16),
            (IDX_HEADS, F32)]
    return pl.pallas_call(
        _attn_proj_kernel,
        grid=(m // tm,),
        in_specs=[row(D_MODEL), full(wq), full(wkv), full(wqi), full(wkw)],
        out_specs=[row(n) for n, _ in outs],
        out_shape=[jax.ShapeDtypeStruct((m, n), dt) for n, dt in outs],
        compiler_params=_cparams(("parallel",)),
        name="attn_proj",
    )(x2d, wq, wkv, wqi, wkw)


def _attn_proj_t_kernel(x_ref, wq_ref, wkv_ref, wqi_ref, wkw_ref,
                        k_ref, v_ref, ki_ref, kb_ref, kib_ref, qt_ref, qit_ref, wit_ref, vbt_ref):
    xb = x_ref[...].astype(BF16)
    q = jnp.dot(xb, wq_ref[...], preferred_element_type=F32)
    qt_ref[...] = (q * (ATT_HEAD_DIM ** -0.5 * LOG2_E)).T.astype(BF16)
    kv = jnp.dot(xb, wkv_ref[...], preferred_element_type=F32)
    k = kv[:, :ATT_KV_DIM]
    v = kv[:, ATT_KV_DIM:]
    k_ref[...] = k
    v_ref[...] = v
    kb_ref[...] = k.astype(BF16)
    vbt_ref[...] = v.T.astype(BF16)
    qi = jnp.dot(xb, wqi_ref[...], preferred_element_type=F32)
    qit_ref[...] = (qi * (IDX_DIM ** -0.5)).T.astype(BF16)
    kw = jnp.dot(xb, wkw_ref[...], preferred_element_type=F32)
    ki = kw[:, :IDX_DIM]
    ki_ref[...] = ki
    kib_ref[...] = ki.astype(BF16)
    wit_ref[...] = kw.T[IDX_DIM:IDX_DIM + IDX_HEADS, :] * (IDX_HEADS ** -0.5)


def _attn_proj_t(x2d, w_in):
    m = x2d.shape[0]
    tm = min(m, 512)
    wq, wkv, wqi, wkw = _attn_weights(w_in)
    row = lambda n: pl.BlockSpec((tm, n), lambda i: (i, 0))
    col = lambda n: pl.BlockSpec((n, tm), lambda i: (0, i))
    full = lambda a: pl.BlockSpec(a.shape, lambda i: (0, 0))
    nat = [(ATT_KV_DIM, F32), (ATT_KV_DIM, F32), (IDX_DIM, F32), (ATT_KV_DIM, BF16), (IDX_DIM, BF16)]
    tr = [(ATT_Q_DIM, BF16), (IDX_HEADS * IDX_DIM, BF16), (IDX_HEADS, F32), (ATT_KV_DIM, BF16)]
    return pl.pallas_call(
        _attn_proj_t_kernel,
        grid=(m // tm,),
        in_specs=[row(D_MODEL), full(wq), full(wkv), full(wqi), full(wkw)],
        out_specs=[row(n) for n, _ in nat] + [col(n) for n, _ in tr],
        out_shape=[jax.ShapeDtypeStruct((m, n), dt) for n, dt in nat] +
                  [jax.ShapeDtypeStruct((n, m), dt) for n, dt in tr],
        compiler_params=_cparams(("parallel",)),
        name="attn_proj_t",
    )(x2d, wq, wkv, wqi, wkw)


def _col_count(keys_ref, n_chunks, ck, pred):
    width = keys_ref.shape[1]
    par = math.gcd(ck, 64)

    def body(c, acc):
        base = pl.multiple_of(c * ck, ck)
        hit = pred(keys_ref[pl.ds(base, ck), :], base)
        return acc + jnp.sum(jnp.where(hit, 1.0, 0.0).reshape(ck // par, par, width), axis=0)
    acc = lax.fori_loop(0, n_chunks, body, jnp.zeros((par, width), F32))
    return jnp.sum(acc, axis=0, keepdims=True)


def _select_threshold_cols(keys_ref, n_chunks, ck, kth, idx_bits):
    width = keys_ref.shape[1]
    kth_f = float(kth)

    def count_ge(t):
        return _col_count(keys_ref, n_chunks, ck, lambda blk, base: blk >= t)

    theta = jnp.full((1, width), INT_MIN, I32)
    zero = jnp.zeros((1, width), I32)
    theta = jnp.where(count_ge(zero) >= kth_f, zero, theta)

    def bit_body(i, t):
        cand = t | lax.shift_left(jnp.int32(1), jnp.int32(30) - i)
        return jnp.where(count_ge(cand) >= kth_f, cand, t)
    theta = lax.fori_loop(0, 31, bit_body, theta)

    n_gt = count_ge(theta + 1)
    n_ge = count_ge(theta)
    need = kth_f - n_gt
    ambiguous = (n_ge - n_gt > need) & (theta > INT_MIN)
    no_cut = jnp.full((1, width), 2 ** idx_bits, I32)

    def tie_search():
        row = lax.broadcasted_iota(I32, (ck, width), 0)

        def ties_before(jc):
            return _col_count(keys_ref, n_chunks, ck, lambda blk, base: (blk == theta) & (row + base < jc))

        def jbit(i, jcur):
            cand = jcur | lax.shift_left(jnp.int32(1), jnp.int32(idx_bits - 1) - i)
            return jnp.where(ties_before(cand) <= need, cand, jcur)
        jfin = lax.fori_loop(0, idx_bits, jbit, jnp.zeros((1, width), I32))
        return jnp.where(ambiguous, jfin, no_cut)

    cut = lax.cond(jnp.max(jnp.where(ambiguous, 1.0, 0.0)) > 0.5, tie_search, lambda: no_cut)
    return theta, cut


def _dsa_prompt_kernel(qt_ref, qit_ref, wit_ref, kb_ref, vbt_ref, kib_ref, o_ref,
                       keys_ref, m_ref, acc_ref, bias_ref, s_ref, p_ref, *, ck, n_sel, idx_bits):
    i = pl.program_id(1)
    qb = Q_BLOCK
    n_chunks = ((i + 1) * qb + ck - 1) // ck
    q_pos = i * qb + lax.broadcasted_iota(I32, (ck, qb), 1)
    row = lax.broadcasted_iota(I32, (ck, qb), 0)

    qit = qit_ref[...]
    qi_all = jnp.concatenate([qit[h * IDX_DIM:(h + 1) * IDX_DIM, :] for h in range(IDX_HEADS)], axis=1)
    wit = wit_ref[...]

    def score_body(c, carry):
        base = pl.multiple_of(c * ck, ck)
        d = jnp.dot(kib_ref[0, pl.ds(base, ck), :], qi_all, preferred_element_type=F32)
        acc = jnp.zeros((ck, qb), F32)
        for h in range(IDX_HEADS):
            acc = acc + jnp.maximum(d[:, h * qb:(h + 1) * qb], 0.0) * wit[h:h + 1, :]
        keys_ref[pl.ds(base, ck), :] = jnp.where(row + base <= q_pos, _sort_key(acc), INT_MIN)
        return carry
    lax.fori_loop(0, n_chunks, score_body, 0)

    theta, cut = _select_threshold_cols(keys_ref, n_chunks, ck, n_sel, idx_bits)

    qt = qt_ref[...]
    qn = [jnp.concatenate([qt[(n * ATT_GROUP + g) * ATT_HEAD_DIM:(n * ATT_GROUP + g + 1) * ATT_HEAD_DIM, :]
                           for g in range(ATT_GROUP)], axis=1) for n in range(ATT_KV_HEADS)]
    m_ref[...] = jnp.full(m_ref.shape, NEG_INF, F32)
    acc_ref[...] = jnp.zeros(acc_ref.shape, F32)
    hd = ATT_HEAD_DIM
    ones = jnp.ones((acc_ref.shape[1] - hd, ck), BF16)
    par = math.gcd(ck, 64)

    n_sub = ck // par
    pair = 2

    eye = (lax.broadcasted_iota(I32, (qb, ATT_GROUP * qb), 1) % qb ==
           lax.broadcasted_iota(I32, (qb, ATT_GROUP * qb), 0))
    eye = jnp.where(eye, 1.0, 0.0).astype(BF16)
    qa = [jnp.concatenate([eye, qn[n]], axis=0) for n in range(ATT_KV_HEADS)]

    def mask_rows(kbase, slot):
        key = keys_ref[pl.ds(kbase, ck), :]
        sel = ((key > theta) | ((key == theta) & (row + kbase < cut))) & (key != INT_MIN)
        bias_ref[slot] = jnp.where(sel, 0.0, MASKED).astype(BF16)

    def scores(kbase, slot, n):
        kc = jnp.concatenate([bias_ref[slot], kb_ref[0, pl.ds(kbase, ck), n * hd:(n + 1) * hd]], axis=1)
        s_ref[n] = jnp.dot(kc, qa[n], preferred_element_type=F32)
    mask_rows(0, 0)
    for n in range(pair):
        scores(0, 0, n)

    def attn_body(c, carry):
        base = pl.multiple_of(c * ck, ck)
        slot = c % 2

        for n0 in range(0, ATT_KV_HEADS, pair):
            heads = range(n0, n0 + pair)
            if n0 + pair < ATT_KV_HEADS:
                for n in heads:
                    scores(base, slot, n + pair)
            else:
                nbase = pl.multiple_of(jnp.minimum(c + 1, n_chunks - 1) * ck, ck)
                mask_rows(nbase, 1 - slot)
                for n in heads:
                    scores(nbase, 1 - slot, n - n0)

            def masked(n, j):
                return s_ref[n, j * par:(j + 1) * par, :]
            mx = {n: masked(n, 0) for n in heads}
            for j in range(1, n_sub):
                for n in heads:
                    mx[n] = jnp.maximum(mx[n], masked(n, j))
            m_safe, alpha, m_new = {}, {}, {}
            for n in heads:
                m_prev = m_ref[n]
                m_new[n] = jnp.maximum(m_prev, jnp.max(mx[n], axis=0, keepdims=True))
                m_safe[n] = jnp.where(m_new[n] == NEG_INF, 0.0, m_new[n])
                alpha[n] = jnp.exp2(m_prev - m_safe[n])
            for j in range(n_sub):
                for n in heads:
                    p_ref[n - n0, j * par:(j + 1) * par, :] = jnp.exp2((masked(n, j) - m_safe[n]).astype(BF16))
            for n in heads:
                vtc = jnp.concatenate([vbt_ref[n * hd:(n + 1) * hd, pl.ds(base, ck)], ones], axis=0)
                acc_ref[n] = alpha[n] * acc_ref[n] + jnp.dot(vtc, p_ref[n - n0], preferred_element_type=F32)
                m_ref[n] = m_new[n]
        return carry
    lax.fori_loop(0, n_chunks, attn_body, 0)

    ot = jnp.concatenate([acc_ref[n, :hd] / acc_ref[n, hd:hd + 1] for n in range(ATT_KV_HEADS)], axis=0)
    for g in range(ATT_GROUP):
        og = ot[:, g * qb:(g + 1) * qb].T
        for n in range(ATT_KV_HEADS):
            h = n * ATT_GROUP + g
            o_ref[:, h * ATT_HEAD_DIM:(h + 1) * ATT_HEAD_DIM] = (
                og[:, n * ATT_HEAD_DIM:(n + 1) * ATT_HEAD_DIM].astype(o_ref.dtype))


def _dsa_prompt(qt, qit, wit, kb, vbt, kib, b, n_sel):
    t = kb.shape[1]
    ck = min(512, t)
    nqb = t // Q_BLOCK
    idx_bits = max(1, math.ceil(math.log2(t))) + 1
    colblk = lambda n: pl.BlockSpec((n, Q_BLOCK), lambda bi, i: (0, bi * nqb + i))
    res = lambda n: pl.BlockSpec((1, t, n), lambda bi, i: (bi, 0, 0))
    gq = ATT_GROUP * Q_BLOCK
    return pl.pallas_call(
        functools.partial(_dsa_prompt_kernel, ck=ck, n_sel=n_sel, idx_bits=idx_bits),
        grid=(b, nqb),
        in_specs=[colblk(ATT_Q_DIM), colblk(IDX_HEADS * IDX_DIM), colblk(IDX_HEADS),
                  res(ATT_KV_DIM), pl.BlockSpec((ATT_KV_DIM, t), lambda bi, i: (0, bi)), res(IDX_DIM)],
        out_specs=pl.BlockSpec((Q_BLOCK, ATT_Q_DIM), lambda bi, i: (bi * nqb + i, 0)),
        out_shape=jax.ShapeDtypeStruct((b * t, ATT_Q_DIM), BF16),
        scratch_shapes=[pltpu.VMEM((t, Q_BLOCK), I32),
                        pltpu.VMEM((ATT_KV_HEADS, 1, gq), F32),
                        pltpu.VMEM((ATT_KV_HEADS, ATT_HEAD_DIM + BF16_ROWS, gq), F32),
                        pltpu.VMEM((2, ck, Q_BLOCK), BF16), pltpu.VMEM((ATT_KV_HEADS, ck, gq), F32),
                        pltpu.VMEM((2, ck, gq), BF16)],
        compiler_params=_cparams(("parallel", "arbitrary"), 56),
        name="dsa_prompt",
    )(qt, qit, wit, kb, vbt, kib)


def _outproj_ln_kernel(o_ref, h_ref, w_ref, g_ref, b_ref, out_ref):
    y = jnp.dot(o_ref[...], w_ref[...], preferred_element_type=F32)
    out_ref[...] = _layer_norm(DEEPNORM_ALPHA * h_ref[...] + y, g_ref[...], b_ref[...])


def _outproj_ln(o2d, h2d, w_out, g, b):
    m, kdim = o2d.shape
    tm = min(m, 512)
    row = lambda n: pl.BlockSpec((tm, n), lambda i: (i, 0))
    return pl.pallas_call(
        _outproj_ln_kernel,
        grid=(m // tm,),
        in_specs=[row(kdim), row(D_MODEL), pl.BlockSpec((kdim, D_MODEL), lambda i: (0, 0)),
                  pl.BlockSpec((1, D_MODEL), lambda i: (0, 0)), pl.BlockSpec((1, D_MODEL), lambda i: (0, 0))],
        out_specs=row(D_MODEL),
        out_shape=jax.ShapeDtypeStruct((m, D_MODEL), F32),
        compiler_params=_cparams(("parallel",)),
        name="outproj_ln",
    )(o2d, h2d, w_out.astype(BF16), g.reshape(1, -1), b.reshape(1, -1))


def _dsa_sample_kernel(pt_ref, qbd_ref, qi_ref, wi_ref, kn_ref, vn_ref, kin_ref,
                       cki_hbm, ck_hbm, cv_hbm, o_ref,
                       ki_buf, k_buf, v_buf, s_ref, keys_ref, cut_ref, sem,
                       *, n_pages, t_new, n_sel, idx_bits, layer_off, lchunk):
    b = pl.program_id(0)
    past = n_pages * PAGE_SIZE
    ltot = past + LANES
    rows = keys_ref.shape[0]

    def page_copy(hbm, buf, p, s):
        page = pt_ref[b, p] + layer_off
        return pltpu.make_async_copy(hbm.at[page],
                                     buf.at[:, pl.ds(pl.multiple_of(p * PAGE_SIZE, PAGE_SIZE), PAGE_SIZE)],
                                     sem.at[s])

    def start_all(p, c):
        page_copy(cki_hbm, ki_buf, p, 0).start()
        page_copy(ck_hbm, k_buf, p, 1).start()
        page_copy(cv_hbm, v_buf, p, 2).start()
        return c
    lax.fori_loop(0, n_pages, start_all, 0)

    def wait_all(hbm, buf, s):
        def body(p, c):
            page_copy(hbm, buf, p, s).wait()
            return c
        lax.fori_loop(0, n_pages, body, 0)

    qi = qi_ref[0]
    wcol = wi_ref[0]
    wait_all(cki_hbm, ki_buf, 0)

    def idx_scores(kic, dims=NN_DIMS):
        d = lax.dot_general(qi, kic, dims, preferred_element_type=F32)
        d = jnp.maximum(d, 0.0) * wcol
        return jnp.sum(d.reshape(t_new, IDX_HEADS, d.shape[1]), axis=1)

    keys_ref[...] = jnp.full(keys_ref.shape, INT_MIN, I32)
    for c in range(past // lchunk):
        sc = idx_scores(ki_buf[:, c * lchunk:(c + 1) * lchunk].astype(BF16))
        keys_ref[0:t_new, c * lchunk:(c + 1) * lchunk] = _sort_key(sc)
    scn = idx_scores(kin_ref[0], NT_DIMS)
    tpos = lax.broadcasted_iota(I32, (t_new, LANES), 0)
    lpos = lax.broadcasted_iota(I32, (t_new, LANES), 1)
    keys_ref[0:t_new, past:ltot] = jnp.where((lpos <= tpos) & (lpos < t_new), _sort_key(scn), INT_MIN)

    sck = max(w for w in range(LANES, 8 * LANES + 1, LANES) if ltot % w == 0)
    theta = _select_threshold(keys_ref, cut_ref, ltot // sck, sck, rows, n_sel, idx_bits)
    key = keys_ref[...]
    lane = lax.broadcasted_iota(I32, (rows, ltot), 1)
    sel = ((key > theta) | ((key == theta) & (lane < cut_ref[:, :1]))) & (key != INT_MIN)
    bias = jnp.where(sel, 0.0, NEG_INF)[0:t_new]
    bias = jnp.broadcast_to(bias[:, None, :], (t_new, ATT_HEADS, ltot)).reshape(t_new * ATT_HEADS, ltot)

    qbd = qbd_ref[0]
    wait_all(ck_hbm, k_buf, 1)
    for c in range(past // lchunk):
        kc = k_buf[:, c * lchunk:(c + 1) * lchunk].astype(BF16)
        s_ref[:, c * lchunk:(c + 1) * lchunk] = jnp.dot(qbd, kc, preferred_element_type=F32)
    s_ref[:, past:ltot] = lax.dot_general(qbd, kn_ref[0], NT_DIMS, preferred_element_type=F32)
    s = s_ref[...] + bias
    m = jnp.max(s, axis=1, keepdims=True)
    p = jnp.exp(s - m)
    linv = 1.0 / jnp.sum(p, axis=1, keepdims=True)
    s_ref[...] = p
    wait_all(cv_hbm, v_buf, 2)
    o = jnp.dot(s_ref[:, past:ltot].astype(BF16), vn_ref[0], preferred_element_type=F32)
    for c in range(past // lchunk):
        vc = v_buf[:, c * lchunk:(c + 1) * lchunk].astype(BF16)
        o = o + lax.dot_general(s_ref[:, c * lchunk:(c + 1) * lchunk].astype(BF16), vc, NT_DIMS,
                                preferred_element_type=F32)
    o = o * linv
    nrow = (lax.broadcasted_iota(I32, (t_new * ATT_HEADS, ATT_HEAD_DIM), 0) % ATT_HEADS) // ATT_GROUP
    out = jnp.zeros((t_new * ATT_HEADS, ATT_HEAD_DIM), F32)
    for n in range(ATT_KV_HEADS):
        out = out + jnp.where(nrow == n, o[:, n * ATT_HEAD_DIM:(n + 1) * ATT_HEAD_DIM], 0.0)
    o_ref[0] = out.astype(o_ref.dtype)


def _dsa_sample(page_table, qs, qis, wi, kb, vb, kib, cache_ki, cache_k, cache_v, layer_off, n_sel):
    b, t, _ = qs.shape
    n_pages = page_table.shape[1]
    past = n_pages * PAGE_SIZE
    ltot = past + LANES
    rows = 8
    q5 = qs.reshape(b, t, ATT_KV_HEADS, ATT_GROUP, 1, ATT_HEAD_DIM)
    eye = jnp.eye(ATT_KV_HEADS, dtype=qs.dtype).reshape(1, 1, ATT_KV_HEADS, 1, ATT_KV_HEADS, 1)
    qbd = (q5 * eye).reshape(b, t * ATT_HEADS, ATT_KV_DIM)
    qi = qis.reshape(b, t * IDX_HEADS, IDX_DIM)
    wcol = wi.reshape(b, t * IDX_HEADS, 1)
    padr = lambda a: jnp.pad(a, ((0, 0), (0, LANES - t), (0, 0)))
    kn, vn, kin = padr(kb), padr(vb), padr(kib)
    idx_bits = max(1, math.ceil(math.log2(ltot))) + 1
    lchunk = min(1024, past)
    bl = lambda a: pl.BlockSpec((1,) + a.shape[1:], lambda i, pt: (i, 0, 0))
    anyspec = pl.BlockSpec(memory_space=pl.ANY)
    grid_spec = pltpu.PrefetchScalarGridSpec(
        num_scalar_prefetch=1,
        grid=(b,),
        in_specs=[bl(qbd), bl(qi), bl(wcol), bl(kn), bl(vn), bl(kin), anyspec, anyspec, anyspec],
        out_specs=pl.BlockSpec((1, t * ATT_HEADS, ATT_HEAD_DIM), lambda i, pt: (i, 0, 0)),
        scratch_shapes=[pltpu.VMEM((IDX_DIM, past), F32), pltpu.VMEM((ATT_KV_DIM, past), F32),
                        pltpu.VMEM((ATT_KV_DIM, past), F32), pltpu.VMEM((t * ATT_HEADS, ltot), F32),
                        pltpu.VMEM((rows, ltot), I32), pltpu.VMEM((rows, LANES), I32),
                        pltpu.SemaphoreType.DMA((3,))],
    )
    o = pl.pallas_call(
        functools.partial(_dsa_sample_kernel, n_pages=n_pages, t_new=t, n_sel=n_sel, idx_bits=idx_bits,
                          layer_off=layer_off, lchunk=lchunk),
        grid_spec=grid_spec,
        out_shape=jax.ShapeDtypeStruct((b, t * ATT_HEADS, ATT_HEAD_DIM), BF16),
        compiler_params=_cparams(("arbitrary",), 56),
        name="dsa_sample",
    )(page_table, qbd, qi, wcol, kn, vn, kin, cache_ki, cache_k, cache_v)
    return o.reshape(b, t, ATT_Q_DIM)


def _gla_proj_kernel(x_ref, wq_ref, wk_ref, wv_ref, wg_ref, wr_ref, wup_ref, bg_ref,
                     q_ref, k_ref, v_ref, la_ref, r_ref):
    xb = x_ref[...].astype(BF16)
    q_ref[...] = jnp.dot(xb, wq_ref[...], preferred_element_type=F32) * (GLA_DK ** -0.5)
    k_ref[...] = jnp.dot(xb, wk_ref[...], preferred_element_type=F32)
    v_ref[...] = jnp.dot(xb, wv_ref[...], preferred_element_type=F32)
    r_ref[...] = jnp.dot(xb, wr_ref[...], preferred_element_type=F32)
    gd = jnp.dot(xb, wg_ref[...], preferred_element_type=F32)
    glogit = jnp.dot(gd.astype(BF16), wup_ref[...], preferred_element_type=F32) + bg_ref[...]
    la_ref[...] = jax.nn.log_sigmoid(glogit) / GLA_TAU


def _gla_proj(x2d, w_in, w_gate_up, b_gate):
    m = x2d.shape[0]
    tm = min(m, 512)
    dq = GLA_HEADS * GLA_DK
    dv = GLA_HEADS * GLA_DV
    wq = w_in[:, :dq].astype(BF16)
    wk = w_in[:, dq:2 * dq].astype(BF16)
    wv = w_in[:, 2 * dq:2 * dq + dv].astype(BF16)
    wg = jnp.pad(w_in[:, 2 * dq + dv:2 * dq + dv + GLA_GATE_RANK], ((0, 0), (0, LANES - GLA_GATE_RANK))).astype(BF16)
    wr = w_in[:, 2 * dq + dv + GLA_GATE_RANK:].astype(BF16)
    wup = jnp.pad(w_gate_up, ((0, LANES - GLA_GATE_RANK), (0, 0))).astype(BF16)
    row = lambda n: pl.BlockSpec((tm, n), lambda i: (i, 0))
    full = lambda a: pl.BlockSpec(a.shape, lambda i: (0, 0))
    bg = b_gate.reshape(1, -1)
    outs = [dq, dq, dv, dq, dv]
    return pl.pallas_call(
        _gla_proj_kernel,
        grid=(m // tm,),
        in_specs=[row(D_MODEL), full(wq), full(wk), full(wv), full(wg), full(wr), full(wup), full(bg)],
        out_specs=[row(n) for n in outs],
        out_shape=[jax.ShapeDtypeStruct((m, n), F32) for n in outs],
        compiler_params=_cparams(("parallel",)),
        name="gla_proj",
    )(x2d, wq, wk, wv, wg, wr, wup, bg)


def _gla_chunk(q, k, v, g, st, c):
    sub = GLA_SUB
    n_sub = c // sub
    ri = lax.broadcasted_iota(I32, (c, c), 0)
    ci = lax.broadcasted_iota(I32, (c, c), 1)
    tri = jnp.where(ci <= ri, 1.0, 0.0).astype(F32)
    bc = jnp.dot(tri, g, preferred_element_type=F32, precision=lax.Precision.HIGHEST)
    o = lax.dot_general((q * jnp.exp(bc)).astype(BF16), st.astype(BF16), NT_DIMS, preferred_element_type=F32)
    if n_sub > 1:
        mrow = jnp.concatenate([jnp.zeros((sub, GLA_DK), F32)] +
                               [jnp.broadcast_to(bc[i * sub - 1:i * sub], (sub, GLA_DK)) for i in range(1, n_sub)], axis=0)
        qh = (q * jnp.exp(bc - mrow)).astype(BF16)
        parts = [jnp.zeros((sub, GLA_DV), F32)]
        for i in range(1, n_sub):
            kh = (k[:i * sub] * jnp.exp(bc[i * sub - 1:i * sub] - bc[:i * sub])).astype(BF16)
            a = lax.dot_general(qh[i * sub:(i + 1) * sub], kh, NT_DIMS, preferred_element_type=F32)
            parts.append(jnp.dot(a.astype(BF16), v[:i * sub].astype(BF16), preferred_element_type=F32))
        o = o + jnp.concatenate(parts, axis=0)
    rsub = lax.broadcasted_iota(I32, (c, 1), 0) % sub
    for dlt in range(sub):
        ks = k if dlt == 0 else pltpu.roll(k, dlt, 0)
        bs = bc if dlt == 0 else pltpu.roll(bc, dlt, 0)
        vs = v if dlt == 0 else pltpu.roll(v, dlt, 0)
        w = jnp.sum(q * ks * jnp.exp(jnp.minimum(bc - bs, 0.0)), axis=1, keepdims=True)
        o = o + jnp.where(rsub >= dlt, w, 0.0) * vs
    bl = bc[c - 1:c]
    kt = (k * jnp.exp(bl - bc)).astype(BF16)
    st_new = st * jnp.exp(bl) + lax.dot_general(v.astype(BF16), kt, TN_DIMS, preferred_element_type=F32)
    return o, st_new


def _gla_kernel(q_ref, k_ref, v_ref, g_ref, r_ref, ng_ref, s0_ref, o_ref, sf_ref, st_ref, *, c, n_c):
    tb = pl.program_id(2)

    @pl.when(tb == 0)
    def _():
        st_ref[...] = s0_ref[0, 0].T

    st = st_ref[...]
    for ci in range(n_c):
        sl = slice(ci * c, (ci + 1) * c)
        o, st = _gla_chunk(q_ref[0, sl, :], k_ref[0, sl, :], v_ref[0, sl, :], g_ref[0, sl, :], st, c)
        o = o * lax.rsqrt(jnp.mean(o * o, axis=-1, keepdims=True) + LN_EPS) * ng_ref[...]
        o_ref[0, sl, :] = (o * jax.nn.silu(r_ref[0, sl, :])).astype(o_ref.dtype)
    st_ref[...] = st

    @pl.when(tb == pl.num_programs(2) - 1)
    def _():
        sf_ref[0, 0] = st.T


def _gla(q, k, v, la, r, norm_g, s0):
    b, t, _ = q.shape
    c = math.gcd(t, GLA_CHUNK)
    tblk = math.gcd(t, 16 * GLA_CHUNK)
    n_c = tblk // c
    kspec = pl.BlockSpec((1, tblk, GLA_DK), lambda bi, h, i: (bi, i, h))
    vspec = pl.BlockSpec((1, tblk, GLA_DV), lambda bi, h, i: (bi, i, h))
    sspec = pl.BlockSpec((1, 1, GLA_DK, GLA_DV), lambda bi, h, i: (bi, h, 0, 0))
    return pl.pallas_call(
        functools.partial(_gla_kernel, c=c, n_c=n_c),
        grid=(b, GLA_HEADS, t // tblk),
        in_specs=[kspec, kspec, vspec, kspec, vspec, pl.BlockSpec((1, GLA_DV), lambda bi, h, i: (0, 0)), sspec],
        out_specs=[vspec, sspec],
        out_shape=[jax.ShapeDtypeStruct((b, t, GLA_HEADS * GLA_DV), BF16),
                   jax.ShapeDtypeStruct((b, GLA_HEADS, GLA_DK, GLA_DV), F32)],
        scratch_shapes=[pltpu.VMEM((GLA_DV, GLA_DK), F32)],
        compiler_params=_cparams(("parallel", "parallel", "arbitrary")),
        name="gla_scan",
    )(q, k, v, la, r, norm_g.reshape(1, -1), s0)


def _gla_mixer(x, s0, w_in, w_gate_up, b_gate, norm_g):
    b, t, _ = x.shape
    q, k, v, la, r = _gla_proj(x.reshape(b * t, D_MODEL), w_in, w_gate_up, b_gate)
    tp = -(-t // GLA_SUB) * GLA_SUB
    r3 = lambda a: jnp.pad(a.reshape(b, t, -1), ((0, 0), (0, tp - t), (0, 0)))
    o, sf = _gla(r3(q), r3(k), r3(v), r3(la), r3(r), norm_g, s0)
    return o[:, :t], sf


def _top_values(s, k):
    vals = []
    cur = s
    for _ in range(k):
        mx = jnp.max(cur, axis=0, keepdims=True)
        vals.append(mx)
        cur = jnp.where(cur == mx, NEG_INF, cur)
    return vals


def _top_values_ranked(s, k):
    vals = []
    cur = s
    rank = jnp.full(s.shape, float(k), F32)
    for i in range(k):
        mx = jnp.max(cur, axis=0, keepdims=True)
        vals.append(mx)
        top = cur == mx
        rank = jnp.where(top, float(i), rank)
        cur = jnp.where(top, NEG_INF, cur)
    return vals, rank


def _peer_select_kernel(x_ref, wq_ref, k1_ref, k2_ref, c1_ref, p_ref, r2_ref, e_ref):
    xb = x_ref[...].astype(BF16)
    q = jnp.dot(xb, wq_ref[...], preferred_element_type=F32).astype(BF16)
    tt = q.shape[0]
    k1 = k1_ref[...]
    k2 = k2_ref[...]
    sub8 = lax.broadcasted_iota(I32, (8, tt), 0)
    for h in range(PEER_HEADS):
        s1 = lax.dot_general(k1, q[:, h * PEER_D_KEY:h * PEER_D_KEY + PEER_HALF], NT_DIMS,
                             preferred_element_type=F32)
        s2 = lax.dot_general(k2, q[:, h * PEER_D_KEY + PEER_HALF:(h + 1) * PEER_D_KEY], NT_DIMS,
                             preferred_element_type=F32)
        t1, rank1 = _top_values_ranked(s1, PEER_TOPK)
        t2, rank2 = _top_values_ranked(s2, PEER_TOPK)
        t2a = jnp.concatenate(t2, axis=0)
        cands = [t1[0] + t2a]
        for a in range(1, PEER_TOPK):
            nb = PEER_TOPK // (a + 1)
            cands.append(jnp.where(sub8 < nb, t1[a] + t2a[:8], NEG_INF))
        best = _top_values(jnp.concatenate(cands, axis=0), PEER_TOPK)
        theta = best[PEER_TOPK - 1]
        z = best[0] * 0.0
        for bv in best:
            z = z + jnp.exp(bv - best[0])
        count1 = jnp.zeros(s1.shape, F32)
        for a in range(PEER_TOPK):
            cnt_a = jnp.sum(jnp.where(t1[a] + t2a >= theta, 1.0, 0.0), axis=0, keepdims=True)
            count1 = jnp.where(rank1 == float(a), cnt_a, count1)
        c1_ref[h] = count1
        p_ref[h] = jnp.exp(s1 - t1[0]) * (0.5 / z)
        r2_ref[h] = rank2
        e_ref[h] = jnp.exp(s2 - t2[0])


def _peer_select(x2d, w_q, keys1, keys2):
    n = x2d.shape[0]
    tt = min(n, 256)
    wq = w_q.astype(BF16)
    k1 = keys1.astype(BF16)
    k2 = keys2.astype(BF16)
    full = lambda a: pl.BlockSpec(a.shape, lambda i: (0,) * a.ndim)
    hspec = pl.BlockSpec((None, PEER_HEADS, PEER_N_KEYS, tt), lambda i: (i, 0, 0, 0))
    big = lambda dt: jax.ShapeDtypeStruct((n // tt, PEER_HEADS, PEER_N_KEYS, tt), dt)
    return pl.pallas_call(
        _peer_select_kernel,
        grid=(n // tt,),
        in_specs=[pl.BlockSpec((tt, D_MODEL), lambda i: (i, 0)), full(wq), full(k1), full(k2)],
        out_specs=[hspec, hspec, hspec, hspec],
        out_shape=[big(F32), big(F32), big(F32), big(F32)],
        compiler_params=_cparams(("parallel",)),
        name="peer_select",
    )(x2d, wq, k1, k2)


def _peer_mix_kernel(h_ref, c1_ref, p_ref, r2_ref, e_ref, u_ref, vt_ref, g_ref, bb_ref, out_ref,
                     xb_ref, ht_ref, wt_ref, yt_ref, *, n_i1):
    eb = pl.program_id(1)
    tt = h_ref.shape[0]

    @pl.when(eb == 0)
    def _():
        xb_ref[...] = h_ref[...].T.astype(BF16)
        yt_ref[...] = jnp.zeros(yt_ref.shape, F32)

    ht_ref[...] = jnp.dot(u_ref[...], xb_ref[...], preferred_element_type=F32)

    i1_base = pl.multiple_of(eb * n_i1, n_i1)

    sw = c1_ref.shape[-1]
    for lg in range(tt // LANES):
        ls = slice(lg * LANES, (lg + 1) * LANES)
        st = (lg * LANES) // sw
        sl = slice((lg * LANES) % sw, (lg * LANES) % sw + LANES)
        c8 = [c1_ref[st, h, pl.ds(i1_base, n_i1), sl] for h in range(PEER_HEADS)]
        p8 = [p_ref[st, h, pl.ds(i1_base, n_i1), sl] for h in range(PEER_HEADS)]
        for j in range(n_i1):
            cb = [jnp.broadcast_to(c8[h][j:j + 1], (BF16_ROWS, LANES)) for h in range(PEER_HEADS)]
            pb = [jnp.broadcast_to(p8[h][j:j + 1], (BF16_ROWS, LANES)) for h in range(PEER_HEADS)]
            for r in range(PEER_N_KEYS // BF16_ROWS):
                ks = slice(r * BF16_ROWS, (r + 1) * BF16_ROWS)
                acc = jnp.zeros((BF16_ROWS, LANES), F32)
                for h in range(PEER_HEADS):
                    acc = acc + jnp.where(r2_ref[st, h, ks, sl] < cb[h], pb[h] * e_ref[st, h, ks, sl], 0.0)
                rs = slice(j * PEER_N_KEYS + r * BF16_ROWS, j * PEER_N_KEYS + (r + 1) * BF16_ROWS)
                x = ht_ref[rs, ls]
                gelu2 = x * (1.0 + lax.erf(x * (2.0 ** -0.5)))
                wt_ref[rs, ls] = (gelu2 * acc).astype(BF16)

    yt_ref[...] += jnp.dot(vt_ref[...], wt_ref[...], preferred_element_type=F32)

    @pl.when(eb == pl.num_programs(1) - 1)
    def _():
        y = yt_ref[...].T
        out_ref[...] = _layer_norm(DEEPNORM_ALPHA * h_ref[...] + y, g_ref[...], bb_ref[...])


def _block_transpose_kernel(v_ref, o_ref):
    o_ref[...] = v_ref[...].T.astype(o_ref.dtype)


def _cast_kernel(x_ref, o_ref):
    o_ref[...] = x_ref[...].astype(o_ref.dtype)


def _key_table(u_all, layer, eblk):
    _, n_e, d = u_all.shape
    return pl.pallas_call(
        _cast_kernel,
        grid=(n_e // eblk,),
        in_specs=[pl.BlockSpec((None, eblk, d), lambda j: (layer, j, 0))],
        out_specs=pl.BlockSpec((eblk, d), lambda j: (j, 0)),
        out_shape=jax.ShapeDtypeStruct((n_e, d), BF16),
        compiler_params=_cparams(("parallel",)),
        name="key_table",
    )(u_all)


def _value_blocks(v_all, layer, eblk):
    _, n_e, d = v_all.shape
    return pl.pallas_call(
        _block_transpose_kernel,
        grid=(n_e // eblk,),
        in_specs=[pl.BlockSpec((None, eblk, d), lambda j: (layer, j, 0))],
        out_specs=pl.BlockSpec((None, d, eblk), lambda j: (j, 0, 0)),
        out_shape=jax.ShapeDtypeStruct((n_e // eblk, d, eblk), BF16),
        compiler_params=_cparams(("parallel",)),
        name="value_blocks",
    )(v_all)


def _peer_ln(h2d, w_q, keys1, keys2, ub, vt, g, b):
    n_real = h2d.shape[0]
    if n_real % LANES:
        h2d = jnp.pad(h2d, ((0, LANES - n_real % LANES), (0, 0)))
        return _peer_ln(h2d, w_q, keys1, keys2, ub, vt, g, b)[:n_real]
    n = n_real
    c1, p, r2, e = _peer_select(h2d, w_q, keys1, keys2)
    tt = min(n, 512)
    n_blk, _, eblk = vt.shape
    sw = c1.shape[-1]
    hspec = pl.BlockSpec((tt // sw, PEER_HEADS, PEER_N_KEYS, sw), lambda i, j: (i, 0, 0, 0))
    vec = pl.BlockSpec((1, D_MODEL), lambda i, j: (0, 0))
    return pl.pallas_call(
        functools.partial(_peer_mix_kernel, n_i1=eblk // PEER_N_KEYS),
        grid=(n // tt, n_blk),
        in_specs=[pl.BlockSpec((tt, D_MODEL), lambda i, j: (i, 0)), hspec, hspec, hspec, hspec,
                  pl.BlockSpec((eblk, D_MODEL), lambda i, j: (j, 0)),
                  pl.BlockSpec((None, D_MODEL, eblk), lambda i, j: (j, 0, 0)), vec, vec],
        out_specs=pl.BlockSpec((tt, D_MODEL), lambda i, j: (i, 0)),
        out_shape=jax.ShapeDtypeStruct((n, D_MODEL), F32),
        scratch_shapes=[pltpu.VMEM((D_MODEL, tt), BF16), pltpu.VMEM((eblk, tt), F32),
                        pltpu.VMEM((eblk, tt), BF16), pltpu.VMEM((D_MODEL, tt), F32)],
        compiler_params=_cparams(("parallel", "arbitrary"), 56),
        name="peer_mix",
    )(h2d, c1, p, r2, e, ub, vt, g.reshape(1, -1), b.reshape(1, -1))


def kernel(x_prompt, x_sample, cache_k, cache_v, cache_kidx, state_gla, page_table,
           attn_w_in, attn_w_out, gla_w_in, gla_w_gate_up, gla_b_gate, gla_norm_g, gla_w_out,
           peer_w_q, peer_keys1, peer_keys2, peer_u, peer_v, ln1_g, ln1_b, ln2_g, ln2_b):
    bp, tp, _ = x_prompt.shape
    bs, ts, _ = x_sample.shape
    n_pool = cache_k.shape[1]
    past = page_table.shape[1] * PAGE_SIZE
    ck = jnp.swapaxes(cache_k.reshape(-1, PAGE_SIZE, ATT_KV_DIM), 1, 2)
    cv = jnp.swapaxes(cache_v.reshape(-1, PAGE_SIZE, ATT_KV_DIM), 1, 2)
    cki = jnp.swapaxes(cache_kidx.reshape(-1, PAGE_SIZE, IDX_DIM), 1, 2)
    hp = x_prompt.reshape(bp * tp, D_MODEL)
    hs = x_sample.reshape(bs * ts, D_MODEL)
    kp_l, vp_l, kip_l, sp_l = [], [], [], []
    ks_l, vs_l, kis_l, ss_l = [], [], [], []
    for i in range(DEPTH):
        j = i // 2
        if i % 2 == 0:
            k, v, ki, kb, kib, qt, qit, wit, vbt = _attn_proj_t(hp, attn_w_in[j])
            r3 = lambda a: a.reshape(bp, tp, -1)
            op = _dsa_prompt(qt, qit, wit, r3(kb), vbt, r3(kib), bp, min(TOPK_MAX, tp // 4))
            kp_l.append(k.reshape(bp, tp, ATT_KV_HEADS, ATT_HEAD_DIM))
            vp_l.append(v.reshape(bp, tp, ATT_KV_HEADS, ATT_HEAD_DIM))
            kip_l.append(ki.reshape(bp, tp, IDX_DIM))
            qs, k, v, kb, vb, qis, ki, kib, wi = _attn_proj(hs, attn_w_in[j])
            r3 = lambda a: a.reshape(bs, ts, -1)
            os_ = _dsa_sample(page_table, r3(qs), r3(qis), r3(wi), r3(kb), r3(vb), r3(kib), cki, ck, cv,
                              j * n_pool, min(TOPK_MAX, (past + ts) // 4))
            ks_l.append(k.reshape(bs, ts, ATT_KV_HEADS, ATT_HEAD_DIM))
            vs_l.append(v.reshape(bs, ts, ATT_KV_HEADS, ATT_HEAD_DIM))
            kis_l.append(ki.reshape(bs, ts, IDX_DIM))
            w_out = attn_w_out[j]
        else:
            s0 = jnp.zeros((bp, GLA_HEADS, GLA_DK, GLA_DV), F32)
            op, sp = _gla_mixer(hp.reshape(bp, tp, D_MODEL), s0, gla_w_in[j], gla_w_gate_up[j], gla_b_gate[j],
                                gla_norm_g[j])
            os_, ss = _gla_mixer(hs.reshape(bs, ts, D_MODEL), state_gla[j], gla_w_in[j], gla_w_gate_up[j],
                                 gla_b_gate[j], gla_norm_g[j])
            sp_l.append(sp)
            ss_l.append(ss)
            w_out = gla_w_out[j]
        hp = _outproj_ln(op.reshape(bp * tp, -1), hp, w_out, ln1_g[i], ln1_b[i])
        hs = _outproj_ln(os_.reshape(bs * ts, -1), hs, w_out, ln1_g[i], ln1_b[i])
        pw = (peer_w_q[i], peer_keys1[i], peer_keys2[i], _key_table(peer_u, i, PEER_EXPERT_BLOCK),
              _value_blocks(peer_v, i, PEER_EXPERT_BLOCK), ln2_g[i], ln2_b[i])
        hp = _peer_ln(hp, *pw)
        hs = _peer_ln(hs, *pw)
    return (hp.reshape(bp, tp, D_MODEL), hs.reshape(bs, ts, D_MODEL),
            jnp.stack(kp_l), jnp.stack(vp_l), jnp.stack(kip_l), jnp.stack(sp_l),
            jnp.stack(ks_l), jnp.stack(vs_l), jnp.stack(kis_l), jnp.stack(ss_l))
```

```python
import functools
import math

import jax
import jax.numpy as jnp
from jax import lax
from jax.experimental import pallas as pl
from jax.experimental.pallas import tpu as pltpu

F32 = jnp.float32
BF16 = jnp.bfloat16
I32 = jnp.int32

D_MODEL = 1024
DEPTH = 2
PAGE_SIZE = 128
DEEPNORM_ALPHA = (2.0 * DEPTH) ** 0.25
LN_EPS = 1e-5

ATT_HEADS = 16
ATT_KV_HEADS = 4
ATT_HEAD_DIM = 64
ATT_GROUP = ATT_HEADS // ATT_KV_HEADS
IDX_HEADS = 8
IDX_DIM = 64
TOPK_MAX = 256
Q_BLOCK = 128
ATT_Q_DIM = ATT_HEADS * ATT_HEAD_DIM
ATT_KV_DIM = ATT_KV_HEADS * ATT_HEAD_DIM

GLA_HEADS = 4
GLA_DK = D_MODEL // 2 // GLA_HEADS
GLA_DV = D_MODEL // GLA_HEADS
GLA_GATE_RANK = 16
GLA_TAU = 16.0
GLA_CHUNK = 64
GLA_SUB = 16

PEER_HEADS = 8
PEER_N_KEYS = 128
PEER_D_KEY = 256
PEER_HALF = PEER_D_KEY // 2
PEER_TOPK = 16
PEER_EXPERT_BLOCK = 8 * PEER_N_KEYS

LANES = 128
BF16_ROWS = 16
LOG2_E = 1.4426950408889634
INT_MIN = -(2 ** 31)
NEG_INF = float("-inf")
MASKED = -1e30

NN_DIMS = (((1,), (0,)), ((), ()))
NT_DIMS = (((1,), (1,)), ((), ()))
TN_DIMS = (((0,), (0,)), ((), ()))


def _cparams(sem, vmem_mib=None, flags=None):
    kw = dict(dimension_semantics=sem)
    if vmem_mib is not None:
        kw["vmem_limit_bytes"] = vmem_mib * 1024 * 1024
    if flags:
        kw["flags"] = flags
    return pltpu.CompilerParams(**kw)


def _sort_key(x):
    b = pltpu.bitcast(x + 0.0, I32)
    return b ^ ((b >> 31) & 0x7FFFFFFF)


def _layer_norm(z, g, b):
    mu = jnp.mean(z, axis=-1, keepdims=True)
    zc = z - mu
    var = jnp.mean(zc * zc, axis=-1, keepdims=True)
    return zc * lax.rsqrt(var + LN_EPS) * g + b


def _row_count(keys_ref, n_chunks, ck, rows, pred):
    def body(c, acc):
        base = pl.multiple_of(c * ck, ck)
        return acc + jnp.where(pred(keys_ref[:, pl.ds(base, ck)], base), 1.0, 0.0)
    acc = lax.fori_loop(0, n_chunks, body, jnp.zeros((rows, ck), F32))
    return jnp.sum(acc, axis=1, keepdims=True)


def _select_threshold(keys_ref, cut_ref, n_chunks, ck, rows, kth, idx_bits):
    kth_f = float(kth)

    def count_ge(t):
        tb = jnp.broadcast_to(t, (rows, ck))
        return _row_count(keys_ref, n_chunks, ck, rows, lambda blk, base: blk >= tb)

    theta = jnp.full((rows, 1), INT_MIN, I32)
    zero = jnp.zeros((rows, 1), I32)
    theta = jnp.where(count_ge(zero) >= kth_f, zero, theta)

    def bit_body(i, t):
        cand = t | lax.shift_left(jnp.int32(1), jnp.int32(30) - i)
        return jnp.where(count_ge(cand) >= kth_f, cand, t)
    theta = lax.fori_loop(0, 31, bit_body, theta)

    n_gt = count_ge(theta + 1)
    n_ge = count_ge(theta)
    need = kth_f - n_gt
    ambiguous = (n_ge - n_gt > need) & (theta > INT_MIN)
    cut_ref[...] = jnp.full(cut_ref.shape, 2 ** idx_bits, I32)

    @pl.when(jnp.max(jnp.where(ambiguous, 1.0, 0.0)) > 0.5)
    def _():
        thb = jnp.broadcast_to(theta, (rows, ck))
        lane = lax.broadcasted_iota(I32, (rows, ck), 1)

        def ties_before(jc):
            jb = jnp.broadcast_to(jc, (rows, ck))
            return _row_count(keys_ref, n_chunks, ck, rows,
                              lambda blk, base: (blk == thb) & (lane + base < jb))

        def jbit(i, jcur):
            cand = jcur | lax.shift_left(jnp.int32(1), jnp.int32(idx_bits - 1) - i)
            return jnp.where(ties_before(cand) <= need, cand, jcur)
        jfin = lax.fori_loop(0, idx_bits, jbit, jnp.zeros((rows, 1), I32))
        jfin = jnp.where(ambiguous, jfin, 2 ** idx_bits)
        cut_ref[...] = jnp.broadcast_to(jfin, cut_ref.shape)
    return theta


def _attn_proj_kernel(x_ref, wq_ref, wkv_ref, wqi_ref, wkw_ref,
                      q_ref, k_ref, v_ref, kb_ref, vb_ref, qi_ref, ki_ref, kib_ref, wi_ref):
    xb = x_ref[...].astype(BF16)
    q = jnp.dot(xb, wq_ref[...], preferred_element_type=F32)
    q_ref[...] = (q * (ATT_HEAD_DIM ** -0.5)).astype(BF16)
    kv = jnp.dot(xb, wkv_ref[...], preferred_element_type=F32)
    k = kv[:, :ATT_KV_DIM]
    v = kv[:, ATT_KV_DIM:]
    k_ref[...] = k
    v_ref[...] = v
    kb_ref[...] = k.astype(BF16)
    vb_ref[...] = v.astype(BF16)
    qi = jnp.dot(xb, wqi_ref[...], preferred_element_type=F32)
    qi_ref[...] = (qi * (IDX_DIM ** -0.5)).astype(BF16)
    kw = jnp.dot(xb, wkw_ref[...], preferred_element_type=F32)
    ki = kw[:, :IDX_DIM]
    ki_ref[...] = ki
    kib_ref[...] = ki.astype(BF16)
    wi_ref[...] = kw[:, IDX_DIM:IDX_DIM + IDX_HEADS] * (IDX_HEADS ** -0.5)


def _attn_weights(w_in):
    s0 = ATT_Q_DIM
    s2 = s0 + 2 * ATT_KV_DIM
    s3 = s2 + IDX_HEADS * IDX_DIM
    wkw = jnp.pad(w_in[:, s3:], ((0, 0), (0, LANES - (w_in.shape[1] - s3))))
    return (w_in[:, :s0].astype(BF16), w_in[:, s0:s2].astype(BF16), w_in[:, s2:s3].astype(BF16),
            wkw.astype(BF16))


def _attn_proj(x2d, w_in):
    m = x2d.shape[0]
    tm = min(m, 512)
    wq, wkv, wqi, wkw = _attn_weights(w_in)
    row = lambda n: pl.BlockSpec((tm, n), lambda i: (i, 0))
    full = lambda a: pl.BlockSpec(a.shape, lambda i: (0, 0))
    outs = [(ATT_Q_DIM, BF16), (ATT_KV_DIM, F32), (ATT_KV_DIM, F32), (ATT_KV_DIM, BF16),
            (ATT_KV_DIM, BF16), (IDX_HEADS * IDX_DIM, BF16), (IDX_DIM, F32), (IDX_DIM, BF16),
            (IDX_HEADS, F32)]
    return pl.pallas_call(
        _attn_proj_kernel,
        grid=(m // tm,),
        in_specs=[row(D_MODEL), full(wq), full(wkv), full(wqi), full(wkw)],
        out_specs=[row(n) for n, _ in outs],
        out_shape=[jax.ShapeDtypeStruct((m, n), dt) for n, dt in outs],
        compiler_params=_cparams(("parallel",)),
        name="attn_proj",
    )(x2d, wq, wkv, wqi, wkw)


def _attn_proj_t_kernel(x_ref, wq_ref, wkv_ref, wqi_ref, wkw_ref,
                        k_ref, v_ref, ki_ref, kb_ref, kib_ref, qt_ref, qit_ref, wit_ref, vbt_ref):
    xb = x_ref[...].astype(BF16)
    q = jnp.dot(xb, wq_ref[...], preferred_element_type=F32)
    qt_ref[...] = (q * (ATT_HEAD_DIM ** -0.5 * LOG2_E)).T.astype(BF16)
    kv = jnp.dot(xb, wkv_ref[...], preferred_element_type=F32)
    k = kv[:, :ATT_KV_DIM]
    v = kv[:, ATT_KV_DIM:]
    k_ref[...] = k
    v_ref[...] = v
    kb_ref[...] = k.astype(BF16)
    vbt_ref[...] = v.T.astype(BF16)
    qi = jnp.dot(xb, wqi_ref[...], preferred_element_type=F32)
    qit_ref[...] = (qi * (IDX_DIM ** -0.5)).T.astype(BF16)
    kw = jnp.dot(xb, wkw_ref[...], preferred_element_type=F32)
    ki = kw[:, :IDX_DIM]
    ki_ref[...] = ki
    kib_ref[...] = ki.astype(BF16)
    wit_ref[...] = kw.T[IDX_DIM:IDX_DIM + IDX_HEADS, :] * (IDX_HEADS ** -0.5)


def _attn_proj_t(x2d, w_in):
    m = x2d.shape[0]
    tm = min(m, 512)
    wq, wkv, wqi, wkw = _attn_weights(w_in)
    row = lambda n: pl.BlockSpec((tm, n), lambda i: (i, 0))
    col = lambda n: pl.BlockSpec((n, tm), lambda i: (0, i))
    full = lambda a: pl.BlockSpec(a.shape, lambda i: (0, 0))
    nat = [(ATT_KV_DIM, F32), (ATT_KV_DIM, F32), (IDX_DIM, F32), (ATT_KV_DIM, BF16), (IDX_DIM, BF16)]
    tr = [(ATT_Q_DIM, BF16), (IDX_HEADS * IDX_DIM, BF16), (IDX_HEADS, F32), (ATT_KV_DIM, BF16)]
    return pl.pallas_call(
        _attn_proj_t_kernel,
        grid=(m // tm,),
        in_specs=[row(D_MODEL), full(wq), full(wkv), full(wqi), full(wkw)],
        out_specs=[row(n) for n, _ in nat] + [col(n) for n, _ in tr],
        out_shape=[jax.ShapeDtypeStruct((m, n), dt) for n, dt in nat] +
                  [jax.ShapeDtypeStruct((n, m), dt) for n, dt in tr],
        compiler_params=_cparams(("parallel",)),
        name="attn_proj_t",
    )(x2d, wq, wkv, wqi, wkw)


def _col_count(keys_ref, n_chunks, ck, pred):
    width = keys_ref.shape[1]
    par = math.gcd(ck, 64)

    def body(c, acc):
        base = pl.multiple_of(c * ck, ck)
        hit = pred(keys_ref[pl.ds(base, ck), :], base)
        return acc + jnp.sum(jnp.where(hit, 1.0, 0.0).reshape(ck // par, par, width), axis=0)
    acc = lax.fori_loop(0, n_chunks, body, jnp.zeros((par, width), F32))
    return jnp.sum(acc, axis=0, keepdims=True)


def _select_threshold_cols(keys_ref, n_chunks, ck, kth, idx_bits):
    width = keys_ref.shape[1]
    kth_f = float(kth)

    def count_ge(t):
        return _col_count(keys_ref, n_chunks, ck, lambda blk, base: blk >= t)

    theta = jnp.full((1, width), INT_MIN, I32)
    zero = jnp.zeros((1, width), I32)
    theta = jnp.where(count_ge(zero) >= kth_f, zero, theta)

    def bit_body(i, t):
        cand = t | lax.shift_left(jnp.int32(1), jnp.int32(30) - i)
        return jnp.where(count_ge(cand) >= kth_f, cand, t)
    theta = lax.fori_loop(0, 31, bit_body, theta)

    n_gt = count_ge(theta + 1)
    n_ge = count_ge(theta)
    need = kth_f - n_gt
    ambiguous = (n_ge - n_gt > need) & (theta > INT_MIN)
    no_cut = jnp.full((1, width), 2 ** idx_bits, I32)

    def tie_search():
        row = lax.broadcasted_iota(I32, (ck, width), 0)

        def ties_before(jc):
            return _col_count(keys_ref, n_chunks, ck, lambda blk, base: (blk == theta) & (row + base < jc))

        def jbit(i, jcur):
            cand = jcur | lax.shift_left(jnp.int32(1), jnp.int32(idx_bits - 1) - i)
            return jnp.where(ties_before(cand) <= need, cand, jcur)
        jfin = lax.fori_loop(0, idx_bits, jbit, jnp.zeros((1, width), I32))
        return jnp.where(ambiguous, jfin, no_cut)

    cut = lax.cond(jnp.max(jnp.where(ambiguous, 1.0, 0.0)) > 0.5, tie_search, lambda: no_cut)
    return theta, cut


def _dsa_prompt_kernel(qt_ref, qit_ref, wit_ref, kb_ref, vbt_ref, kib_ref, o_ref,
                       keys_ref, m_ref, acc_ref, bias_ref, s_ref, p_ref, *, ck, n_sel, idx_bits):
    i = pl.program_id(1)
    qb = Q_BLOCK
    n_chunks = ((i + 1) * qb + ck - 1) // ck
    q_pos = i * qb + lax.broadcasted_iota(I32, (ck, qb), 1)
    row = lax.broadcasted_iota(I32, (ck, qb), 0)

    qit = qit_ref[...]
    qi_all = jnp.concatenate([qit[h * IDX_DIM:(h + 1) * IDX_DIM, :] for h in range(IDX_HEADS)], axis=1)
    wit = wit_ref[...]

    def score_body(c, carry):
        base = pl.multiple_of(c * ck, ck)
        d = jnp.dot(kib_ref[0, pl.ds(base, ck), :], qi_all, preferred_element_type=F32)
        acc = jnp.zeros((ck, qb), F32)
        for h in range(IDX_HEADS):
            acc = acc + jnp.maximum(d[:, h * qb:(h + 1) * qb], 0.0) * wit[h:h + 1, :]
        keys_ref[pl.ds(base, ck), :] = jnp.where(row + base <= q_pos, _sort_key(acc), INT_MIN)
        return carry
    lax.fori_loop(0, n_chunks, score_body, 0)

    theta, cut = _select_threshold_cols(keys_ref, n_chunks, ck, n_sel, idx_bits)

    qt = qt_ref[...]
    qn = [jnp.concatenate([qt[(n * ATT_GROUP + g) * ATT_HEAD_DIM:(n * ATT_GROUP + g + 1) * ATT_HEAD_DIM, :]
                           for g in range(ATT_GROUP)], axis=1) for n in range(ATT_KV_HEADS)]
    m_ref[...] = jnp.full(m_ref.shape, NEG_INF, F32)
    acc_ref[...] = jnp.zeros(acc_ref.shape, F32)
    hd = ATT_HEAD_DIM
    ones = jnp.ones((acc_ref.shape[1] - hd, ck), BF16)
    par = math.gcd(ck, 64)

    n_sub = ck // par
    pair = 2

    eye = (lax.broadcasted_iota(I32, (qb, ATT_GROUP * qb), 1) % qb ==
           lax.broadcasted_iota(I32, (qb, ATT_GROUP * qb), 0))
    eye = jnp.where(eye, 1.0, 0.0).astype(BF16)
    qa = [jnp.concatenate([eye, qn[n]], axis=0) for n in range(ATT_KV_HEADS)]

    def mask_rows(kbase, slot):
        key = keys_ref[pl.ds(kbase, ck), :]
        sel = ((key > theta) | ((key == theta) & (row + kbase < cut))) & (key != INT_MIN)
        bias_ref[slot] = jnp.where(sel, 0.0, MASKED).astype(BF16)

    def scores(kbase, slot, n):
        kc = jnp.concatenate([bias_ref[slot], kb_ref[0, pl.ds(kbase, ck), n * hd:(n + 1) * hd]], axis=1)
        s_ref[n] = jnp.dot(kc, qa[n], preferred_element_type=F32)
    mask_rows(0, 0)
    for n in range(pair):
        scores(0, 0, n)

    def attn_body(c, carry):
        base = pl.multiple_of(c * ck, ck)
        slot = c % 2

        for n0 in range(0, ATT_KV_HEADS, pair):
            heads = range(n0, n0 + pair)
            if n0 + pair < ATT_KV_HEADS:
                for n in heads:
                    scores(base, slot, n + pair)
            else:
                nbase = pl.multiple_of(jnp.minimum(c + 1, n_chunks - 1) * ck, ck)
                mask_rows(nbase, 1 - slot)
                for n in heads:
                    scores(nbase, 1 - slot, n - n0)

            def masked(n, j):
                return s_ref[n, j * par:(j + 1) * par, :]
            mx = {n: masked(n, 0) for n in heads}
            for j in range(1, n_sub):
                for n in heads:
                    mx[n] = jnp.maximum(mx[n], masked(n, j))
            m_safe, alpha, m_new = {}, {}, {}
            for n in heads:
                m_prev = m_ref[n]
                m_new[n] = jnp.maximum(m_prev, jnp.max(mx[n], axis=0, keepdims=True))
                m_safe[n] = jnp.where(m_new[n] == NEG_INF, 0.0, m_new[n])
                alpha[n] = jnp.exp2(m_prev - m_safe[n])
            for j in range(n_sub):
                for n in heads:
                    p_ref[n - n0, j * par:(j + 1) * par, :] = jnp.exp2((masked(n, j) - m_safe[n]).astype(BF16))
            for n in heads:
                vtc = jnp.concatenate([vbt_ref[n * hd:(n + 1) * hd, pl.ds(base, ck)], ones], axis=0)
                acc_ref[n] = alpha[n] * acc_ref[n] + jnp.dot(vtc, p_ref[n - n0], preferred_element_type=F32)
                m_ref[n] = m_new[n]
        return carry
    lax.fori_loop(0, n_chunks, attn_body, 0)

    ot = jnp.concatenate([acc_ref[n, :hd] / acc_ref[n, hd:hd + 1] for n in range(ATT_KV_HEADS)], axis=0)
    for g in range(ATT_GROUP):
        og = ot[:, g * qb:(g + 1) * qb].T
        for n in range(ATT_KV_HEADS):
            h = n * ATT_GROUP + g
            o_ref[:, h * ATT_HEAD_DIM:(h + 1) * ATT_HEAD_DIM] = (
                og[:, n * ATT_HEAD_DIM:(n + 1) * ATT_HEAD_DIM].astype(o_ref.dtype))


def _dsa_prompt(qt, qit, wit, kb, vbt, kib, b, n_sel):
    t = kb.shape[1]
    ck = min(512, t)
    nqb = t // Q_BLOCK
    idx_bits = max(1, math.ceil(math.log2(t))) + 1
    colblk = lambda n: pl.BlockSpec((n, Q_BLOCK), lambda bi, i: (0, bi * nqb + i))
    res = lambda n: pl.BlockSpec((1, t, n), lambda bi, i: (bi, 0, 0))
    gq = ATT_GROUP * Q_BLOCK
    return pl.pallas_call(
        functools.partial(_dsa_prompt_kernel, ck=ck, n_sel=n_sel, idx_bits=idx_bits),
        grid=(b, nqb),
        in_specs=[colblk(ATT_Q_DIM), colblk(IDX_HEADS * IDX_DIM), colblk(IDX_HEADS),
                  res(ATT_KV_DIM), pl.BlockSpec((ATT_KV_DIM, t), lambda bi, i: (0, bi)), res(IDX_DIM)],
        out_specs=pl.BlockSpec((Q_BLOCK, ATT_Q_DIM), lambda bi, i: (bi * nqb + i, 0)),
        out_shape=jax.ShapeDtypeStruct((b * t, ATT_Q_DIM), BF16),
        scratch_shapes=[pltpu.VMEM((t, Q_BLOCK), I32),
                        pltpu.VMEM((ATT_KV_HEADS, 1, gq), F32),
                        pltpu.VMEM((ATT_KV_HEADS, ATT_HEAD_DIM + BF16_ROWS, gq), F32),
                        pltpu.VMEM((2, ck, Q_BLOCK), BF16), pltpu.VMEM((ATT_KV_HEADS, ck, gq), F32),
                        pltpu.VMEM((2, ck, gq), BF16)],
        compiler_params=_cparams(("parallel", "arbitrary"), 56),
        name="dsa_prompt",
    )(qt, qit, wit, kb, vbt, kib)


def _outproj_ln_kernel(o_ref, h_ref, w_ref, g_ref, b_ref, out_ref):
    y = jnp.dot(o_ref[...], w_ref[...], preferred_element_type=F32)
    out_ref[...] = _layer_norm(DEEPNORM_ALPHA * h_ref[...] + y, g_ref[...], b_ref[...])


def _outproj_ln(o2d, h2d, w_out, g, b):
    m, kdim = o2d.shape
    tm = min(m, 512)
    row = lambda n: pl.BlockSpec((tm, n), lambda i: (i, 0))
    return pl.pallas_call(
        _outproj_ln_kernel,
        grid=(m // tm,),
        in_specs=[row(kdim), row(D_MODEL), pl.BlockSpec((kdim, D_MODEL), lambda i: (0, 0)),
                  pl.BlockSpec((1, D_MODEL), lambda i: (0, 0)), pl.BlockSpec((1, D_MODEL), lambda i: (0, 0))],
        out_specs=row(D_MODEL),
        out_shape=jax.ShapeDtypeStruct((m, D_MODEL), F32),
        compiler_params=_cparams(("parallel",)),
        name="outproj_ln",
    )(o2d, h2d, w_out.astype(BF16), g.reshape(1, -1), b.reshape(1, -1))


def _dsa_sample_kernel(pt_ref, qbd_ref, qi_ref, wi_ref, kn_ref, vn_ref, kin_ref,
                       cki_hbm, ck_hbm, cv_hbm, o_ref,
                       ki_buf, k_buf, v_buf, s_ref, keys_ref, cut_ref, sem,
                       *, n_pages, t_new, n_sel, idx_bits, layer_off, lchunk):
    b = pl.program_id(0)
    past = n_pages * PAGE_SIZE
    ltot = past + LANES
    rows = keys_ref.shape[0]

    def page_copy(hbm, buf, p, s):
        page = pt_ref[b, p] + layer_off
        return pltpu.make_async_copy(hbm.at[page],
                                     buf.at[:, pl.ds(pl.multiple_of(p * PAGE_SIZE, PAGE_SIZE), PAGE_SIZE)],
                                     sem.at[s])

    def start_all(p, c):
        page_copy(cki_hbm, ki_buf, p, 0).start()
        page_copy(ck_hbm, k_buf, p, 1).start()
        page_copy(cv_hbm, v_buf, p, 2).start()
        return c
    lax.fori_loop(0, n_pages, start_all, 0)

    def wait_all(hbm, buf, s):
        def body(p, c):
            page_copy(hbm, buf, p, s).wait()
            return c
        lax.fori_loop(0, n_pages, body, 0)

    qi = qi_ref[0]
    wcol = wi_ref[0]
    wait_all(cki_hbm, ki_buf, 0)

    def idx_scores(kic, dims=NN_DIMS):
        d = lax.dot_general(qi, kic, dims, preferred_element_type=F32)
        d = jnp.maximum(d, 0.0) * wcol
        return jnp.sum(d.reshape(t_new, IDX_HEADS, d.shape[1]), axis=1)

    keys_ref[...] = jnp.full(keys_ref.shape, INT_MIN, I32)
    for c in range(past // lchunk):
        sc = idx_scores(ki_buf[:, c * lchunk:(c + 1) * lchunk].astype(BF16))
        keys_ref[0:t_new, c * lchunk:(c + 1) * lchunk] = _sort_key(sc)
    scn = idx_scores(kin_ref[0], NT_DIMS)
    tpos = lax.broadcasted_iota(I32, (t_new, LANES), 0)
    lpos = lax.broadcasted_iota(I32, (t_new, LANES), 1)
    keys_ref[0:t_new, past:ltot] = jnp.where((lpos <= tpos) & (lpos < t_new), _sort_key(scn), INT_MIN)

    sck = max(w for w in range(LANES, 8 * LANES + 1, LANES) if ltot % w == 0)
    theta = _select_threshold(keys_ref, cut_ref, ltot // sck, sck, rows, n_sel, idx_bits)
    key = keys_ref[...]
    lane = lax.broadcasted_iota(I32, (rows, ltot), 1)
    sel = ((key > theta) | ((key == theta) & (lane < cut_ref[:, :1]))) & (key != INT_MIN)
    bias = jnp.where(sel, 0.0, NEG_INF)[0:t_new]
    bias = jnp.broadcast_to(bias[:, None, :], (t_new, ATT_HEADS, ltot)).reshape(t_new * ATT_HEADS, ltot)

    qbd = qbd_ref[0]
    wait_all(ck_hbm, k_buf, 1)
    for c in range(past // lchunk):
        kc = k_buf[:, c * lchunk:(c + 1) * lchunk].astype(BF16)
        s_ref[:, c * lchunk:(c + 1) * lchunk] = jnp.dot(qbd, kc, preferred_element_type=F32)
    s_ref[:, past:ltot] = lax.dot_general(qbd, kn_ref[0], NT_DIMS, preferred_element_type=F32)
    s = s_ref[...] + bias
    m = jnp.max(s, axis=1, keepdims=True)
    p = jnp.exp(s - m)
    linv = 1.0 / jnp.sum(p, axis=1, keepdims=True)
    s_ref[...] = p
    wait_all(cv_hbm, v_buf, 2)
    o = jnp.dot(s_ref[:, past:ltot].astype(BF16), vn_ref[0], preferred_element_type=F32)
    for c in range(past // lchunk):
        vc = v_buf[:, c * lchunk:(c + 1) * lchunk].astype(BF16)
        o = o + lax.dot_general(s_ref[:, c * lchunk:(c + 1) * lchunk].astype(BF16), vc, NT_DIMS,
                                preferred_element_type=F32)
    o = o * linv
    nrow = (lax.broadcasted_iota(I32, (t_new * ATT_HEADS, ATT_HEAD_DIM), 0) % ATT_HEADS) // ATT_GROUP
    out = jnp.zeros((t_new * ATT_HEADS, ATT_HEAD_DIM), F32)
    for n in range(ATT_KV_HEADS):
        out = out + jnp.where(nrow == n, o[:, n * ATT_HEAD_DIM:(n + 1) * ATT_HEAD_DIM], 0.0)
    o_ref[0] = out.astype(o_ref.dtype)


def _dsa_sample(page_table, qs, qis, wi, kb, vb, kib, cache_ki, cache_k, cache_v, layer_off, n_sel):
    b, t, _ = qs.shape
    n_pages = page_table.shape[1]
    past = n_pages * PAGE_SIZE
    ltot = past + LANES
    rows = 8
    q5 = qs.reshape(b, t, ATT_KV_HEADS, ATT_GROUP, 1, ATT_HEAD_DIM)
    eye = jnp.eye(ATT_KV_HEADS, dtype=qs.dtype).reshape(1, 1, ATT_KV_HEADS, 1, ATT_KV_HEADS, 1)
    qbd = (q5 * eye).reshape(b, t * ATT_HEADS, ATT_KV_DIM)
    qi = qis.reshape(b, t * IDX_HEADS, IDX_DIM)
    wcol = wi.reshape(b, t * IDX_HEADS, 1)
    padr = lambda a: jnp.pad(a, ((0, 0), (0, LANES - t), (0, 0)))
    kn, vn, kin = padr(kb), padr(vb), padr(kib)
    idx_bits = max(1, math.ceil(math.log2(ltot))) + 1
    lchunk = min(1024, past)
    bl = lambda a: pl.BlockSpec((1,) + a.shape[1:], lambda i, pt: (i, 0, 0))
    anyspec = pl.BlockSpec(memory_space=pl.ANY)
    grid_spec = pltpu.PrefetchScalarGridSpec(
        num_scalar_prefetch=1,
        grid=(b,),
        in_specs=[bl(qbd), bl(qi), bl(wcol), bl(kn), bl(vn), bl(kin), anyspec, anyspec, anyspec],
        out_specs=pl.BlockSpec((1, t * ATT_HEADS, ATT_HEAD_DIM), lambda i, pt: (i, 0, 0)),
        scratch_shapes=[pltpu.VMEM((IDX_DIM, past), F32), pltpu.VMEM((ATT_KV_DIM, past), F32),
                        pltpu.VMEM((ATT_KV_DIM, past), F32), pltpu.VMEM((t * ATT_HEADS, ltot), F32),
                        pltpu.VMEM((rows, ltot), I32), pltpu.VMEM((rows, LANES), I32),
                        pltpu.SemaphoreType.DMA((3,))],
    )
    o = pl.pallas_call(
        functools.partial(_dsa_sample_kernel, n_pages=n_pages, t_new=t, n_sel=n_sel, idx_bits=idx_bits,
                          layer_off=layer_off, lchunk=lchunk),
        grid_spec=grid_spec,
        out_shape=jax.ShapeDtypeStruct((b, t * ATT_HEADS, ATT_HEAD_DIM), BF16),
        compiler_params=_cparams(("arbitrary",), 56),
        name="dsa_sample",
    )(page_table, qbd, qi, wcol, kn, vn, kin, cache_ki, cache_k, cache_v)
    return o.reshape(b, t, ATT_Q_DIM)


def _gla_proj_kernel(x_ref, wq_ref, wk_ref, wv_ref, wg_ref, wr_ref, wup_ref, bg_ref,
                     q_ref, k_ref, v_ref, la_ref, r_ref):
    xb = x_ref[...].astype(BF16)
    q_ref[...] = jnp.dot(xb, wq_ref[...], preferred_element_type=F32) * (GLA_DK ** -0.5)
    k_ref[...] = jnp.dot(xb, wk_ref[...], preferred_element_type=F32)
    v_ref[...] = jnp.dot(xb, wv_ref[...], preferred_element_type=F32)
    r_ref[...] = jnp.dot(xb, wr_ref[...], preferred_element_type=F32)
    gd = jnp.dot(xb, wg_ref[...], preferred_element_type=F32)
    glogit = jnp.dot(gd.astype(BF16), wup_ref[...], preferred_element_type=F32) + bg_ref[...]
    la_ref[...] = jax.nn.log_sigmoid(glogit) / GLA_TAU


def _gla_proj(x2d, w_in, w_gate_up, b_gate):
    m = x2d.shape[0]
    tm = min(m, 512)
    dq = GLA_HEADS * GLA_DK
    dv = GLA_HEADS * GLA_DV
    wq = w_in[:, :dq].astype(BF16)
    wk = w_in[:, dq:2 * dq].astype(BF16)
    wv = w_in[:, 2 * dq:2 * dq + dv].astype(BF16)
    wg = jnp.pad(w_in[:, 2 * dq + dv:2 * dq + dv + GLA_GATE_RANK], ((0, 0), (0, LANES - GLA_GATE_RANK))).astype(BF16)
    wr = w_in[:, 2 * dq + dv + GLA_GATE_RANK:].astype(BF16)
    wup = jnp.pad(w_gate_up, ((0, LANES - GLA_GATE_RANK), (0, 0))).astype(BF16)
    row = lambda n: pl.BlockSpec((tm, n), lambda i: (i, 0))
    full = lambda a: pl.BlockSpec(a.shape, lambda i: (0, 0))
    bg = b_gate.reshape(1, -1)
    outs = [dq, dq, dv, dq, dv]
    return pl.pallas_call(
        _gla_proj_kernel,
        grid=(m // tm,),
        in_specs=[row(D_MODEL), full(wq), full(wk), full(wv), full(wg), full(wr), full(wup), full(bg)],
        out_specs=[row(n) for n in outs],
        out_shape=[jax.ShapeDtypeStruct((m, n), F32) for n in outs],
        compiler_params=_cparams(("parallel",)),
        name="gla_proj",
    )(x2d, wq, wk, wv, wg, wr, wup, bg)


def _gla_chunk(q, k, v, g, st, c):
    sub = GLA_SUB
    n_sub = c // sub
    ri = lax.broadcasted_iota(I32, (c, c), 0)
    ci = lax.broadcasted_iota(I32, (c, c), 1)
    tri = jnp.where(ci <= ri, 1.0, 0.0).astype(F32)
    bc = jnp.dot(tri, g, preferred_element_type=F32, precision=lax.Precision.HIGHEST)
    o = lax.dot_general((q * jnp.exp(bc)).astype(BF16), st.astype(BF16), NT_DIMS, preferred_element_type=F32)
    if n_sub > 1:
        mrow = jnp.concatenate([jnp.zeros((sub, GLA_DK), F32)] +
                               [jnp.broadcast_to(bc[i * sub - 1:i * sub], (sub, GLA_DK)) for i in range(1, n_sub)], axis=0)
        qh = (q * jnp.exp(bc - mrow)).astype(BF16)
        parts = [jnp.zeros((sub, GLA_DV), F32)]
        for i in range(1, n_sub):
            kh = (k[:i * sub] * jnp.exp(bc[i * sub - 1:i * sub] - bc[:i * sub])).astype(BF16)
            a = lax.dot_general(qh[i * sub:(i + 1) * sub], kh, NT_DIMS, preferred_element_type=F32)
            parts.append(jnp.dot(a.astype(BF16), v[:i * sub].astype(BF16), preferred_element_type=F32))
        o = o + jnp.concatenate(parts, axis=0)
    rsub = lax.broadcasted_iota(I32, (c, 1), 0) % sub
    for dlt in range(sub):
        ks = k if dlt == 0 else pltpu.roll(k, dlt, 0)
        bs = bc if dlt == 0 else pltpu.roll(bc, dlt, 0)
        vs = v if dlt == 0 else pltpu.roll(v, dlt, 0)
        w = jnp.sum(q * ks * jnp.exp(jnp.minimum(bc - bs, 0.0)), axis=1, keepdims=True)
        o = o + jnp.where(rsub >= dlt, w, 0.0) * vs
    bl = bc[c - 1:c]
    kt = (k * jnp.exp(bl - bc)).astype(BF16)
    st_new = st * jnp.exp(bl) + lax.dot_general(v.astype(BF16), kt, TN_DIMS, preferred_element_type=F32)
    return o, st_new


def _gla_kernel(q_ref, k_ref, v_ref, g_ref, r_ref, ng_ref, s0_ref, o_ref, sf_ref, st_ref, *, c, n_c):
    tb = pl.program_id(2)

    @pl.when(tb == 0)
    def _():
        st_ref[...] = s0_ref[0, 0].T

    st = st_ref[...]
    for ci in range(n_c):
        sl = slice(ci * c, (ci + 1) * c)
        o, st = _gla_chunk(q_ref[0, sl, :], k_ref[0, sl, :], v_ref[0, sl, :], g_ref[0, sl, :], st, c)
        o = o * lax.rsqrt(jnp.mean(o * o, axis=-1, keepdims=True) + LN_EPS) * ng_ref[...]
        o_ref[0, sl, :] = (o * jax.nn.silu(r_ref[0, sl, :])).astype(o_ref.dtype)
    st_ref[...] = st

    @pl.when(tb == pl.num_programs(2) - 1)
    def _():
        sf_ref[0, 0] = st.T


def _gla(q, k, v, la, r, norm_g, s0):
    b, t, _ = q.shape
    c = math.gcd(t, GLA_CHUNK)
    tblk = math.gcd(t, 16 * GLA_CHUNK)
    n_c = tblk // c
    kspec = pl.BlockSpec((1, tblk, GLA_DK), lambda bi, h, i: (bi, i, h))
    vspec = pl.BlockSpec((1, tblk, GLA_DV), lambda bi, h, i: (bi, i, h))
    sspec = pl.BlockSpec((1, 1, GLA_DK, GLA_DV), lambda bi, h, i: (bi, h, 0, 0))
    return pl.pallas_call(
        functools.partial(_gla_kernel, c=c, n_c=n_c),
        grid=(b, GLA_HEADS, t // tblk),
        in_specs=[kspec, kspec, vspec, kspec, vspec, pl.BlockSpec((1, GLA_DV), lambda bi, h, i: (0, 0)), sspec],
        out_specs=[vspec, sspec],
        out_shape=[jax.ShapeDtypeStruct((b, t, GLA_HEADS * GLA_DV), BF16),
                   jax.ShapeDtypeStruct((b, GLA_HEADS, GLA_DK, GLA_DV), F32)],
        scratch_shapes=[pltpu.VMEM((GLA_DV, GLA_DK), F32)],
        compiler_params=_cparams(("parallel", "parallel", "arbitrary")),
        name="gla_scan",
    )(q, k, v, la, r, norm_g.reshape(1, -1), s0)


def _gla_mixer(x, s0, w_in, w_gate_up, b_gate, norm_g):
    b, t, _ = x.shape
    q, k, v, la, r = _gla_proj(x.reshape(b * t, D_MODEL), w_in, w_gate_up, b_gate)
    tp = -(-t // GLA_SUB) * GLA_SUB
    r3 = lambda a: jnp.pad(a.reshape(b, t, -1), ((0, 0), (0, tp - t), (0, 0)))
    o, sf = _gla(r3(q), r3(k), r3(v), r3(la), r3(r), norm_g, s0)
    return o[:, :t], sf


def _top_values(s, k):
    vals = []
    cur = s
    for _ in range(k):
        mx = jnp.max(cur, axis=0, keepdims=True)
        vals.append(mx)
        cur = jnp.where(cur == mx, NEG_INF, cur)
    return vals


def _top_values_ranked(s, k):
    vals = []
    cur = s
    rank = jnp.full(s.shape, float(k), F32)
    for i in range(k):
        mx = jnp.max(cur, axis=0, keepdims=True)
        vals.append(mx)
        top = cur == mx
        rank = jnp.where(top, float(i), rank)
        cur = jnp.where(top, NEG_INF, cur)
    return vals, rank


def _peer_select_kernel(x_ref, wq_ref, k1_ref, k2_ref, c1_ref, p_ref, r2_ref, e_ref):
    xb = x_ref[...].astype(BF16)
    q = jnp.dot(xb, wq_ref[...], preferred_element_type=F32).astype(BF16)
    tt = q.shape[0]
    k1 = k1_ref[...]
    k2 = k2_ref[...]
    sub8 = lax.broadcasted_iota(I32, (8, tt), 0)
    for h in range(PEER_HEADS):
        s1 = lax.dot_general(k1, q[:, h * PEER_D_KEY:h * PEER_D_KEY + PEER_HALF], NT_DIMS,
                             preferred_element_type=F32)
        s2 = lax.dot_general(k2, q[:, h * PEER_D_KEY + PEER_HALF:(h + 1) * PEER_D_KEY], NT_DIMS,
                             preferred_element_type=F32)
        t1, rank1 = _top_values_ranked(s1, PEER_TOPK)
        t2, rank2 = _top_values_ranked(s2, PEER_TOPK)
        t2a = jnp.concatenate(t2, axis=0)
        cands = [t1[0] + t2a]
        for a in range(1, PEER_TOPK):
            nb = PEER_TOPK // (a + 1)
            cands.append(jnp.where(sub8 < nb, t1[a] + t2a[:8], NEG_INF))
        best = _top_values(jnp.concatenate(cands, axis=0), PEER_TOPK)
        theta = best[PEER_TOPK - 1]
        z = best[0] * 0.0
        for bv in best:
            z = z + jnp.exp(bv - best[0])
        count1 = jnp.zeros(s1.shape, F32)
        for a in range(PEER_TOPK):
            cnt_a = jnp.sum(jnp.where(t1[a] + t2a >= theta, 1.0, 0.0), axis=0, keepdims=True)
            count1 = jnp.where(rank1 == float(a), cnt_a, count1)
        c1_ref[h] = count1
        p_ref[h] = jnp.exp(s1 - t1[0]) * (0.5 / z)
        r2_ref[h] = rank2
        e_ref[h] = jnp.exp(s2 - t2[0])


def _peer_select(x2d, w_q, keys1, keys2):
    n = x2d.shape[0]
    tt = min(n, 256)
    wq = w_q.astype(BF16)
    k1 = keys1.astype(BF16)
    k2 = keys2.astype(BF16)
    full = lambda a: pl.BlockSpec(a.shape, lambda i: (0,) * a.ndim)
    hspec = pl.BlockSpec((PEER_HEADS, PEER_N_KEYS, tt), lambda i: (0, 0, i))
    big = lambda dt: jax.ShapeDtypeStruct((PEER_HEADS, PEER_N_KEYS, n), dt)
    return pl.pallas_call(
        _peer_select_kernel,
        grid=(n // tt,),
        in_specs=[pl.BlockSpec((tt, D_MODEL), lambda i: (i, 0)), full(wq), full(k1), full(k2)],
        out_specs=[hspec, hspec, hspec, hspec],
        out_shape=[big(F32), big(F32), big(F32), big(F32)],
        compiler_params=_cparams(("parallel",)),
        name="peer_select",
    )(x2d, wq, k1, k2)


def _peer_mix_kernel(h_ref, c1_ref, p_ref, r2_ref, e_ref, u_ref, vt_ref, g_ref, bb_ref, out_ref,
                     xb_ref, ht_ref, wt_ref, yt_ref, *, n_i1):
    eb = pl.program_id(1)
    tt = h_ref.shape[0]

    @pl.when(eb == 0)
    def _():
        xb_ref[...] = h_ref[...].T.astype(BF16)
        yt_ref[...] = jnp.zeros(yt_ref.shape, F32)

    ht_ref[...] = jnp.dot(u_ref[...], xb_ref[...], preferred_element_type=F32)

    i1_base = pl.multiple_of(eb * n_i1, n_i1)

    for lg in range(tt // LANES):
        ls = slice(lg * LANES, (lg + 1) * LANES)
        c8 = [c1_ref[h, pl.ds(i1_base, n_i1), ls] for h in range(PEER_HEADS)]
        p8 = [p_ref[h, pl.ds(i1_base, n_i1), ls] for h in range(PEER_HEADS)]
        for j0 in range(0, n_i1, 2):
            pair = (j0, j0 + 1)
            cb = {j: [jnp.broadcast_to(c8[h][j:j + 1], (8, LANES)) for h in range(PEER_HEADS)] for j in pair}
            pb = {j: [jnp.broadcast_to(p8[h][j:j + 1], (8, LANES)) for h in range(PEER_HEADS)] for j in pair}
            for r in range(PEER_N_KEYS // BF16_ROWS):
                halves = {j: [] for j in pair}
                for k0 in (r * BF16_ROWS, r * BF16_ROWS + 8):
                    ks = slice(k0, k0 + 8)
                    acc = {j: jnp.zeros((8, LANES), F32) for j in pair}
                    for h in range(PEER_HEADS):
                        rk = r2_ref[h, ks, ls]
                        ev = e_ref[h, ks, ls]
                        for j in pair:
                            acc[j] = acc[j] + jnp.where(rk < cb[j][h], pb[j][h] * ev, 0.0)
                    for j in pair:
                        halves[j].append(acc[j])
                for j in pair:
                    rs = slice(j * PEER_N_KEYS + r * BF16_ROWS, j * PEER_N_KEYS + (r + 1) * BF16_ROWS)
                    x = ht_ref[rs, ls]
                    gelu2 = x * (1.0 + lax.erf(x * (2.0 ** -0.5)))
                    wt_ref[rs, ls] = (gelu2 * jnp.concatenate(halves[j], axis=0)).astype(BF16)

    yt_ref[...] += jnp.dot(vt_ref[...], wt_ref[...], preferred_element_type=F32)

    @pl.when(eb == pl.num_programs(1) - 1)
    def _():
        y = yt_ref[...].T
        out_ref[...] = _layer_norm(DEEPNORM_ALPHA * h_ref[...] + y, g_ref[...], bb_ref[...])


def _block_transpose_kernel(v_ref, o_ref):
    o_ref[...] = v_ref[...].T.astype(o_ref.dtype)


def _cast_kernel(x_ref, o_ref):
    o_ref[...] = x_ref[...].astype(o_ref.dtype)


def _key_table(u_all, layer, eblk):
    _, n_e, d = u_all.shape
    return pl.pallas_call(
        _cast_kernel,
        grid=(n_e // eblk,),
        in_specs=[pl.BlockSpec((None, eblk, d), lambda j: (layer, j, 0))],
        out_specs=pl.BlockSpec((eblk, d), lambda j: (j, 0)),
        out_shape=jax.ShapeDtypeStruct((n_e, d), BF16),
        compiler_params=_cparams(("parallel",)),
        name="key_table",
    )(u_all)


def _value_blocks(v_all, layer, eblk):
    _, n_e, d = v_all.shape
    return pl.pallas_call(
        _block_transpose_kernel,
        grid=(n_e // eblk,),
        in_specs=[pl.BlockSpec((None, eblk, d), lambda j: (layer, j, 0))],
        out_specs=pl.BlockSpec((None, d, eblk), lambda j: (j, 0, 0)),
        out_shape=jax.ShapeDtypeStruct((n_e // eblk, d, eblk), BF16),
        compiler_params=_cparams(("parallel",)),
        name="value_blocks",
    )(v_all)


def _peer_ln(h2d, w_q, keys1, keys2, ub, vt, g, b):
    n_real = h2d.shape[0]
    if n_real % LANES:
        h2d = jnp.pad(h2d, ((0, LANES - n_real % LANES), (0, 0)))
        return _peer_ln(h2d, w_q, keys1, keys2, ub, vt, g, b)[:n_real]
    n = n_real
    c1, p, r2, e = _peer_select(h2d, w_q, keys1, keys2)
    tt = min(n, 512)
    n_blk, _, eblk = vt.shape
    hspec = pl.BlockSpec((PEER_HEADS, PEER_N_KEYS, tt), lambda i, j: (0, 0, i))
    vec = pl.BlockSpec((1, D_MODEL), lambda i, j: (0, 0))
    return pl.pallas_call(
        functools.partial(_peer_mix_kernel, n_i1=eblk // PEER_N_KEYS),
        grid=(n // tt, n_blk),
        in_specs=[pl.BlockSpec((tt, D_MODEL), lambda i, j: (i, 0)), hspec, hspec, hspec, hspec,
                  pl.BlockSpec((eblk, D_MODEL), lambda i, j: (j, 0)),
                  pl.BlockSpec((None, D_MODEL, eblk), lambda i, j: (j, 0, 0)), vec, vec],
        out_specs=pl.BlockSpec((tt, D_MODEL), lambda i, j: (i, 0)),
        out_shape=jax.ShapeDtypeStruct((n, D_MODEL), F32),
        scratch_shapes=[pltpu.VMEM((D_MODEL, tt), BF16), pltpu.VMEM((eblk, tt), F32),
                        pltpu.VMEM((eblk, tt), BF16), pltpu.VMEM((D_MODEL, tt), F32)],
        compiler_params=_cparams(("parallel", "arbitrary"), 56),
        name="peer_mix",
    )(h2d, c1, p, r2, e, ub, vt, g.reshape(1, -1), b.reshape(1, -1))


def kernel(x_prompt, x_sample, cache_k, cache_v, cache_kidx, state_gla, page_table,
           attn_w_in, attn_w_out, gla_w_in, gla_w_gate_up, gla_b_gate, gla_norm_g, gla_w_out,
           peer_w_q, peer_keys1, peer_keys2, peer_u, peer_v, ln1_g, ln1_b, ln2_g, ln2_b):
    bp, tp, _ = x_prompt.shape
    bs, ts, _ = x_sample.shape
    n_pool = cache_k.shape[1]
    past = page_table.shape[1] * PAGE_SIZE
    ck = jnp.swapaxes(cache_k.reshape(-1, PAGE_SIZE, ATT_KV_DIM), 1, 2)
    cv = jnp.swapaxes(cache_v.reshape(-1, PAGE_SIZE, ATT_KV_DIM), 1, 2)
    cki = jnp.swapaxes(cache_kidx.reshape(-1, PAGE_SIZE, IDX_DIM), 1, 2)
    hp = x_prompt.reshape(bp * tp, D_MODEL)
    hs = x_sample.reshape(bs * ts, D_MODEL)
    kp_l, vp_l, kip_l, sp_l = [], [], [], []
    ks_l, vs_l, kis_l, ss_l = [], [], [], []
    for i in range(DEPTH):
        j = i // 2
        if i % 2 == 0:
            k, v, ki, kb, kib, qt, qit, wit, vbt = _attn_proj_t(hp, attn_w_in[j])
            r3 = lambda a: a.reshape(bp, tp, -1)
            op = _dsa_prompt(qt, qit, wit, r3(kb), vbt, r3(kib), bp, min(TOPK_MAX, tp // 4))
            kp_l.append(k.reshape(bp, tp, ATT_KV_HEADS, ATT_HEAD_DIM))
            vp_l.append(v.reshape(bp, tp, ATT_KV_HEADS, ATT_HEAD_DIM))
            kip_l.append(ki.reshape(bp, tp, IDX_DIM))
            qs, k, v, kb, vb, qis, ki, kib, wi = _attn_proj(hs, attn_w_in[j])
            r3 = lambda a: a.reshape(bs, ts, -1)
            os_ = _dsa_sample(page_table, r3(qs), r3(qis), r3(wi), r3(kb), r3(vb), r3(kib), cki, ck, cv,
                              j * n_pool, min(TOPK_MAX, (past + ts) // 4))
            ks_l.append(k.reshape(bs, ts, ATT_KV_HEADS, ATT_HEAD_DIM))
            vs_l.append(v.reshape(bs, ts, ATT_KV_HEADS, ATT_HEAD_DIM))
            kis_l.append(ki.reshape(bs, ts, IDX_DIM))
            w_out = attn_w_out[j]
        else:
            s0 = jnp.zeros((bp, GLA_HEADS, GLA_DK, GLA_DV), F32)
            op, sp = _gla_mixer(hp.reshape(bp, tp, D_MODEL), s0, gla_w_in[j], gla_w_gate_up[j], gla_b_gate[j],
                                gla_norm_g[j])
            os_, ss = _gla_mixer(hs.reshape(bs, ts, D_MODEL), state_gla[j], gla_w_in[j], gla_w_gate_up[j],
                                 gla_b_gate[j], gla_norm_g[j])
            sp_l.append(sp)
            ss_l.append(ss)
            w_out = gla_w_out[j]
        hp = _outproj_ln(op.reshape(bp * tp, -1), hp, w_out, ln1_g[i], ln1_b[i])
        hs = _outproj_ln(os_.reshape(bs * ts, -1), hs, w_out, ln1_g[i], ln1_b[i])
        pw = (peer_w_q[i], peer_keys1[i], peer_keys2[i], _key_table(peer_u, i, PEER_EXPERT_BLOCK),
              _value_blocks(peer_v, i, PEER_EXPERT_BLOCK), ln2_g[i], ln2_b[i])
        hp = _peer_ln(hp, *pw)
        hs = _peer_ln(hs, *pw)
    return (hp.reshape(bp, tp, D_MODEL), hs.reshape(bs, ts, D_MODEL),
            jnp.stack(kp_l), jnp.stack(vp_l), jnp.stack(kip_l), jnp.stack(sp_l),
            jnp.stack(ks_l), jnp.stack(vs_l), jnp.stack(kis_l), jnp.stack(ss_l))
```
